```python
import math
import jax
import jax.numpy as jnp
from jax import lax
import numpy as np

D_MODEL = 1024
BATCH = 32
SEQ = 256
DEPTH = 2
DEC_BATCH = 4
DEC_SEQ = 2048
PAST_LEN = 256

GRID_W = 64
MIX_W = D_MODEL // 2
N_BRANCH = 3
A_HEADS = 4
A_HEAD_DIM = MIX_W // (2 * A_HEADS)
ROPE_THETA = 10000.0
Q_BLOCK = 128
B_HEADS = 4
B_VDIM = MIX_W // B_HEADS
B_KDIM = B_VDIM // 2
GLA_RANK = 16
GLA_TAU = 16.0
GLA_CHUNK = 32
S5_W = MIX_W
S5_GROUP = 16
S5_G = S5_W // S5_GROUP
S5_P = 64
FFN_DIM = -(-8 * D_MODEL // (3 * 256)) * 256
EPS = 1e-6
IN_WIDTHS = (MIX_W, MIX_W, MIX_W, B_HEADS * B_KDIM, B_HEADS * B_KDIM, MIX_W, MIX_W, 2 * GLA_RANK, S5_W, N_BRANCH * D_MODEL)
IN_DIM = sum(IN_WIDTHS)

kernel_name = "hybrid_diffattn_gla_s5_dit_step"


def rms_norm(x, g):
    xf = x.astype(jnp.float32)
    return xf * lax.rsqrt(jnp.mean(xf * xf, axis=-1, keepdims=True) + EPS) * g


def split_cols(z):
    idx = [int(i) for i in np.cumsum(IN_WIDTHS)[:-1]]
    return jnp.split(z, idx, axis=-1)


def axial_rope_tables(n_tok):
    rows = n_tok // GRID_W
    t = jnp.arange(rows * GRID_W)
    row = (t // GRID_W).astype(jnp.float32)
    col = (t % GRID_W).astype(jnp.float32)
    half = A_HEAD_DIM // 2
    inv = ROPE_THETA ** (-jnp.arange(0, half, 2, dtype=jnp.float32) / half)
    ang = jnp.concatenate([row[:, None] * inv, col[:, None] * inv], axis=-1)
    return jnp.cos(ang), jnp.sin(ang)


def apply_axial_rope(x, cos, sin):
    half = A_HEAD_DIM // 2
    q4 = half // 2
    c = cos[None, :, None, None, :]
    s = sin[None, :, None, None, :]

    def rot(xp, cp, sp):
        x1, x2 = xp[..., :q4], xp[..., q4:]
        return jnp.concatenate([x1 * cp - x2 * sp, x2 * cp + x1 * sp], axis=-1)

    return jnp.concatenate([rot(x[..., :half], c[..., :q4], s[..., :q4]),
                            rot(x[..., half:], c[..., q4:], s[..., q4:])], axis=-1)


def diff_attention(q, k, v, lam):
    bsz, lq, nh, _, d = q.shape
    nb = lq // Q_BLOCK
    qb = q.astype(jnp.float32).reshape(bsz, nb, Q_BLOCK, nh, 2, d).swapaxes(0, 1)
    kf = k.astype(jnp.float32)
    vf = v.astype(jnp.float32)
    scale = d ** -0.5

    def block(qblk):
        s = jnp.einsum('bqhmd,bkhmd->bhmqk', qblk, kf) * scale
        p = jax.nn.softmax(s, axis=-1)
        w = p[:, :, 0] - lam * p[:, :, 1]
        return jnp.einsum('bhqk,bkhe->bqhe', w, vf)

    o = lax.map(block, qb)
    return o.swapaxes(0, 1).reshape(bsz, lq, nh, 2 * d)


def gla_chunked(q, k, v, log_a, s0):
    bsz, n_tok, nh, dk = q.shape
    dv = v.shape[-1]
    n = n_tok // GLA_CHUNK
    f32 = jnp.float32
    q = q.astype(f32).reshape(bsz, n, GLA_CHUNK, nh, dk)
    k = k.astype(f32).reshape(bsz, n, GLA_CHUNK, nh, dk)
    v = v.astype(f32).reshape(bsz, n, GLA_CHUNK, nh, dv)
    bc = jnp.cumsum(log_a.astype(f32).reshape(bsz, n, GLA_CHUNK, nh, dk), axis=2)
    rel = bc[:, :, :, None] - bc[:, :, None, :]
    causal = jnp.tril(jnp.ones((GLA_CHUNK, GLA_CHUNK), dtype=bool))[None, None, :, :, None, None]
    decay = jnp.where(causal, jnp.exp(jnp.minimum(rel, 0.0)), 0.0)
    attn = jnp.einsum('bnthk,bnshk,bntshk->bnhts', q, k, decay)
    intra = jnp.einsum('bnhts,bnshv->bnthv', attn, v)
    b_last = bc[:, :, -1]
    kdec = k * jnp.exp(b_last[:, :, None] - bc)
    ds = jnp.einsum('bnchk,bnchv->nbhkv', kdec, v)

    def step(state, inp):
        dec, d_state = inp
        return dec[..., None] * state + d_state, state

    s_final, s_start = lax.scan(step, s0.astype(f32), (jnp.exp(b_last).swapaxes(0, 1), ds))
    inter = jnp.einsum('bnchk,nbhkv->bnchv', q * jnp.exp(bc), s_start)
    return (intra + inter).reshape(bsz, n_tok, nh, dv), s_final


def complex_affine_combine(e1, e2):
    a1r, a1i, b1r, b1i = e1
    a2r, a2i, b2r, b2i = e2
    return (a2r * a1r - a2i * a1i, a2r * a1i + a2i * a1r,
            a2r * b1r - a2i * b1i + b2r, a2r * b1i + a2i * b1r + b2i)


def s5_scan(u, lam_re, lam_im, log_dt, b_re, b_im, c_re, c_im, h0, reverse):
    f32 = jnp.float32
    dt = jnp.exp(log_dt.astype(f32))[:, None]
    lr = lam_re.astype(f32)
    li = lam_im.astype(f32)
    mag = jnp.exp(lr * dt)
    ar = mag * jnp.cos(li * dt)
    ai = mag * jnp.sin(li * dt)
    den = lr * lr + li * li
    fr = ((ar - 1.0) * lr + ai * li) / den
    fi = (ai * lr - (ar - 1.0) * li) / den
    bbr = fr[..., None] * b_re - fi[..., None] * b_im
    bbi = fr[..., None] * b_im + fi[..., None] * b_re
    bu_r = jnp.einsum('gpi,blgi->blgp', bbr, u)
    bu_i = jnp.einsum('gpi,blgi->blgp', bbi, u)
    a_r = jnp.broadcast_to(ar, bu_r.shape)
    a_i = jnp.broadcast_to(ai, bu_r.shape)
    pr, pim, hr, hi = lax.associative_scan(complex_affine_combine, (a_r, a_i, bu_r, bu_i), axis=1, reverse=reverse)
    h0 = h0.astype(f32)
    h0r = h0[:, 0][:, None]
    h0i = h0[:, 1][:, None]
    hr = pr * h0r - pim * h0i + hr
    hi = pr * h0i + pim * h0r + hi
    y = jnp.einsum('gip,blgp->blgi', c_re, hr) - jnp.einsum('gip,blgp->blgi', c_im, hi)
    idx = 0 if reverse else -1
    return y, jnp.stack([hr[:, idx], hi[:, idx]], axis=1)


def trunk_layer(x, cond, p, lam_init, ctx, rope):
    bsz, n_tok, _ = x.shape
    dtype = x.dtype
    f32 = jnp.float32
    mod = jax.nn.silu(cond.astype(f32)) @ p['w_mod'] + p['b_mod']
    sh1, sc1, g1, sh2, sc2, g2 = jnp.split(mod[:, None, :], 6, axis=-1)
    h = rms_norm(x, p['norm1_g']) * (1.0 + sc1) + sh1
    aq, ak, av, bq, bk, bv, bg, br, cu, gz = split_cols(h @ p['w_in'])

    aq = rms_norm(aq.reshape(bsz, n_tok, A_HEADS, 2, A_HEAD_DIM), p['diff_qn_g'])
    ak = rms_norm(ak.reshape(bsz, n_tok, A_HEADS, 2, A_HEAD_DIM), p['diff_kn_g'])
    av = av.reshape(bsz, n_tok, A_HEADS, 2 * A_HEAD_DIM)
    lv = p['diff_lam']
    lam = jnp.exp(jnp.sum(lv[0] * lv[1])) - jnp.exp(jnp.sum(lv[2] * lv[3])) + lam_init
    if ctx is None:
        keys, vals = ak, av
    else:
        cos, sin = rope
        aq = apply_axial_rope(aq, cos, sin)
        keys = jnp.concatenate([ctx['k'], apply_axial_rope(ak, cos, sin)], axis=1)
        vals = jnp.concatenate([ctx['v'], av], axis=1)
    oa = diff_attention(aq, keys, vals, lam)
    oa = (rms_norm(oa, p['diff_subln_g']) * (1.0 - lam_init)).reshape(bsz, n_tok, MIX_W)

    bq = bq.reshape(bsz, n_tok, B_HEADS, B_KDIM) * (B_KDIM ** -0.5)
    bk = bk.reshape(bsz, n_tok, B_HEADS, B_KDIM)
    bv = bv.reshape(bsz, n_tok, B_HEADS, B_VDIM)
    r_f, r_b = jnp.split(br, 2, axis=-1)
    la_f = (jax.nn.log_sigmoid(r_f @ p['gla_wa2'][0] + p['gla_ba'][0]) / GLA_TAU).reshape(bsz, n_tok, B_HEADS, B_KDIM)
    la_b = (jax.nn.log_sigmoid(r_b @ p['gla_wa2'][1] + p['gla_ba'][1]) / GLA_TAU).reshape(bsz, n_tok, B_HEADS, B_KDIM)
    s0 = jnp.zeros((bsz, 2, B_HEADS, B_KDIM, B_VDIM), f32) if ctx is None else ctx['gla']
    flip = lambda t: jnp.flip(t, axis=1)
    o_f, s_f = gla_chunked(bq, bk, bv, la_f, s0[:, 0])
    o_b, s_b = gla_chunked(flip(bq), flip(bk), flip(bv), flip(la_b), s0[:, 1])
    ob = rms_norm(o_f + flip(o_b), p['gla_on_g']) * jax.nn.silu(bg.reshape(bsz, n_tok, B_HEADS, B_VDIM))
    ob = ob.reshape(bsz, n_tok, MIX_W)

    u = cu.astype(f32).reshape(bsz, n_tok, S5_G, S5_GROUP)
    h0 = jnp.zeros((bsz, 2, 2, S5_G, S5_P), f32) if ctx is None else ctx['s5']
    y_f, hf = s5_scan(u, p['s5_lam_re'][0], p['s5_lam_im'][0], p['s5_log_dt'][0], p['s5_b_re'][0], p['s5_b_im'][0],
                      p['s5_c_re'][0], p['s5_c_im'][0], h0[:, 0], False)
    y_b, hb = s5_scan(u, p['s5_lam_re'][1], p['s5_lam_im'][1], p['s5_log_dt'][1], p['s5_b_re'][1], p['s5_b_im'][1],
                      p['s5_c_re'][1], p['s5_c_im'][1], h0[:, 1], True)
    yc = jax.nn.gelu((y_f + y_b).reshape(bsz, n_tok, S5_W) + p['s5_d'] * cu)
    glu_a, glu_b = jnp.split(yc @ p['s5_w_glu'] + p['s5_b_glu'], 2, axis=-1)
    oc = glu_a * jax.nn.sigmoid(glu_b)

    branches = jnp.einsum('blrm,rmd->blrd', jnp.stack([oa, ob, oc], axis=2), p['w_branch'])
    gates = jax.nn.sigmoid(gz.reshape(bsz, n_tok, N_BRANCH, D_MODEL))
    merged = jnp.sum(gates * branches, axis=2)
    x = x + g1 * (merged @ p['w_out'])

    h = rms_norm(x, p['norm2_g']) * (1.0 + sc2) + sh2
    x = x + g2 * ((jax.nn.silu(h @ p['w_ffn_gate']) * (h @ p['w_ffn_up'])) @ p['w_ffn_down'])
    x = x.astype(dtype)
    if ctx is None:
        return x, (ak, av, jnp.stack([s_f, s_b], axis=1), jnp.stack([hf, hb], axis=1))
    return x, None


def setup_inputs(seed: int = 0) -> dict:
    key = jax.random.key(seed)
    keys = jax.random.split(key, 48)
    counter = iter(range(48))
    f32 = jnp.float32

    def nrm(shape, scale):
        return jax.random.normal(keys[next(counter)], shape, f32) * scale

    def gain(shape):
        return 1.0 + nrm(shape, 0.02)

    return {
        'x_prompt': nrm((BATCH, SEQ, D_MODEL), 1.0),
        'x_sample': nrm((DEC_BATCH, DEC_SEQ, D_MODEL), 1.0),
        'cache_diff_k': nrm((DEC_BATCH, DEPTH, PAST_LEN, A_HEADS, 2, A_HEAD_DIM), 1.0),
        'cache_diff_v': nrm((DEC_BATCH, DEPTH, PAST_LEN, A_HEADS, 2 * A_HEAD_DIM), 1.0),
        'state_gla': nrm((DEC_BATCH, DEPTH, 2, B_HEADS, B_KDIM, B_VDIM), 1.0),
        'state_s5': nrm((DEC_BATCH, DEPTH, 2, 2, S5_G, S5_P), 0.1),
        'c': nrm((DEC_BATCH, D_MODEL), 1.0),
        'c_ctx': nrm((D_MODEL,), 1.0),
        'w_mod': nrm((DEPTH, D_MODEL, 6 * D_MODEL), D_MODEL ** -0.5),
        'b_mod': nrm((DEPTH, 6 * D_MODEL), 0.02),
        'norm1_g': gain((DEPTH, D_MODEL)),
        'norm2_g': gain((DEPTH, D_MODEL)),
        'w_in': nrm((DEPTH, D_MODEL, IN_DIM), D_MODEL ** -0.5),
        'diff_qn_g': gain((DEPTH, A_HEAD_DIM)),
        'diff_kn_g': gain((DEPTH, A_HEAD_DIM)),
        'diff_lam': nrm((DEPTH, 4, A_HEAD_DIM), 0.1),
        'diff_subln_g': gain((DEPTH, 2 * A_HEAD_DIM)),
        'gla_wa2': nrm((DEPTH, 2, GLA_RANK, B_HEADS * B_KDIM), GLA_RANK ** -0.5),
        'gla_ba': nrm((DEPTH, 2, B_HEADS * B_KDIM), 0.1),
        'gla_on_g': gain((DEPTH, B_VDIM)),
        's5_lam_re': -0.5 + nrm((DEPTH, 2, S5_G, S5_P), 0.01),
        's5_lam_im': math.pi * jnp.arange(S5_P, dtype=f32) + nrm((DEPTH, 2, S5_G, S5_P), 0.01),
        's5_log_dt': jax.random.uniform(keys[next(counter)], (DEPTH, 2, S5_G), f32, minval=math.log(1e-3), maxval=math.log(1e-1)),
        's5_b_re': nrm((DEPTH, 2, S5_G, S5_P, S5_GROUP), (2 * S5_GROUP) ** -0.5),
        's5_b_im': nrm((DEPTH, 2, S5_G, S5_P, S5_GROUP), (2 * S5_GROUP) ** -0.5),
        's5_c_re': nrm((DEPTH, 2, S5_G, S5_GROUP, S5_P), S5_P ** -0.5),
        's5_c_im': nrm((DEPTH, 2, S5_G, S5_GROUP, S5_P), S5_P ** -0.5),
        's5_d': nrm((DEPTH, S5_W), 1.0),
        's5_w_glu': nrm((DEPTH, S5_W, 2 * S5_W), S5_W ** -0.5),
        's5_b_glu': nrm((DEPTH, 2 * S5_W), 0.02),
        'w_branch': nrm((DEPTH, N_BRANCH, MIX_W, D_MODEL), MIX_W ** -0.5),
        'w_out': nrm((DEPTH, D_MODEL, D_MODEL), D_MODEL ** -0.5),
        'w_ffn_gate': nrm((DEPTH, D_MODEL, FFN_DIM), D_MODEL ** -0.5),
        'w_ffn_up': nrm((DEPTH, D_MODEL, FFN_DIM), D_MODEL ** -0.5),
        'w_ffn_down': nrm((DEPTH, FFN_DIM, D_MODEL), FFN_DIM ** -0.5),
    }


def reference(x_prompt, x_sample, cache_diff_k, cache_diff_v, state_gla, state_s5, c, c_ctx,
              w_mod, b_mod, norm1_g, norm2_g, w_in, diff_qn_g, diff_kn_g, diff_lam, diff_subln_g,
              gla_wa2, gla_ba, gla_on_g, s5_lam_re, s5_lam_im, s5_log_dt, s5_b_re, s5_b_im,
              s5_c_re, s5_c_im, s5_d, s5_w_glu, s5_b_glu, w_branch, w_out, w_ffn_gate, w_ffn_up, w_ffn_down):
    weights = dict(w_mod=w_mod, b_mod=b_mod, norm1_g=norm1_g, norm2_g=norm2_g, w_in=w_in,
                   diff_qn_g=diff_qn_g, diff_kn_g=diff_kn_g, diff_lam=diff_lam, diff_subln_g=diff_subln_g,
                   gla_wa2=gla_wa2, gla_ba=gla_ba, gla_on_g=gla_on_g,
                   s5_lam_re=s5_lam_re, s5_lam_im=s5_lam_im, s5_log_dt=s5_log_dt,
                   s5_b_re=s5_b_re, s5_b_im=s5_b_im, s5_c_re=s5_c_re, s5_c_im=s5_c_im,
                   s5_d=s5_d, s5_w_glu=s5_w_glu, s5_b_glu=s5_b_glu,
                   w_branch=w_branch, w_out=w_out,
                   w_ffn_gate=w_ffn_gate, w_ffn_up=w_ffn_up, w_ffn_down=w_ffn_down)
    cond_ctx = jnp.broadcast_to(c_ctx, (x_prompt.shape[0], c_ctx.shape[-1]))
    rope = axial_rope_tables(x_sample.shape[1])
    y_prompt, y_sample = x_prompt, x_sample
    k_list, v_list, gla_list, s5_list = [], [], [], []
    for l in range(DEPTH):
        p = {name: w[l] for name, w in weights.items()}
        lam_init = 0.8 - 0.6 * math.exp(-0.3 * l)
        y_prompt, (k_l, v_l, g_l, s_l) = trunk_layer(y_prompt, cond_ctx, p, lam_init, None, None)
        k_list.append(k_l)
        v_list.append(v_l)
        gla_list.append(g_l)
        s5_list.append(s_l)
        ctx = dict(k=cache_diff_k[:, l], v=cache_diff_v[:, l], gla=state_gla[:, l], s5=state_s5[:, l])
        y_sample, _ = trunk_layer(y_sample, c, p, lam_init, ctx, rope)
    new_diff_k = jnp.stack(k_list, axis=1)
    new_diff_v = jnp.stack(v_list, axis=1)
    new_gla_state = jnp.stack(gla_list, axis=1)
    new_s5_state = jnp.stack(s5_list, axis=1)
    return (y_prompt, y_sample, new_diff_k, new_diff_v, new_gla_state, new_s5_state)
```

```python
import functools
import math

import numpy as np
import jax
import jax.numpy as jnp
from jax import lax
from jax.experimental import pallas as pl
from jax.experimental.pallas import tpu as pltpu

F32 = jnp.float32
BF16 = jnp.bfloat16

D_MODEL = 1024
MIX_W = 512
A_HEADS = 4
A_HEAD_DIM = 64
GRID_W = 64
ROPE_THETA = 10000.0
B_HEADS = 4
B_KDIM = 64
B_VDIM = 128
GLA_RANK = 16
GLA_TAU = 16.0
S5_G = 32
S5_GROUP = 16
S5_P = 64
FFN_DIM = 2816
EPS = 1e-6

VMEM_LIMIT_BYTES = 56 * 1024 * 1024
TOKEN_TILE = 256
GLA_CHUNK = 128
S5_CHUNK = 16
SUBLANES = 8

OFF_AQ, OFF_AK, OFF_AV, OFF_BQK, OFF_BV, OFF_BG, OFF_CU, OFF_GZ, OFF_BR = (
    0, 512, 1024, 1536, 2048, 2560, 3072, 3584, 6656)
BR_PAD = 128
IN_COLS = OFF_BR + BR_PAD


def _cparams(*sem):
    return pltpu.CompilerParams(dimension_semantics=sem, vmem_limit_bytes=VMEM_LIMIT_BYTES)


def _resident(shape):
    nd = len(shape)
    return pl.BlockSpec(shape, lambda *_: (0,) * nd, pipeline_mode=pl.Buffered(1))


def _split_bf16(x):
    hi = x.astype(BF16)
    lo = (x - hi.astype(F32)).astype(BF16)
    return hi, lo


def _dot(a, b):
    return jnp.dot(a, b, preferred_element_type=F32)


def _dot_nt(a, b):
    return lax.dot_general(a, b, (((1,), (1,)), ((), ())), preferred_element_type=F32)


def _dot_tn(a, b):
    return lax.dot_general(a, b, (((0,), (0,)), ((), ())), preferred_element_type=F32)


def _dot_f32(a, b, nt=False):
    d = _dot_nt if nt else _dot
    ah, al = _split_bf16(a)
    bh, bl = _split_bf16(b)
    return d(ah, bh) + d(ah, bl) + d(al, bh)


def _rms(x):
    return x * lax.rsqrt(jnp.mean(x * x, axis=-1, keepdims=True) + EPS)


def _mod_kernel(cond_ref, w_ref, b_ref, o_ref):
    cnd = cond_ref[...]
    s = (cnd * jax.nn.sigmoid(cnd))
    o_ref[...] = _dot_f32(s, w_ref[...]) + b_ref[...]


def _modulation(cond8, w_mod, b_mod):
    depth, d, n = w_mod.shape
    tn = 1536
    return pl.pallas_call(
        _mod_kernel,
        grid=(depth, n // tn),
        in_specs=[pl.BlockSpec((SUBLANES, d), lambda l, j: (0, 0)),
                  pl.BlockSpec((None, d, tn), lambda l, j: (l, 0, j)),
                  pl.BlockSpec((None, 1, tn), lambda l, j: (l, 0, j))],
        out_specs=pl.BlockSpec((None, SUBLANES, tn), lambda l, j: (l, 0, j)),
        out_shape=jax.ShapeDtypeStruct((depth, SUBLANES, n), F32),
        compiler_params=_cparams("parallel", "parallel"),
        name="modulation",
    )(cond8, w_mod, b_mod.reshape(depth, 1, n))


def _group_rms64(z, gmat, gain):
    hi, lo = _split_bf16(z * z)
    ms = _dot(hi, gmat) + _dot(lo, gmat)
    return z * lax.rsqrt(ms + EPS) * gain


def _rope(z, c, s):
    n = z.shape[-1]
    lane = lax.broadcasted_iota(jnp.int32, z.shape, 1)
    first = (lane & 31) < 16
    partner = jnp.where(first, pltpu.roll(z, n - 16, 1), pltpu.roll(z, 16, 1))
    return z * c + partner * s


def _inproj_kernel(rope, x_ref, mod_ref, n1_ref, w_ref, gmat_ref, qg_ref, kg_ref, wa_ref, ba_ref, *rest):
    if rope:
        cos_ref, sin_ref = rest[:2]
        rest = rest[2:]
    q_ref, k_ref, v_ref, gqk_ref, gv_ref, gg_ref, la_ref, cu_ref, gate_ref = rest
    mod = mod_ref[...]
    sh1 = mod[:, 0:D_MODEL]
    sc1 = mod[:, D_MODEL:2 * D_MODEL]
    h = _rms(x_ref[...]) * n1_ref[...] * (1.0 + sc1) + sh1
    hb = h.astype(BF16)
    gmat = gmat_ref[...]

    def seg(a, b):
        return _dot(hb, w_ref[:, a:b])

    q = _group_rms64(seg(OFF_AQ, OFF_AK), gmat, qg_ref[...])
    k = _group_rms64(seg(OFF_AK, OFF_AV), gmat, kg_ref[...])
    if rope:
        c = cos_ref[...]
        s = sin_ref[...]
        q = _rope(q, c, s)
        k = _rope(k, c, s)
    q_ref[...] = (q * (A_HEAD_DIM ** -0.5)).astype(q_ref.dtype)
    k_ref[...] = k.astype(k_ref.dtype)
    v_ref[...] = seg(OFF_AV, OFF_BQK).astype(v_ref.dtype)

    bqk = seg(OFF_BQK, OFF_BV)
    lane = lax.broadcasted_iota(jnp.int32, bqk.shape, 1)
    gqk_ref[...] = jnp.where(lane < B_HEADS * B_KDIM, bqk * (B_KDIM ** -0.5), bqk).astype(gqk_ref.dtype)
    gv_ref[...] = seg(OFF_BV, OFF_BG).astype(gv_ref.dtype)
    bg = seg(OFF_BG, OFF_CU)
    gg_ref[...] = (bg * jax.nn.sigmoid(bg)).astype(gg_ref.dtype)
    cu_ref[...] = seg(OFF_CU, OFF_GZ).astype(cu_ref.dtype)
    gate_ref[...] = jax.nn.sigmoid(seg(OFF_GZ, OFF_BR)).astype(gate_ref.dtype)

    r = seg(OFF_BR, IN_COLS)
    pre = _dot_f32(r, wa_ref[...]) + ba_ref[...]
    la_ref[...] = (jnp.minimum(pre, 0.0) - jnp.log1p(jnp.exp(-jnp.abs(pre)))) * (1.0 / GLA_TAU)


def _inproj(x2, mod3, tiles_per_mod, n1g, w_all, gmat, qg, kg, wa, ba, rope_tabs, kv_dtype):
    n_tok = x2.shape[0]
    tm = TOKEN_TILE
    rope = rope_tabs is not None
    tok = lambda w: pl.BlockSpec((tm, w), lambda i: (i, 0))
    in_specs = [tok(D_MODEL),
                pl.BlockSpec((None, 1, mod3.shape[-1]), lambda i: (i // tiles_per_mod, 0, 0)),
                _resident(n1g.shape), _resident(w_all.shape), _resident(gmat.shape),
                _resident(qg.shape), _resident(kg.shape), _resident(wa.shape), _resident(ba.shape)]
    args = [x2, mod3, n1g, w_all, gmat, qg, kg, wa, ba]
    if rope:
        tiles_per_seq = rope_tabs[0].shape[0] // tm
        for t in rope_tabs:
            in_specs.append(pl.BlockSpec((tm, MIX_W), lambda i: (i % tiles_per_seq, 0)))
            args.append(t)
    widths = [(MIX_W, BF16), (MIX_W, kv_dtype), (MIX_W, kv_dtype), (MIX_W, BF16), (MIX_W, BF16),
              (MIX_W, BF16), (MIX_W, F32), (MIX_W, BF16), (3 * D_MODEL, BF16)]
    return pl.pallas_call(
        functools.partial(_inproj_kernel, rope),
        grid=(n_tok // tm,),
        in_specs=in_specs,
        out_specs=[tok(w) for w, _ in widths],
        out_shape=[jax.ShapeDtypeStruct((n_tok, w), dt) for w, dt in widths],
        compiler_params=_cparams("parallel"),
        name="inproj",
    )(*args)


def _attn_kernel(lam_init, q_ref, k_ref, v_ref, lamp_ref, subg_ref, o_ref):
    lv = lamp_ref[...]
    lam = (jnp.exp(jnp.sum(lv[0:1] * lv[1:2], axis=-1, keepdims=True))
           - jnp.exp(jnp.sum(lv[2:3] * lv[3:4], axis=-1, keepdims=True)) + lam_init)
    hd = 2 * A_HEAD_DIM
    for h in range(A_HEADS):
        sl = slice(h * hd, (h + 1) * hd)
        qh = q_ref[:, sl]
        kh = k_ref[:, sl].astype(BF16)
        vh = v_ref[:, sl].astype(BF16)
        first = lax.broadcasted_iota(jnp.int32, qh.shape, 1) < A_HEAD_DIM
        zero = jnp.zeros_like(qh)
        outs = []
        for qm in (jnp.where(first, qh, zero), jnp.where(first, zero, qh)):
            s = _dot_nt(qm, kh)
            e = jnp.exp(s - jnp.max(s, axis=-1, keepdims=True))
            outs.append(_dot(e.astype(BF16), vh) / jnp.sum(e, axis=-1, keepdims=True))
        o = outs[0] - lam * outs[1]
        o_ref[:, sl] = (_rms(o) * subg_ref[:, sl] * (1.0 - lam_init)).astype(o_ref.dtype)


def _diff_attention(q, k, v, lamp, subg, lam_init):
    bsz, lq, w = q.shape
    lk = k.shape[1]
    tq = TOKEN_TILE
    return pl.pallas_call(
        functools.partial(_attn_kernel, lam_init),
        grid=(bsz, lq // tq),
        in_specs=[pl.BlockSpec((None, tq, w), lambda b, i: (b, i, 0)),
                  pl.BlockSpec((None, lk, w), lambda b, i: (b, 0, 0)),
                  pl.BlockSpec((None, lk, w), lambda b, i: (b, 0, 0)),
                  pl.BlockSpec(lamp.shape, lambda b, i: (0, 0)),
                  pl.BlockSpec(subg.shape, lambda b, i: (0, 0))],
        out_specs=pl.BlockSpec((None, tq, w), lambda b, i: (b, i, 0)),
        out_shape=jax.ShapeDtypeStruct((bsz, lq, w), BF16),
        compiler_params=_cparams("parallel", "parallel"),
        name="diff_attention",
    )(q, k, v, lamp, subg)


def _gla_masks(chunk):
    nlev = int(math.log2(chunk))
    assert 1 << nlev == chunk
    t = np.arange(chunk)[:, None]
    r = np.arange(chunk)[None, :]
    cum, pair = [], []
    for j in range(nlev + 1):
        start = (t >> j) << j
        end = start + (1 << j) - 1
        cum.append((r >= start) & (r <= t))
        cum.append((r > t) & (r <= end))
        if j < nlev:
            pair.append(((t >> (j + 1)) == (r >> (j + 1))) & (((t >> j) & 1) == 1) & (((r >> j) & 1) == 0))
    pair.append(t == r)
    cum_f = np.concatenate(cum, 0).astype(np.float32)
    pair_f = np.stack(pair, 0).astype(np.float32)
    cum_b = np.concatenate([m[::-1, ::-1] for m in cum], 0).astype(np.float32)
    pair_b = pair_f[:, ::-1, ::-1]
    return (jnp.asarray(np.stack([cum_f, cum_b]), BF16), jnp.asarray(np.stack([pair_f, pair_b]), F32), nlev)


def _gla_chunk(qk, v, la, cum, pair_ref, d, st_ref, bd, nlev, last_row):
    c = qk.shape[0]
    kw = B_HEADS * B_KDIM
    q = qk[:, :kw].astype(F32)
    k = qk[:, kw:].astype(F32)
    hi, lo = _split_bf16(la)
    e2 = _dot(cum, jnp.concatenate([hi, lo], axis=1))
    e = jnp.exp(e2[:, :kw] + e2[:, kw:])
    lane = lax.broadcasted_iota(jnp.int32, (c, kw), 1)
    head_masks = [(lane >= h * B_KDIM) & (lane < (h + 1) * B_KDIM) for h in range(B_HEADS)]
    zero = jnp.zeros((c, kw), BF16)

    def scores(qf, kf, pm):
        qb = qf.astype(BF16)
        kb = kf.astype(BF16)
        return [pm * _dot_nt(jnp.where(m, qb, zero), kb) for m in head_masks]

    att = scores(q, k, pair_ref[d, nlev])
    for j in range(nlev):
        eq = e[(2 * j) * c:(2 * j + 1) * c]
        ek = e[(2 * j + 1) * c:(2 * j + 2) * c]
        lev = scores(q * eq, k * ek, pair_ref[d, j])
        att = [a + b for a, b in zip(att, lev)]
    eq = e[(2 * nlev) * c:(2 * nlev + 1) * c]
    ek = e[(2 * nlev + 1) * c:(2 * nlev + 2) * c]
    st = st_ref[...]
    o = _dot_nt((q * eq).astype(BF16), st.astype(BF16))
    outs = []
    for h in range(B_HEADS):
        sl = slice(h * B_VDIM, (h + 1) * B_VDIM)
        outs.append(o[:, sl] + _dot(att[h].astype(BF16), v[:, sl]))
    dec = eq[last_row:last_row + 1, :]
    st_ref[...] = st * dec + bd * _dot_tn(v, (k * ek).astype(BF16))
    return outs


def _gla_kernel(has_s0, nlev, *refs):
    if has_s0:
        s0_ref, refs = refs[0], refs[1:]
    (qkf_ref, vf_ref, laf_ref, qkb_ref, vb_ref, lab_ref, cum_ref, pair_ref, bd_ref,
     of_ref, ob_ref, sfin_ref, st_f, st_b) = refs
    i = pl.program_id(1)
    c = qkf_ref.shape[0]

    @pl.when(i == 0)
    def _():
        if has_s0:
            st_f[...] = s0_ref[0]
            st_b[...] = s0_ref[1]
        else:
            st_f[...] = jnp.zeros_like(st_f)
            st_b[...] = jnp.zeros_like(st_b)

    bd = bd_ref[...]
    outs = _gla_chunk(qkf_ref[...], vf_ref[...], laf_ref[...], cum_ref[0], pair_ref, 0, st_f, bd, nlev, c - 1)
    for h, o in enumerate(outs):
        of_ref[:, h * B_VDIM:(h + 1) * B_VDIM] = o
    outs = _gla_chunk(qkb_ref[...], vb_ref[...], lab_ref[...], cum_ref[1], pair_ref, 1, st_b, bd, nlev, 0)
    for h, o in enumerate(outs):
        ob_ref[:, h * B_VDIM:(h + 1) * B_VDIM] = o

    @pl.when(i == pl.num_programs(1) - 1)
    def _():
        sfin_ref[0] = st_f[...]
        sfin_ref[1] = st_b[...]


def _gla(gqk, gv, la, s0t):
    bsz, n_tok, _ = gqk.shape
    c = GLA_CHUNK
    n = n_tok // c
    cum, pair, nlev = _gla_masks(c)
    kw = B_HEADS * B_KDIM
    vw = B_HEADS * B_VDIM
    rows = np.arange(vw)[:, None] // B_VDIM
    cols = np.arange(kw)[None, :] // B_KDIM
    bd = jnp.asarray((rows == cols).astype(np.float32))
    fwd = lambda w, off=0: pl.BlockSpec((None, c, w), lambda b, i: (b, i, off))
    bwd = lambda w, off=0: pl.BlockSpec((None, c, w), lambda b, i: (b, n - 1 - i, off))
    in_specs = [fwd(2 * kw), fwd(vw), fwd(kw, 0), bwd(2 * kw), bwd(vw), bwd(kw, 1),
                _resident(cum.shape), _resident(pair.shape), _resident(bd.shape)]
    args = [gqk, gv, la, gqk, gv, la, cum, pair, bd]
    if s0t is not None:
        in_specs.insert(0, pl.BlockSpec((None, 2, vw, kw), lambda b, i: (b, 0, 0, 0)))
        args.insert(0, s0t)
    return pl.pallas_call(
        functools.partial(_gla_kernel, s0t is not None, nlev),
        grid=(bsz, n),
        in_specs=in_specs,
        out_specs=[fwd(vw), bwd(vw), pl.BlockSpec((None, 2, vw, kw), lambda b, i: (b, 0, 0, 0))],
        out_shape=[jax.ShapeDtypeStruct((bsz, n_tok, vw), F32), jax.ShapeDtypeStruct((bsz, n_tok, vw), F32),
                   jax.ShapeDtypeStruct((bsz, 2, vw, kw), F32)],
        scratch_shapes=[pltpu.VMEM((vw, kw), F32), pltpu.VMEM((vw, kw), F32)],
        compiler_params=_cparams("parallel", "arbitrary"),
        name="gla",
    )(*args)


def _s5_prep_kernel(lr_ref, li_ref, dt_ref, bre_ref, bim_ref, cre_ref, cim_ref, coef_ref, ker_ref, apow_ref):
    t_len = S5_CHUNK
    rows = t_len * S5_GROUP
    step = lax.broadcasted_iota(jnp.int32, (rows, S5_P), 0) >> 4
    for d in range(2):
        lr = lr_ref[d:d + 1, :]
        li = li_ref[d:d + 1, :]
        dt = jnp.exp(dt_ref[d:d + 1, :])
        a = lr * dt
        th = li * dt
        mag = jnp.exp(a)
        ar = mag * jnp.cos(th)
        ai = mag * jnp.sin(th)
        den = lr * lr + li * li
        fr = ((ar - 1.0) * lr + ai * li) / den
        fi = (ai * lr - (ar - 1.0) * li) / den
        b_re = bre_ref[d]
        b_im = bim_ref[d]
        bbr = jnp.concatenate([fr * b_re - fi * b_im] * t_len, axis=0)
        bbi = jnp.concatenate([fr * b_im + fi * b_re] * t_len, axis=0)
        c_re = cre_ref[d]
        c_im = cim_ref[d]
        c_re_t = jnp.concatenate([c_re] * t_len, axis=0)
        c_im_t = jnp.concatenate([c_im] * t_len, axis=0)

        def power(ex):
            exf = ex.astype(F32)
            m = jnp.exp(exf * a)
            return m * jnp.cos(exf * th), m * jnp.sin(exf * th)

        pr, pi = power(step)
        xr = bbr * pr - bbi * pi
        xi = bbr * pi + bbi * pr
        ker_ref[d] = _dot_f32(xr, c_re, nt=True) - _dot_f32(xi, c_im, nt=True)
        pr, pi = power((t_len - 1 - step) if d == 0 else step)
        coef_ref[d, 0] = bbr * pr - bbi * pi
        coef_ref[d, 1] = bbr * pi + bbi * pr
        pr, pi = power((step + 1) if d == 0 else (t_len - step))
        coef_ref[d, 2] = c_re_t * pr - c_im_t * pi
        coef_ref[d, 3] = -(c_re_t * pi + c_im_t * pr)
        tl = jnp.full((SUBLANES, S5_P), float(t_len), F32)
        mt = jnp.exp(tl * a)
        apow_ref[d, 0] = mt * jnp.cos(tl * th)
        apow_ref[d, 1] = mt * jnp.sin(tl * th)


def _s5_prep(lr, li, ldt, bre_t, bim_t, cre, cim):
    g = lr.shape[0]
    rows = S5_CHUNK * S5_GROUP
    grp = lambda *s: pl.BlockSpec((None,) + s, lambda i: (i,) + (0,) * len(s))
    return pl.pallas_call(
        _s5_prep_kernel,
        grid=(g,),
        in_specs=[grp(2, S5_P)] * 3 + [grp(2, S5_GROUP, S5_P)] * 4,
        out_specs=[grp(2, 4, rows, S5_P), grp(2, rows, S5_GROUP), grp(2, 2, SUBLANES, S5_P)],
        out_shape=[jax.ShapeDtypeStruct((g, 2, 4, rows, S5_P), F32),
                   jax.ShapeDtypeStruct((g, 2, rows, S5_GROUP), F32),
                   jax.ShapeDtypeStruct((g, 2, 2, SUBLANES, S5_P), F32)],
        compiler_params=_cparams("parallel"),
        name="s5_prep",
    )(lr, li, ldt, bre_t, bim_t, cre, cim)


def _s5_kernel(n_chunks, bp, u_ref, toep_ref, wst_ref, cst_ref, apow_ref, h0_ref, y_ref, hfin_ref, s_scr, h_scr):
    u = u_ref[...]
    s_scr[...] = _dot(u, wst_ref[...])
    ap = apow_ref[...]
    a_re = ap[0:1, :]
    a_im = ap[1:2, :]
    half = 2 * S5_P
    lane = lax.broadcasted_iota(jnp.int32, (bp, 2 * half), 1)
    is_fwd = (lane & (half - 1)) < S5_P

    def body(kk, hs):
        rf = pl.multiple_of(kk * bp, bp)
        rb = pl.multiple_of((n_chunks - 1 - kk) * bp, bp)
        h_scr[pl.ds(rf, bp), 0:2 * half] = hs
        h_scr[pl.ds(rb, bp), 2 * half:4 * half] = hs
        sc = jnp.where(is_fwd, s_scr[pl.ds(rf, bp), :], s_scr[pl.ds(rb, bp), :])
        re = hs[:, :half]
        im = hs[:, half:]
        return jnp.concatenate([a_re * re - a_im * im + sc[:, :half],
                                a_re * im + a_im * re + sc[:, half:]], axis=1)

    hs = lax.fori_loop(0, n_chunks, body, h0_ref[...])
    hfin_ref[...] = hs
    y_ref[...] = _dot(u, toep_ref[...]) + _dot_nt(h_scr[...].astype(BF16), cst_ref[...])


def _s5(u, toep, wst, cst, apow, h0, n_chunks, bp):
    g, m, w = u.shape
    grp = lambda *s: pl.BlockSpec((None,) + s, lambda i: (i,) + (0,) * len(s))
    return pl.pallas_call(
        functools.partial(_s5_kernel, n_chunks, bp),
        grid=(g,),
        in_specs=[grp(m, w), grp(w, w), grp(w, 4 * S5_P), grp(w, 8 * S5_P), grp(SUBLANES, 2 * S5_P),
                  grp(bp, 4 * S5_P)],
        out_specs=[grp(m, w), grp(bp, 4 * S5_P)],
        out_shape=[jax.ShapeDtypeStruct((g, m, w), F32), jax.ShapeDtypeStruct((g, bp, 4 * S5_P), F32)],
        scratch_shapes=[pltpu.VMEM((m, 4 * S5_P), F32), pltpu.VMEM((m, 8 * S5_P), F32)],
        compiler_params=_cparams("parallel"),
        name="s5_scan",
    )(u, toep, wst, cst, apow, h0)


def _merge_kernel(x_ref, mod_ref, oa_ref, of_ref, ob_ref, gg_ref, y_ref, cu_ref, gate_ref,
                  ong_ref, s5d_ref, wglu_ref, bglu_ref, wbr_ref, wout_ref, o_ref):
    g1 = mod_ref[...][:, 2 * D_MODEL:3 * D_MODEL]
    o_gla = of_ref[...] + ob_ref[...]
    parts = []
    for h in range(B_HEADS):
        sl = slice(h * B_VDIM, (h + 1) * B_VDIM)
        parts.append((_rms(o_gla[:, sl]) * ong_ref[...] * gg_ref[:, sl].astype(F32)).astype(BF16))
    ob = jnp.concatenate(parts, axis=1)

    cu = cu_ref[...].astype(F32)
    z = y_ref[...] + s5d_ref[...] * cu
    yc = 0.5 * z * (1.0 + jnp.tanh(math.sqrt(2.0 / math.pi) * (z + 0.044715 * (z * z * z))))
    glu = _dot(yc.astype(BF16), wglu_ref[...]) + bglu_ref[...]
    oc = (glu[:, :MIX_W] * jax.nn.sigmoid(glu[:, MIX_W:])).astype(BF16)

    merged = None
    for r, br in enumerate((oa_ref[...], ob, oc)):
        term = gate_ref[:, r * D_MODEL:(r + 1) * D_MODEL].astype(F32) * _dot(br, wbr_ref[r])
        merged = term if merged is None else merged + term
    o_ref[...] = x_ref[...] + g1 * _dot(merged.astype(BF16), wout_ref[...])


def _merge(x2, mod3, tiles_per_mod, oa, of, ob, gg, y, cu, gates, ong, s5d, wglu, bglu, wbr, wout):
    n_tok = x2.shape[0]
    tm = TOKEN_TILE
    tok = lambda w: pl.BlockSpec((tm, w), lambda i: (i, 0))
    return pl.pallas_call(
        _merge_kernel,
        grid=(n_tok // tm,),
        in_specs=[tok(D_MODEL), pl.BlockSpec((None, 1, mod3.shape[-1]), lambda i: (i // tiles_per_mod, 0, 0)),
                  tok(MIX_W), tok(MIX_W), tok(MIX_W), tok(MIX_W), tok(MIX_W), tok(MIX_W), tok(3 * D_MODEL),
                  _resident(ong.shape), _resident(s5d.shape), _resident(wglu.shape), _resident(bglu.shape),
                  _resident(wbr.shape), _resident(wout.shape)],
        out_specs=tok(D_MODEL),
        out_shape=jax.ShapeDtypeStruct((n_tok, D_MODEL), F32),
        compiler_params=_cparams("parallel"),
        name="merge",
    )(x2, mod3, oa, of, ob, gg, y, cu, gates, ong, s5d, wglu, bglu, wbr, wout)


def _ffn_kernel(x_ref, mod_ref, n2_ref, wg_ref, wu_ref, wd_ref, o_ref):
    mod = mod_ref[...]
    sh2 = mod[:, 3 * D_MODEL:4 * D_MODEL]
    sc2 = mod[:, 4 * D_MODEL:5 * D_MODEL]
    g2 = mod[:, 5 * D_MODEL:6 * D_MODEL]
    x = x_ref[...]
    hb = (_rms(x) * n2_ref[...] * (1.0 + sc2) + sh2).astype(BF16)
    gate = _dot(hb, wg_ref[...])
    act = (gate * jax.nn.sigmoid(gate) * _dot(hb, wu_ref[...])).astype(BF16)
    o_ref[...] = x + g2 * _dot(act, wd_ref[...])


def _ffn(x2, mod3, tiles_per_mod, n2g, wg, wu, wd):
    n_tok = x2.shape[0]
    tm = TOKEN_TILE
    tok = pl.BlockSpec((tm, D_MODEL), lambda i: (i, 0))
    return pl.pallas_call(
        _ffn_kernel,
        grid=(n_tok // tm,),
        in_specs=[tok, pl.BlockSpec((None, 1, mod3.shape[-1]), lambda i: (i // tiles_per_mod, 0, 0)),
                  _resident(n2g.shape), _resident(wg.shape), _resident(wu.shape), _resident(wd.shape)],
        out_specs=tok,
        out_shape=jax.ShapeDtypeStruct((n_tok, D_MODEL), F32),
        compiler_params=_cparams("parallel"),
        name="swiglu",
    )(x2, mod3, n2g, wg, wu, wd)


def _rope_tables(n_tok):
    t = np.arange(n_tok)
    row = (t // GRID_W).astype(np.float32)
    col = (t % GRID_W).astype(np.float32)
    half = A_HEAD_DIM // 2
    inv = jnp.asarray(ROPE_THETA, F32) ** (-jnp.arange(0, half, 2, dtype=F32) / half)
    ang_r = jnp.asarray(row)[:, None] * inv
    ang_c = jnp.asarray(col)[:, None] * inv
    cos = jnp.concatenate([jnp.cos(ang_r)] * 2 + [jnp.cos(ang_c)] * 2, axis=-1)
    sin = jnp.concatenate([-jnp.sin(ang_r), jnp.sin(ang_r), -jnp.sin(ang_c), jnp.sin(ang_c)], axis=-1)
    reps = MIX_W // A_HEAD_DIM
    return jnp.tile(cos, (1, reps)), jnp.tile(sin, (1, reps))


def _toeplitz(ker):
    g = ker.shape[0]
    t_len = S5_CHUNK
    k4 = ker.reshape(g, 2, t_len, S5_GROUP, S5_GROUP)
    s = np.arange(t_len)[:, None]
    t = np.arange(t_len)[None, :]
    lag_f = np.clip(t - s, 0, t_len - 1)
    lag_b = np.clip(s - t, 0, t_len - 1)
    kf = jnp.where((t >= s)[None, :, :, None, None], k4[:, 0][:, lag_f], 0.0)
    kb = jnp.where((s >= t)[None, :, :, None, None], k4[:, 1][:, lag_b], 0.0)
    full = (kf + kb).transpose(0, 1, 3, 2, 4)
    return full.reshape(g, t_len * S5_GROUP, t_len * S5_GROUP)


def _prepare_layer(l, w):
    w_in = w['w_in'][l]
    br0, br1 = OFF_CU, OFF_CU + 2 * GLA_RANK
    w_all = jnp.concatenate([w_in[:, :br0], w_in[:, br1:], w_in[:, br0:br1],
                             jnp.zeros((D_MODEL, BR_PAD - 2 * GLA_RANK), F32)], axis=1).astype(BF16)
    wa = jnp.zeros((BR_PAD, 2 * B_HEADS * B_KDIM), F32)
    wa = wa.at[0:GLA_RANK, 0:256].set(w['gla_wa2'][l, 0]).at[GLA_RANK:2 * GLA_RANK, 256:512].set(w['gla_wa2'][l, 1])
    ba = w['gla_ba'][l].reshape(1, 512)
    gidx = np.arange(MIX_W) // A_HEAD_DIM
    gmat = jnp.asarray((gidx[:, None] == gidx[None, :]).astype(np.float32) / A_HEAD_DIM, BF16)
    gm = lambda a: jnp.swapaxes(a, 0, 1)
    ldt = jnp.broadcast_to(w['s5_log_dt'][l][:, :, None], (2, S5_G, S5_P))
    coef, ker, apow = _s5_prep(gm(w['s5_lam_re'][l]), gm(w['s5_lam_im'][l]), gm(ldt),
                               gm(jnp.swapaxes(w['s5_b_re'][l], -1, -2)), gm(jnp.swapaxes(w['s5_b_im'][l], -1, -2)),
                               gm(w['s5_c_re'][l]), gm(w['s5_c_im'][l]))
    zc = jnp.zeros_like(coef[:, 0, 0])
    wst = jnp.concatenate([coef[:, 0, 0], coef[:, 1, 0], coef[:, 0, 1], coef[:, 1, 1]], axis=-1).astype(BF16)
    cst = jnp.concatenate([coef[:, 0, 2], zc, coef[:, 0, 3], zc, zc, coef[:, 1, 2], zc, coef[:, 1, 3]],
                          axis=-1).astype(BF16)
    ap = jnp.concatenate([apow[:, 0], apow[:, 1]], axis=-1)
    ap = jnp.concatenate([ap[:, 0, 0:1], ap[:, 1, 0:1], jnp.zeros((S5_G, SUBLANES - 2, 2 * S5_P), F32)], axis=1)
    return dict(
        w_all=w_all, wa=wa, ba=ba, gmat=gmat,
        n1g=w['norm1_g'][l].reshape(1, -1), n2g=w['norm2_g'][l].reshape(1, -1),
        qg=jnp.tile(w['diff_qn_g'][l], MIX_W // A_HEAD_DIM).reshape(1, -1),
        kg=jnp.tile(w['diff_kn_g'][l], MIX_W // A_HEAD_DIM).reshape(1, -1),
        lamp=w['diff_lam'][l], subg=jnp.tile(w['diff_subln_g'][l], A_HEADS).reshape(1, -1),
        ong=w['gla_on_g'][l].reshape(1, -1), s5d=w['s5_d'][l].reshape(1, -1),
        toep=_toeplitz(ker).astype(BF16), wst=wst, cst=cst, apow=ap,
        wglu=w['s5_w_glu'][l].astype(BF16), bglu=w['s5_b_glu'][l].reshape(1, -1),
        wbr=w['w_branch'][l].astype(BF16), wout=w['w_out'][l].astype(BF16),
        wg=w['w_ffn_gate'][l].astype(BF16), wu=w['w_ffn_up'][l].astype(BF16), wd=w['w_ffn_down'][l].astype(BF16),
    )


def _layer(x, mod3, p, lam_init, ctx, rope_tabs):
    bsz, n_tok, _ = x.shape
    x2 = x.reshape(bsz * n_tok, D_MODEL)
    tiles_per_mod = (bsz * n_tok // TOKEN_TILE) // mod3.shape[0]
    latent = ctx is not None
    q, k, v, gqk, gv, gg, la, cu, gates = _inproj(
        x2, mod3, tiles_per_mod, p['n1g'], p['w_all'], p['gmat'], p['qg'], p['kg'], p['wa'], p['ba'],
        rope_tabs if latent else None, BF16 if latent else F32)
    sh = lambda a: a.reshape(bsz, n_tok, a.shape[-1])

    if latent:
        keys = jnp.concatenate([ctx['k'].reshape(bsz, -1, MIX_W).astype(BF16), sh(k)], axis=1)
        vals = jnp.concatenate([ctx['v'].reshape(bsz, -1, MIX_W).astype(BF16), sh(v)], axis=1)
    else:
        keys, vals = sh(k), sh(v)
    oa = _diff_attention(sh(q), keys, vals, p['lamp'], p['subg'], lam_init)

    s0t = None
    if latent:
        s0 = ctx['gla'].transpose(0, 1, 2, 4, 3)
        eye = jnp.eye(B_HEADS, dtype=F32)
        s0t = (s0[:, :, :, :, None, :] * eye[None, None, :, None, :, None]).reshape(
            bsz, 2, B_HEADS * B_VDIM, B_HEADS * B_KDIM)
    of, ob, sfin = _gla(sh(gqk), sh(gv), sh(la), s0t)

    t_len = S5_CHUNK
    n_chunks = n_tok // t_len
    bp = -(-bsz // SUBLANES) * SUBLANES
    u = sh(cu).reshape(bsz, n_chunks, t_len, S5_G, S5_GROUP).transpose(3, 1, 0, 2, 4)
    u = jnp.pad(u, ((0, 0), (0, 0), (0, bp - bsz), (0, 0), (0, 0))).reshape(S5_G, n_chunks * bp, t_len * S5_GROUP)
    if latent:
        h0 = ctx['s5'].transpose(3, 0, 2, 1, 4).reshape(S5_G, bsz, 4 * S5_P)
        h0 = jnp.pad(h0, ((0, 0), (0, bp - bsz), (0, 0)))
    else:
        h0 = jnp.zeros((S5_G, bp, 4 * S5_P), F32)
    y, hfin = _s5(u, p['toep'], p['wst'], p['cst'], p['apow'], h0, n_chunks, bp)
    y = y.reshape(S5_G, n_chunks, bp, t_len, S5_GROUP)[:, :, :bsz].transpose(2, 1, 3, 0, 4)
    y = y.reshape(bsz * n_tok, MIX_W)

    x1 = _merge(x2, mod3, tiles_per_mod, oa.reshape(bsz * n_tok, MIX_W), of.reshape(bsz * n_tok, MIX_W),
                ob.reshape(bsz * n_tok, MIX_W), gg, y, cu, gates, p['ong'], p['s5d'], p['wglu'], p['bglu'],
                p['wbr'], p['wout'])
    x_out = _ffn(x1, mod3, tiles_per_mod, p['n2g'], p['wg'], p['wu'], p['wd']).reshape(bsz, n_tok, D_MODEL)
    if latent:
        return x_out, None
    new_k = sh(k).reshape(bsz, n_tok, A_HEADS, 2, A_HEAD_DIM)
    new_v = sh(v).reshape(bsz, n_tok, A_HEADS, 2 * A_HEAD_DIM)
    st = sfin.reshape(bsz, 2, B_HEADS, B_VDIM, B_HEADS, B_KDIM)
    new_gla = jnp.stack([st[:, :, h, :, h, :] for h in range(B_HEADS)], axis=2).transpose(0, 1, 2, 4, 3)
    hf = hfin[:, :bsz].reshape(S5_G, bsz, 2, 2, S5_P)
    new_s5 = hf.transpose(1, 3, 2, 0, 4)
    return x_out, (new_k, new_v, new_gla, new_s5)


def kernel(x_prompt, x_sample, cache_diff_k, cache_diff_v, state_gla, state_s5, c, c_ctx, w_mod, b_mod, norm1_g, norm2_g, w_in, diff_qn_g, diff_kn_g, diff_lam, diff_subln_g, gla_wa2, gla_ba, gla_on_g, s5_lam_re, s5_lam_im, s5_log_dt, s5_b_re, s5_b_im, s5_c_re, s5_c_im, s5_d, s5_w_glu, s5_b_glu, w_branch, w_out, w_ffn_gate, w_ffn_up, w_ffn_down):
    weights = dict(norm1_g=norm1_g, norm2_g=norm2_g, w_in=w_in, diff_qn_g=diff_qn_g, diff_kn_g=diff_kn_g,
                   diff_lam=diff_lam, diff_subln_g=diff_subln_g, gla_wa2=gla_wa2, gla_ba=gla_ba,
                   gla_on_g=gla_on_g, s5_lam_re=s5_lam_re, s5_lam_im=s5_lam_im, s5_log_dt=s5_log_dt,
                   s5_b_re=s5_b_re, s5_b_im=s5_b_im, s5_c_re=s5_c_re, s5_c_im=s5_c_im, s5_d=s5_d,
                   s5_w_glu=s5_w_glu, s5_b_glu=s5_b_glu, w_branch=w_branch, w_out=w_out,
                   w_ffn_gate=w_ffn_gate, w_ffn_up=w_ffn_up, w_ffn_down=w_ffn_down)
    depth = w_mod.shape[0]
    dec_b = c.shape[0]
    cond8 = jnp.concatenate([c_ctx[None, :], c, jnp.zeros((SUBLANES - 1 - dec_b, D_MODEL), F32)], axis=0)
    mod = _modulation(cond8, w_mod, b_mod)
    rope_tabs = _rope_tables(x_sample.shape[1])
    y_prompt, y_sample = x_prompt, x_sample
    k_list, v_list, gla_list, s5_list = [], [], [], []
    for l in range(depth):
        p = _prepare_layer(l, weights)
        lam_init = 0.8 - 0.6 * math.exp(-0.3 * l)
        y_prompt, (k_l, v_l, g_l, s_l) = _layer(y_prompt, mod[l, 0:1][:, None, :], p, lam_init, None, None)
        k_list.append(k_l)
        v_list.append(v_l)
        gla_list.append(g_l)
        s5_list.append(s_l)
        ctx = dict(k=cache_diff_k[:, l], v=cache_diff_v[:, l], gla=state_gla[:, l], s5=state_s5[:, l])
        y_sample, _ = _layer(y_sample, mod[l, 1:1 + dec_b][:, None, :], p, lam_init, ctx, rope_tabs)
    return (y_prompt, y_sample, jnp.stack(k_list, axis=1), jnp.stack(v_list, axis=1),
            jnp.stack(gla_list, axis=1), jnp.stack(s5_list, axis=1))
```

```python
import functools
import math

import numpy as np
import jax
import jax.numpy as jnp
from jax import lax
from jax.experimental import pallas as pl
from jax.experimental.pallas import tpu as pltpu

F32 = jnp.float32
BF16 = jnp.bfloat16

D_MODEL = 1024
MIX_W = 512
A_HEADS = 4
A_HEAD_DIM = 64
GRID_W = 64
ROPE_THETA = 10000.0
B_HEADS = 4
B_KDIM = 64
B_VDIM = 128
GLA_RANK = 16
GLA_TAU = 16.0
S5_G = 32
S5_GROUP = 16
S5_P = 64
FFN_DIM = 2816
EPS = 1e-6

VMEM_LIMIT_BYTES = 56 * 1024 * 1024
TOKEN_TILE = 256
GLA_CHUNK = 128
S5_CHUNK = 16
SUBLANES = 8
LANES = 128
S5_TILE_GROUPS = LANES // 16
S5_ROW_BLOCK = 64

OFF_AQ, OFF_AK, OFF_AV, OFF_BQK, OFF_BV, OFF_BG, OFF_CU, OFF_GZ, OFF_BR = (
    0, 512, 1024, 1536, 2048, 2560, 3072, 3584, 6656)
BR_PAD = 128
IN_COLS = OFF_BR + BR_PAD


def _cparams(*sem):
    return pltpu.CompilerParams(dimension_semantics=sem, vmem_limit_bytes=VMEM_LIMIT_BYTES)


def _resident(shape):
    nd = len(shape)
    return pl.BlockSpec(shape, lambda *_: (0,) * nd, pipeline_mode=pl.Buffered(1))


def _split_bf16(x):
    hi = x.astype(BF16)
    lo = (x - hi.astype(F32)).astype(BF16)
    return hi, lo


def _dot(a, b):
    return jnp.dot(a, b, preferred_element_type=F32)


def _dot_nt(a, b):
    return lax.dot_general(a, b, (((1,), (1,)), ((), ())), preferred_element_type=F32)


def _dot_tn(a, b):
    return lax.dot_general(a, b, (((0,), (0,)), ((), ())), preferred_element_type=F32)


def _dot_f32(a, b, nt=False):
    d = _dot_nt if nt else _dot
    ah, al = _split_bf16(a)
    bh, bl = _split_bf16(b)
    return d(ah, bh) + d(ah, bl) + d(al, bh)


def _rms(x):
    return x * lax.rsqrt(jnp.mean(x * x, axis=-1, keepdims=True) + EPS)


def _mod_kernel(cond_ref, w_ref, b_ref, o_ref):
    cnd = cond_ref[...]
    s = (cnd * jax.nn.sigmoid(cnd))
    o_ref[...] = _dot_f32(s, w_ref[...]) + b_ref[...]


def _modulation(cond8, w_mod, b_mod):
    depth, d, n = w_mod.shape
    tn = 1536
    return pl.pallas_call(
        _mod_kernel,
        grid=(depth, n // tn),
        in_specs=[pl.BlockSpec((SUBLANES, d), lambda l, j: (0, 0)),
                  pl.BlockSpec((None, d, tn), lambda l, j: (l, 0, j)),
                  pl.BlockSpec((None, 1, tn), lambda l, j: (l, 0, j))],
        out_specs=pl.BlockSpec((None, SUBLANES, tn), lambda l, j: (l, 0, j)),
        out_shape=jax.ShapeDtypeStruct((depth, SUBLANES, n), F32),
        compiler_params=_cparams("parallel", "parallel"),
        name="modulation",
    )(cond8, w_mod, b_mod.reshape(depth, 1, n))


def _group_rms64(z, gmat, gain):
    hi, lo = _split_bf16(z * z)
    ms = _dot(hi, gmat) + _dot(lo, gmat)
    return z * lax.rsqrt(ms + EPS) * gain


def _rope(z, c, s):
    n = z.shape[-1]
    lane = lax.broadcasted_iota(jnp.int32, z.shape, 1)
    first = (lane & 31) < 16
    partner = jnp.where(first, pltpu.roll(z, n - 16, 1), pltpu.roll(z, 16, 1))
    return z * c + partner * s


def _inproj_kernel(rope, x_ref, mod_ref, n1_ref, w_ref, gmat_ref, qg_ref, kg_ref, wa_ref, ba_ref, *rest):
    if rope:
        cos_ref, sin_ref = rest[:2]
        rest = rest[2:]
    q_ref, k_ref, v_ref, gqk_ref, gv_ref, gg_ref, la_ref, cu_ref, gate_ref = rest
    mod = mod_ref[...]
    sh1 = mod[:, 0:D_MODEL]
    sc1 = mod[:, D_MODEL:2 * D_MODEL]
    h = _rms(x_ref[...]) * n1_ref[...] * (1.0 + sc1) + sh1
    hb = h.astype(BF16)
    gmat = gmat_ref[...]

    def seg(a, b):
        return _dot(hb, w_ref[:, a:b])

    q = _group_rms64(seg(OFF_AQ, OFF_AK), gmat, qg_ref[...])
    k = _group_rms64(seg(OFF_AK, OFF_AV), gmat, kg_ref[...])
    if rope:
        c = cos_ref[...]
        s = sin_ref[...]
        q = _rope(q, c, s)
        k = _rope(k, c, s)
    q_ref[...] = (q * (A_HEAD_DIM ** -0.5)).astype(q_ref.dtype)
    k_ref[...] = k.astype(k_ref.dtype)
    v_ref[...] = seg(OFF_AV, OFF_BQK).astype(v_ref.dtype)

    bqk = seg(OFF_BQK, OFF_BV)
    lane = lax.broadcasted_iota(jnp.int32, bqk.shape, 1)
    gqk_ref[...] = jnp.where(lane < B_HEADS * B_KDIM, bqk * (B_KDIM ** -0.5), bqk).astype(gqk_ref.dtype)
    gv_ref[...] = seg(OFF_BV, OFF_BG).astype(gv_ref.dtype)
    bg = seg(OFF_BG, OFF_CU)
    gg_ref[...] = (bg * jax.nn.sigmoid(bg)).astype(gg_ref.dtype)
    cu_ref[...] = seg(OFF_CU, OFF_GZ).astype(cu_ref.dtype)
    gate_ref[...] = jax.nn.sigmoid(seg(OFF_GZ, OFF_BR)).astype(gate_ref.dtype)

    r = seg(OFF_BR, IN_COLS)
    pre = _dot_f32(r, wa_ref[...]) + ba_ref[...]
    la_ref[...] = (jnp.minimum(pre, 0.0) - jnp.log1p(jnp.exp(-jnp.abs(pre)))) * (1.0 / GLA_TAU)


def _inproj(x2, mod3, tiles_per_mod, n1g, w_all, gmat, qg, kg, wa, ba, rope_tabs, kv_dtype):
    n_tok = x2.shape[0]
    tm = TOKEN_TILE
    rope = rope_tabs is not None
    tok = lambda w: pl.BlockSpec((tm, w), lambda i: (i, 0))
    in_specs = [tok(D_MODEL),
                pl.BlockSpec((None, 1, mod3.shape[-1]), lambda i: (i // tiles_per_mod, 0, 0)),
                _resident(n1g.shape), _resident(w_all.shape), _resident(gmat.shape),
                _resident(qg.shape), _resident(kg.shape), _resident(wa.shape), _resident(ba.shape)]
    args = [x2, mod3, n1g, w_all, gmat, qg, kg, wa, ba]
    if rope:
        tiles_per_seq = rope_tabs[0].shape[0] // tm
        for t in rope_tabs:
            in_specs.append(pl.BlockSpec((tm, MIX_W), lambda i: (i % tiles_per_seq, 0)))
            args.append(t)
    widths = [(MIX_W, BF16), (MIX_W, kv_dtype), (MIX_W, kv_dtype), (MIX_W, BF16), (MIX_W, BF16),
              (MIX_W, BF16), (MIX_W, F32), (MIX_W, F32), (3 * D_MODEL, BF16)]
    return pl.pallas_call(
        functools.partial(_inproj_kernel, rope),
        grid=(n_tok // tm,),
        in_specs=in_specs,
        out_specs=[tok(w) for w, _ in widths],
        out_shape=[jax.ShapeDtypeStruct((n_tok, w), dt) for w, dt in widths],
        compiler_params=_cparams("parallel"),
        name="inproj",
    )(*args)


def _attn_kernel(lam_init, n_seg, q_ref, *refs):
    k_refs = refs[:n_seg]
    v_refs = refs[n_seg:2 * n_seg]
    lamp_ref, subg_ref, o_ref = refs[2 * n_seg:]
    lv = lamp_ref[...]
    lam = (jnp.exp(jnp.sum(lv[0:1] * lv[1:2], axis=-1, keepdims=True))
           - jnp.exp(jnp.sum(lv[2:3] * lv[3:4], axis=-1, keepdims=True)) + lam_init)
    hd = 2 * A_HEAD_DIM
    tq = q_ref.shape[0]
    for h in range(A_HEADS):
        sl = slice(h * hd, (h + 1) * hd)
        qh = q_ref[:, sl]
        first = lax.broadcasted_iota(jnp.int32, qh.shape, 1) < A_HEAD_DIM
        zero = jnp.zeros_like(qh)
        q2 = jnp.concatenate([jnp.where(first, qh, zero), jnp.where(first, zero, qh)], axis=0)
        scores = [_dot_nt(q2, k_ref[:, sl].astype(BF16)) for k_ref in k_refs]
        m = functools.reduce(jnp.maximum, [jnp.max(s, axis=-1, keepdims=True) for s in scores])
        acc = None
        den = None
        for s, v_ref in zip(scores, v_refs):
            e = jnp.exp(s - m)
            pv = _dot(e.astype(BF16), v_ref[:, sl].astype(BF16))
            ssum = jnp.sum(e, axis=-1, keepdims=True)
            acc = pv if acc is None else acc + pv
            den = ssum if den is None else den + ssum
        o2 = acc / den
        o = o2[:tq] - lam * o2[tq:]
        o_ref[:, sl] = (_rms(o) * subg_ref[:, sl] * (1.0 - lam_init)).astype(o_ref.dtype)


def _diff_attention(q, ks, vs, lamp, subg, lam_init):
    bsz, lq, w = q.shape
    tq = TOKEN_TILE
    kv_spec = lambda a: pl.BlockSpec((None, a.shape[1], w), lambda b, i: (b, 0, 0))
    return pl.pallas_call(
        functools.partial(_attn_kernel, lam_init, len(ks)),
        grid=(bsz, lq // tq),
        in_specs=[pl.BlockSpec((None, tq, w), lambda b, i: (b, i, 0))]
                 + [kv_spec(a) for a in ks] + [kv_spec(a) for a in vs]
                 + [pl.BlockSpec(lamp.shape, lambda b, i: (0, 0)), pl.BlockSpec(subg.shape, lambda b, i: (0, 0))],
        out_specs=pl.BlockSpec((None, tq, w), lambda b, i: (b, i, 0)),
        out_shape=jax.ShapeDtypeStruct((bsz, lq, w), BF16),
        compiler_params=_cparams("parallel", "parallel"),
        name="diff_attention",
    )(q, *ks, *vs, lamp, subg)


def _gla_masks(chunk):
    nlev = int(math.log2(chunk))
    assert 1 << nlev == chunk
    t = np.arange(chunk)[:, None]
    r = np.arange(chunk)[None, :]
    cum, pair = [], []
    for j in range(nlev + 1):
        start = (t >> j) << j
        end = start + (1 << j) - 1
        cum.append((r >= start) & (r <= t))
        cum.append((r > t) & (r <= end))
        if j < nlev:
            pair.append(((t >> (j + 1)) == (r >> (j + 1))) & (((t >> j) & 1) == 1) & (((r >> j) & 1) == 0))
    pair.append(t == r)
    cum_f = np.concatenate(cum, 0).astype(np.float32)
    pair_f = np.stack(pair, 0).astype(np.float32)
    cum_b = np.concatenate([m[::-1, ::-1] for m in cum], 0).astype(np.float32)
    pair_b = pair_f[:, ::-1, ::-1]
    return (jnp.asarray(np.stack([cum_f, cum_b]), BF16), jnp.asarray(np.stack([pair_f, pair_b]), F32), nlev)


def _gla_chunk(qk, v, la, cum, pair_ref, d, st_ref, bd, nlev, last_row):
    c = qk.shape[0]
    kw = B_HEADS * B_KDIM
    q = qk[:, :kw].astype(F32)
    k = qk[:, kw:].astype(F32)
    hi, lo = _split_bf16(la)
    e2 = _dot(cum, jnp.concatenate([hi, lo], axis=1))
    e = jnp.exp(e2[:, :kw] + e2[:, kw:])
    lane = lax.broadcasted_iota(jnp.int32, (c, kw), 1)
    head_masks = [(lane >= h * B_KDIM) & (lane < (h + 1) * B_KDIM) for h in range(B_HEADS)]
    zero = jnp.zeros((c, kw), BF16)

    def scores(qf, kf, pm):
        qb = qf.astype(BF16)
        kb = kf.astype(BF16)
        stacked = _dot_nt(jnp.concatenate([jnp.where(m, qb, zero) for m in head_masks], axis=0), kb)
        return [pm * stacked[h * c:(h + 1) * c] for h in range(B_HEADS)]

    att = scores(q, k, pair_ref[d, nlev])
    for j in range(nlev):
        eq = e[(2 * j) * c:(2 * j + 1) * c]
        ek = e[(2 * j + 1) * c:(2 * j + 2) * c]
        lev = scores(q * eq, k * ek, pair_ref[d, j])
        att = [a + b for a, b in zip(att, lev)]
    eq = e[(2 * nlev) * c:(2 * nlev + 1) * c]
    ek = e[(2 * nlev + 1) * c:(2 * nlev + 2) * c]
    st = st_ref[...]
    o = _dot_nt((q * eq).astype(BF16), st.astype(BF16))
    outs = []
    for h in range(B_HEADS):
        sl = slice(h * B_VDIM, (h + 1) * B_VDIM)
        outs.append(o[:, sl] + _dot(att[h].astype(BF16), v[:, sl]))
    dec = eq[last_row:last_row + 1, :]
    st_ref[...] = st * dec + bd * _dot_tn(v, (k * ek).astype(BF16))
    return outs


def _gla_kernel(has_s0, nlev, *refs):
    if has_s0:
        s0_ref, refs = refs[0], refs[1:]
    (qkf_ref, vf_ref, laf_ref, qkb_ref, vb_ref, lab_ref, cum_ref, pair_ref, bd_ref,
     of_ref, ob_ref, sfin_ref, st_f, st_b) = refs
    i = pl.program_id(1)
    c = qkf_ref.shape[0]

    @pl.when(i == 0)
    def _():
        for d, st in enumerate((st_f, st_b)):
            if has_s0:
                zero = jnp.zeros((B_KDIM, B_VDIM), F32)
                full = jnp.concatenate(
                    [jnp.concatenate([s0_ref[d, h] if hh == h else zero for hh in range(B_HEADS)], axis=1)
                     for h in range(B_HEADS)], axis=0)
                st[...] = full.T
            else:
                st[...] = jnp.zeros_like(st)

    bd = bd_ref[...]
    outs = _gla_chunk(qkf_ref[...], vf_ref[...], laf_ref[...], cum_ref[0], pair_ref, 0, st_f, bd, nlev, c - 1)
    for h, o in enumerate(outs):
        of_ref[:, h * B_VDIM:(h + 1) * B_VDIM] = o
    outs = _gla_chunk(qkb_ref[...], vb_ref[...], lab_ref[...], cum_ref[1], pair_ref, 1, st_b, bd, nlev, 0)
    for h, o in enumerate(outs):
        ob_ref[:, h * B_VDIM:(h + 1) * B_VDIM] = o

    @pl.when(i == pl.num_programs(1) - 1)
    def _():
        for d, st in enumerate((st_f, st_b)):
            full = st[...].T
            for h in range(B_HEADS):
                sfin_ref[d, h] = full[h * B_KDIM:(h + 1) * B_KDIM, h * B_VDIM:(h + 1) * B_VDIM]


def _gla(gqk, gv, la, s0):
    bsz, n_tok, _ = gqk.shape
    c = GLA_CHUNK
    n = n_tok // c
    cum, pair, nlev = _gla_masks(c)
    kw = B_HEADS * B_KDIM
    vw = B_HEADS * B_VDIM
    rows = np.arange(vw)[:, None] // B_VDIM
    cols = np.arange(kw)[None, :] // B_KDIM
    bd = jnp.asarray((rows == cols).astype(np.float32))
    fwd = lambda w, off=0: pl.BlockSpec((None, c, w), lambda b, i: (b, i, off))
    bwd = lambda w, off=0: pl.BlockSpec((None, c, w), lambda b, i: (b, n - 1 - i, off))
    in_specs = [fwd(2 * kw), fwd(vw), fwd(kw, 0), bwd(2 * kw), bwd(vw), bwd(kw, 1),
                _resident(cum.shape), _resident(pair.shape), _resident(bd.shape)]
    args = [gqk, gv, la, gqk, gv, la, cum, pair, bd]
    state_spec = pl.BlockSpec((None, 2, B_HEADS, B_KDIM, B_VDIM), lambda b, i: (b, 0, 0, 0, 0))
    if s0 is not None:
        in_specs.insert(0, state_spec)
        args.insert(0, s0)
    return pl.pallas_call(
        functools.partial(_gla_kernel, s0 is not None, nlev),
        grid=(bsz, n),
        in_specs=in_specs,
        out_specs=[fwd(vw), bwd(vw), state_spec],
        out_shape=[jax.ShapeDtypeStruct((bsz, n_tok, vw), F32), jax.ShapeDtypeStruct((bsz, n_tok, vw), F32),
                   jax.ShapeDtypeStruct((bsz, 2, B_HEADS, B_KDIM, B_VDIM), F32)],
        scratch_shapes=[pltpu.VMEM((vw, kw), F32), pltpu.VMEM((vw, kw), F32)],
        compiler_params=_cparams("parallel", "arbitrary"),
        name="gla",
    )(*args)


def _s5_prep_kernel(lr_ref, li_ref, dt_ref, bre_ref, bim_ref, cre_ref, cim_ref, coef_ref, ker_ref, apow_ref):
    t_len = S5_CHUNK
    rows = t_len * S5_GROUP
    step = lax.broadcasted_iota(jnp.int32, (rows, S5_P), 0) >> 4
    for d in range(2):
        lr = lr_ref[d:d + 1, :]
        li = li_ref[d:d + 1, :]
        dt = jnp.exp(dt_ref[d:d + 1, :])
        a = lr * dt
        th = li * dt
        mag = jnp.exp(a)
        ar = mag * jnp.cos(th)
        ai = mag * jnp.sin(th)
        den = lr * lr + li * li
        fr = ((ar - 1.0) * lr + ai * li) / den
        fi = (ai * lr - (ar - 1.0) * li) / den
        b_re = bre_ref[d]
        b_im = bim_ref[d]
        bbr = jnp.concatenate([fr * b_re - fi * b_im] * t_len, axis=0)
        bbi = jnp.concatenate([fr * b_im + fi * b_re] * t_len, axis=0)
        c_re = cre_ref[d]
        c_im = cim_ref[d]
        c_re_t = jnp.concatenate([c_re] * t_len, axis=0)
        c_im_t = jnp.concatenate([c_im] * t_len, axis=0)

        def power(ex):
            exf = ex.astype(F32)
            m = jnp.exp(exf * a)
            return m * jnp.cos(exf * th), m * jnp.sin(exf * th)

        pr, pi = power(step)
        xr = bbr * pr - bbi * pi
        xi = bbr * pi + bbi * pr
        ker_ref[d] = _dot_f32(xr, c_re, nt=True) - _dot_f32(xi, c_im, nt=True)
        pr, pi = power((t_len - 1 - step) if d == 0 else step)
        coef_ref[d, 0] = bbr * pr - bbi * pi
        coef_ref[d, 1] = bbr * pi + bbi * pr
        pr, pi = power((step + 1) if d == 0 else (t_len - step))
        coef_ref[d, 2] = c_re_t * pr - c_im_t * pi
        coef_ref[d, 3] = -(c_re_t * pi + c_im_t * pr)
        tl = jnp.full((SUBLANES, S5_P), float(t_len), F32)
        mt = jnp.exp(tl * a)
        apow_ref[d, 0] = mt * jnp.cos(tl * th)
        apow_ref[d, 1] = mt * jnp.sin(tl * th)


def _s5_prep(lr, li, ldt, bre_t, bim_t, cre, cim):
    g = lr.shape[0]
    rows = S5_CHUNK * S5_GROUP
    grp = lambda *s: pl.BlockSpec((None,) + s, lambda i: (i,) + (0,) * len(s))
    return pl.pallas_call(
        _s5_prep_kernel,
        grid=(g,),
        in_specs=[grp(2, S5_P)] * 3 + [grp(2, S5_GROUP, S5_P)] * 4,
        out_specs=[grp(2, 4, rows, S5_P), grp(2, rows, S5_GROUP), grp(2, 2, SUBLANES, S5_P)],
        out_shape=[jax.ShapeDtypeStruct((g, 2, 4, rows, S5_P), F32),
                   jax.ShapeDtypeStruct((g, 2, rows, S5_GROUP), F32),
                   jax.ShapeDtypeStruct((g, 2, 2, SUBLANES, S5_P), F32)],
        compiler_params=_cparams("parallel"),
        name="s5_prep",
    )(lr, li, ldt, bre_t, bim_t, cre, cim)


def _s5_kernel(n_chunks, bsz, cu_ref, toep_ref, wst_ref, cst_ref, apow_ref, h0_ref, y_ref, hfin_ref,
               u_scr, s_scr, h_scr, yg_scr):
    t_len = S5_CHUNK
    ng = S5_TILE_GROUPS
    rows = bsz * n_chunks
    rb = math.gcd(S5_ROW_BLOCK, rows)
    blk = lax.broadcasted_iota(jnp.int32, (rb, LANES), 1) // S5_GROUP

    def gather_blocks(pieces):
        acc = None
        for piece, src_blk, dst_blk in pieces:
            shift = (S5_GROUP * (dst_blk - src_blk)) % LANES
            v = piece if shift == 0 else pltpu.roll(piece, shift, 1)
            acc = v if acc is None else jnp.where(blk == dst_blk, v, acc)
        return acc

    def relayout_in(r, carry):
        r0 = pl.multiple_of(r * rb, rb)
        for h in range(t_len // ng):
            xs = [cu_ref[pl.ds(r0 * t_len + ng * h + b, rb, stride=t_len), :] for b in range(ng)]
            for g in range(ng):
                tile = gather_blocks([(xs[b], g, b) for b in range(ng)])
                u_scr[g, pl.ds(r0, rb), h * LANES:(h + 1) * LANES] = tile.astype(BF16)
        return carry

    lax.fori_loop(0, rows // rb, relayout_in, 0)

    half = 2 * S5_P
    for g in range(ng):
        s = _dot(u_scr[g], wst_ref[g])
        s_scr[g, 0] = s[:, :half]
        s_scr[g, 1] = s[:, half:]

    is_fwd = lax.broadcasted_iota(jnp.int32, (bsz, half), 1) < S5_P

    def scan_step(kk, hs):
        fwd_rows = pl.ds(kk, bsz, stride=n_chunks)
        bwd_rows = pl.ds(n_chunks - 1 - kk, bsz, stride=n_chunks)
        new = []
        for g in range(ng):
            re, im = hs[g]
            h_scr[g, 0, fwd_rows, :] = re
            h_scr[g, 1, fwd_rows, :] = im
            h_scr[g, 2, bwd_rows, :] = re
            h_scr[g, 3, bwd_rows, :] = im
            s_re = jnp.where(is_fwd, s_scr[g, 0, fwd_rows, :], s_scr[g, 0, bwd_rows, :])
            s_im = jnp.where(is_fwd, s_scr[g, 1, fwd_rows, :], s_scr[g, 1, bwd_rows, :])
            a_re = apow_ref[g, 0:1, :]
            a_im = apow_ref[g, 1:2, :]
            new.append((a_re * re - a_im * im + s_re, a_re * im + a_im * re + s_im))
        return tuple(new)

    hs = lax.fori_loop(0, n_chunks, scan_step,
                       tuple((h0_ref[g, :, :half], h0_ref[g, :, half:]) for g in range(ng)))
    for g in range(ng):
        hfin_ref[g, :, :half] = hs[g][0]
        hfin_ref[g, :, half:] = hs[g][1]
        h_all = jnp.concatenate([h_scr[g, j] for j in range(4)], axis=1).astype(BF16)
        yg_scr[g] = _dot(u_scr[g], toep_ref[g]) + _dot_nt(h_all, cst_ref[g])

    def relayout_out(r, carry):
        r0 = pl.multiple_of(r * rb, rb)
        for h in range(t_len // ng):
            ys = [yg_scr[g, pl.ds(r0, rb), h * LANES:(h + 1) * LANES] for g in range(ng)]
            for b in range(ng):
                tile = gather_blocks([(ys[g], b, g) for g in range(ng)])
                y_ref[pl.ds(r0 * t_len + ng * h + b, rb, stride=t_len), :] = tile
        return carry

    lax.fori_loop(0, rows // rb, relayout_out, 0)


def _s5(cu, toep, wst, cst, apow, h0, bsz):
    n_rows, width = cu.shape
    n_chunks = n_rows // bsz // S5_CHUNK
    rows = bsz * n_chunks
    ng = S5_TILE_GROUPS
    w = S5_CHUNK * S5_GROUP
    tile = lambda *s: pl.BlockSpec((ng,) + s, lambda j: (j,) + (0,) * len(s))
    lanes = pl.BlockSpec((n_rows, LANES), lambda j: (0, j))
    return pl.pallas_call(
        functools.partial(_s5_kernel, n_chunks, bsz),
        grid=(width // LANES,),
        in_specs=[lanes, tile(w, w), tile(w, 4 * S5_P), tile(w, 8 * S5_P), tile(SUBLANES, 2 * S5_P),
                  tile(bsz, 4 * S5_P)],
        out_specs=[lanes, tile(bsz, 4 * S5_P)],
        out_shape=[jax.ShapeDtypeStruct((n_rows, width), F32), jax.ShapeDtypeStruct((S5_G, bsz, 4 * S5_P), F32)],
        scratch_shapes=[pltpu.VMEM((ng, rows, w), BF16), pltpu.VMEM((ng, 2, rows, LANES), F32),
                        pltpu.VMEM((ng, 4, rows, LANES), F32), pltpu.VMEM((ng, rows, w), F32)],
        compiler_params=_cparams("parallel"),
        name="s5_scan",
    )(cu, toep, wst, cst, apow, h0)


def _merge_kernel(x_ref, mod_ref, oa_ref, of_ref, ob_ref, gg_ref, y_ref, cu_ref, gate_ref,
                  ong_ref, s5d_ref, wglu_ref, bglu_ref, wbr_ref, wout_ref, o_ref):
    g1 = mod_ref[...][:, 2 * D_MODEL:3 * D_MODEL]
    o_gla = of_ref[...] + ob_ref[...]
    parts = []
    for h in range(B_HEADS):
        sl = slice(h * B_VDIM, (h + 1) * B_VDIM)
        parts.append((_rms(o_gla[:, sl]) * ong_ref[...] * gg_ref[:, sl].astype(F32)).astype(BF16))
    ob = jnp.concatenate(parts, axis=1)

    cu = cu_ref[...].astype(F32)
    z = y_ref[...] + s5d_ref[...] * cu
    yc = 0.5 * z * (1.0 + jnp.tanh(math.sqrt(2.0 / math.pi) * (z + 0.044715 * (z * z * z))))
    glu = _dot(yc.astype(BF16), wglu_ref[...]) + bglu_ref[...]
    oc = (glu[:, :MIX_W] * jax.nn.sigmoid(glu[:, MIX_W:])).astype(BF16)

    merged = None
    for r, br in enumerate((oa_ref[...], ob, oc)):
        term = gate_ref[:, r * D_MODEL:(r + 1) * D_MODEL].astype(F32) * _dot(br, wbr_ref[r])
        merged = term if merged is None else merged + term
    o_ref[...] = x_ref[...] + g1 * _dot(merged.astype(BF16), wout_ref[...])


def _merge(x2, mod3, tiles_per_mod, oa, of, ob, gg, y, cu, gates, ong, s5d, wglu, bglu, wbr, wout):
    n_tok = x2.shape[0]
    tm = TOKEN_TILE
    tok = lambda w: pl.BlockSpec((tm, w), lambda i: (i, 0))
    return pl.pallas_call(
        _merge_kernel,
        grid=(n_tok // tm,),
        in_specs=[tok(D_MODEL), pl.BlockSpec((None, 1, mod3.shape[-1]), lambda i: (i // tiles_per_mod, 0, 0)),
                  tok(MIX_W), tok(MIX_W), tok(MIX_W), tok(MIX_W), tok(MIX_W), tok(MIX_W), tok(3 * D_MODEL),
                  _resident(ong.shape), _resident(s5d.shape), _resident(wglu.shape), _resident(bglu.shape),
                  _resident(wbr.shape), _resident(wout.shape)],
        out_specs=tok(D_MODEL),
        out_shape=jax.ShapeDtypeStruct((n_tok, D_MODEL), F32),
        compiler_params=_cparams("parallel"),
        name="merge",
    )(x2, mod3, oa, of, ob, gg, y, cu, gates, ong, s5d, wglu, bglu, wbr, wout)


def _ffn_kernel(x_ref, mod_ref, n2_ref, wg_ref, wu_ref, wd_ref, o_ref):
    mod = mod_ref[...]
    sh2 = mod[:, 3 * D_MODEL:4 * D_MODEL]
    sc2 = mod[:, 4 * D_MODEL:5 * D_MODEL]
    g2 = mod[:, 5 * D_MODEL:6 * D_MODEL]
    x = x_ref[...]
    hb = (_rms(x) * n2_ref[...] * (1.0 + sc2) + sh2).astype(BF16)
    gate = _dot(hb, wg_ref[...])
    act = (gate * jax.nn.sigmoid(gate) * _dot(hb, wu_ref[...])).astype(BF16)
    o_ref[...] = x + g2 * _dot(act, wd_ref[...])


def _ffn(x2, mod3, tiles_per_mod, n2g, wg, wu, wd):
    n_tok = x2.shape[0]
    tm = TOKEN_TILE
    tok = pl.BlockSpec((tm, D_MODEL), lambda i: (i, 0))
    return pl.pallas_call(
        _ffn_kernel,
        grid=(n_tok // tm,),
        in_specs=[tok, pl.BlockSpec((None, 1, mod3.shape[-1]), lambda i: (i // tiles_per_mod, 0, 0)),
                  _resident(n2g.shape), _resident(wg.shape), _resident(wu.shape), _resident(wd.shape)],
        out_specs=tok,
        out_shape=jax.ShapeDtypeStruct((n_tok, D_MODEL), F32),
        compiler_params=_cparams("parallel"),
        name="swiglu",
    )(x2, mod3, n2g, wg, wu, wd)


def _rope_tables(n_tok):
    t = np.arange(n_tok)
    row = (t // GRID_W).astype(np.float32)
    col = (t % GRID_W).astype(np.float32)
    half = A_HEAD_DIM // 2
    inv = jnp.asarray(ROPE_THETA, F32) ** (-jnp.arange(0, half, 2, dtype=F32) / half)
    ang_r = jnp.asarray(row)[:, None] * inv
    ang_c = jnp.asarray(col)[:, None] * inv
    cos = jnp.concatenate([jnp.cos(ang_r)] * 2 + [jnp.cos(ang_c)] * 2, axis=-1)
    sin = jnp.concatenate([-jnp.sin(ang_r), jnp.sin(ang_r), -jnp.sin(ang_c), jnp.sin(ang_c)], axis=-1)
    reps = MIX_W // A_HEAD_DIM
    return jnp.tile(cos, (1, reps)), jnp.tile(sin, (1, reps))


def _toeplitz(ker):
    g = ker.shape[0]
    t_len = S5_CHUNK
    k4 = ker.reshape(g, 2, t_len, S5_GROUP, S5_GROUP)
    s = np.arange(t_len)[:, None]
    t = np.arange(t_len)[None, :]
    lag_f = np.clip(t - s, 0, t_len - 1)
    lag_b = np.clip(s - t, 0, t_len - 1)
    kf = jnp.where((t >= s)[None, :, :, None, None], k4[:, 0][:, lag_f], 0.0)
    kb = jnp.where((s >= t)[None, :, :, None, None], k4[:, 1][:, lag_b], 0.0)
    full = (kf + kb).transpose(0, 1, 3, 2, 4)
    return full.reshape(g, t_len * S5_GROUP, t_len * S5_GROUP)


def _prepare_layer(l, w):
    w_in = w['w_in'][l]
    br0, br1 = OFF_CU, OFF_CU + 2 * GLA_RANK
    w_all = jnp.concatenate([w_in[:, :br0], w_in[:, br1:], w_in[:, br0:br1],
                             jnp.zeros((D_MODEL, BR_PAD - 2 * GLA_RANK), F32)], axis=1).astype(BF16)
    wa = jnp.zeros((BR_PAD, 2 * B_HEADS * B_KDIM), F32)
    wa = wa.at[0:GLA_RANK, 0:256].set(w['gla_wa2'][l, 0]).at[GLA_RANK:2 * GLA_RANK, 256:512].set(w['gla_wa2'][l, 1])
    ba = w['gla_ba'][l].reshape(1, 512)
    gidx = np.arange(MIX_W) // A_HEAD_DIM
    gmat = jnp.asarray((gidx[:, None] == gidx[None, :]).astype(np.float32) / A_HEAD_DIM, BF16)
    gm = lambda a: jnp.swapaxes(a, 0, 1)
    ldt = jnp.broadcast_to(w['s5_log_dt'][l][:, :, None], (2, S5_G, S5_P))
    coef, ker, apow = _s5_prep(gm(w['s5_lam_re'][l]), gm(w['s5_lam_im'][l]), gm(ldt),
                               gm(jnp.swapaxes(w['s5_b_re'][l], -1, -2)), gm(jnp.swapaxes(w['s5_b_im'][l], -1, -2)),
                               gm(w['s5_c_re'][l]), gm(w['s5_c_im'][l]))
    zc = jnp.zeros_like(coef[:, 0, 0])
    wst = jnp.concatenate([coef[:, 0, 0], coef[:, 1, 0], coef[:, 0, 1], coef[:, 1, 1]], axis=-1).astype(BF16)
    cst = jnp.concatenate([coef[:, 0, 2], zc, coef[:, 0, 3], zc, zc, coef[:, 1, 2], zc, coef[:, 1, 3]],
                          axis=-1).astype(BF16)
    ap = jnp.concatenate([apow[:, 0], apow[:, 1]], axis=-1)
    ap = jnp.concatenate([ap[:, 0, 0:1], ap[:, 1, 0:1], jnp.zeros((S5_G, SUBLANES - 2, 2 * S5_P), F32)], axis=1)
    return dict(
        w_all=w_all, wa=wa, ba=ba, gmat=gmat,
        n1g=w['norm1_g'][l].reshape(1, -1), n2g=w['norm2_g'][l].reshape(1, -1),
        qg=jnp.tile(w['diff_qn_g'][l], MIX_W // A_HEAD_DIM).reshape(1, -1),
        kg=jnp.tile(w['diff_kn_g'][l], MIX_W // A_HEAD_DIM).reshape(1, -1),
        lamp=w['diff_lam'][l], subg=jnp.tile(w['diff_subln_g'][l], A_HEADS).reshape(1, -1),
        ong=w['gla_on_g'][l].reshape(1, -1), s5d=w['s5_d'][l].reshape(1, -1),
        toep=_toeplitz(ker).astype(BF16), wst=wst, cst=cst, apow=ap,
        wglu=w['s5_w_glu'][l].astype(BF16), bglu=w['s5_b_glu'][l].reshape(1, -1),
        wbr=w['w_branch'][l].astype(BF16), wout=w['w_out'][l].astype(BF16),
        wg=w['w_ffn_gate'][l].astype(BF16), wu=w['w_ffn_up'][l].astype(BF16), wd=w['w_ffn_down'][l].astype(BF16),
    )


def _layer(x, mod3, p, lam_init, ctx, rope_tabs):
    bsz, n_tok, _ = x.shape
    x2 = x.reshape(bsz * n_tok, D_MODEL)
    tiles_per_mod = (bsz * n_tok // TOKEN_TILE) // mod3.shape[0]
    latent = ctx is not None
    q, k, v, gqk, gv, gg, la, cu, gates = _inproj(
        x2, mod3, tiles_per_mod, p['n1g'], p['w_all'], p['gmat'], p['qg'], p['kg'], p['wa'], p['ba'],
        rope_tabs if latent else None, BF16 if latent else F32)
    sh = lambda a: a.reshape(bsz, n_tok, a.shape[-1])

    if latent:
        keys = [ctx['k'].reshape(bsz, -1, MIX_W), sh(k)]
        vals = [ctx['v'].reshape(bsz, -1, MIX_W), sh(v)]
    else:
        keys, vals = [sh(k)], [sh(v)]
    oa = _diff_attention(sh(q), keys, vals, p['lamp'], p['subg'], lam_init)

    of, ob, new_gla = _gla(sh(gqk), sh(gv), sh(la), ctx['gla'] if latent else None)

    if latent:
        h0 = ctx['s5'].transpose(3, 0, 2, 1, 4).reshape(S5_G, bsz, 4 * S5_P)
    else:
        h0 = jnp.zeros((S5_G, bsz, 4 * S5_P), F32)
    y, hfin = _s5(cu, p['toep'], p['wst'], p['cst'], p['apow'], h0, bsz)

    x1 = _merge(x2, mod3, tiles_per_mod, oa.reshape(bsz * n_tok, MIX_W), of.reshape(bsz * n_tok, MIX_W),
                ob.reshape(bsz * n_tok, MIX_W), gg, y, cu, gates, p['ong'], p['s5d'], p['wglu'], p['bglu'],
                p['wbr'], p['wout'])
    x_out = _ffn(x1, mod3, tiles_per_mod, p['n2g'], p['wg'], p['wu'], p['wd']).reshape(bsz, n_tok, D_MODEL)
    if latent:
        return x_out, None
    new_k = sh(k).reshape(bsz, n_tok, A_HEADS, 2, A_HEAD_DIM)
    new_v = sh(v).reshape(bsz, n_tok, A_HEADS, 2 * A_HEAD_DIM)
    hf = hfin.reshape(S5_G, bsz, 2, 2, S5_P)
    new_s5 = hf.transpose(1, 3, 2, 0, 4)
    return x_out, (new_k, new_v, new_gla, new_s5)


def kernel(x_prompt, x_sample, cache_diff_k, cache_diff_v, state_gla, state_s5, c, c_ctx, w_mod, b_mod, norm1_g, norm2_g, w_in, diff_qn_g, diff_kn_g, diff_lam, diff_subln_g, gla_wa2, gla_ba, gla_on_g, s5_lam_re, s5_lam_im, s5_log_dt, s5_b_re, s5_b_im, s5_c_re, s5_c_im, s5_d, s5_w_glu, s5_b_glu, w_branch, w_out, w_ffn_gate, w_ffn_up, w_ffn_down):
    weights = dict(norm1_g=norm1_g, norm2_g=norm2_g, w_in=w_in, diff_qn_g=diff_qn_g, diff_kn_g=diff_kn_g,
                   diff_lam=diff_lam, diff_subln_g=diff_subln_g, gla_wa2=gla_wa2, gla_ba=gla_ba,
                   gla_on_g=gla_on_g, s5_lam_re=s5_lam_re, s5_lam_im=s5_lam_im, s5_log_dt=s5_log_dt,
                   s5_b_re=s5_b_re, s5_b_im=s5_b_im, s5_c_re=s5_c_re, s5_c_im=s5_c_im, s5_d=s5_d,
                   s5_w_glu=s5_w_glu, s5_b_glu=s5_b_glu, w_branch=w_branch, w_out=w_out,
                   w_ffn_gate=w_ffn_gate, w_ffn_up=w_ffn_up, w_ffn_down=w_ffn_down)
    depth = w_mod.shape[0]
    dec_b = c.shape[0]
    cond8 = jnp.concatenate([c_ctx[None, :], c, jnp.zeros((SUBLANES - 1 - dec_b, D_MODEL), F32)], axis=0)
    mod = _modulation(cond8, w_mod, b_mod)
    rope_tabs = _rope_tables(x_sample.shape[1])
    y_prompt, y_sample = x_prompt, x_sample
    k_list, v_list, gla_list, s5_list = [], [], [], []
    for l in range(depth):
        p = _prepare_layer(l, weights)
        lam_init = 0.8 - 0.6 * math.exp(-0.3 * l)
        y_prompt, (k_l, v_l, g_l, s_l) = _layer(y_prompt, mod[l, 0:1][:, None, :], p, lam_init, None, None)
        k_list.append(k_l)
        v_list.append(v_l)
        gla_list.append(g_l)
        s5_list.append(s_l)
        ctx = dict(k=cache_diff_k[:, l], v=cache_diff_v[:, l], gla=state_gla[:, l], s5=state_s5[:, l])
        y_sample, _ = _layer(y_sample, mod[l, 1:1 + dec_b][:, None, :], p, lam_init, ctx, rope_tabs)
    return (y_prompt, y_sample, jnp.stack(k_list, axis=1), jnp.stack(v_list, axis=1),
            jnp.stack(gla_list, axis=1), jnp.stack(s5_list, axis=1))
```

```python
import functools
import math

import numpy as np
import jax
import jax.numpy as jnp
from jax import lax
from jax.experimental import pallas as pl
from jax.experimental.pallas import tpu as pltpu

F32 = jnp.float32
BF16 = jnp.bfloat16

D_MODEL = 1024
MIX_W = 512
A_HEADS = 4
A_HEAD_DIM = 64
GRID_W = 64
ROPE_THETA = 10000.0
B_HEADS = 4
B_KDIM = 64
B_VDIM = 128
GLA_RANK = 16
GLA_TAU = 16.0
S5_G = 32
S5_GROUP = 16
S5_P = 64
FFN_DIM = 2816
EPS = 1e-6

VMEM_LIMIT_BYTES = 56 * 1024 * 1024
INPROJ_TILE = 256
DENSE_TILE = 512
ATTN_Q_TILE = 256
GLA_CHUNK = 128
S5_CHUNK = 16
SUBLANES = 8
LANES = 128
S5_TILE_GROUPS = LANES // 16
S5_ROW_BLOCK = 64

OFF_AQ, OFF_AK, OFF_AV, OFF_BQK, OFF_BV, OFF_BG, OFF_CU, OFF_GZ, OFF_BR = (
    0, 512, 1024, 1536, 2048, 2560, 3072, 3584, 6656)
BR_PAD = 128
IN_COLS = OFF_BR + BR_PAD


def _cparams(*sem):
    return pltpu.CompilerParams(dimension_semantics=sem, vmem_limit_bytes=VMEM_LIMIT_BYTES)


def _resident(shape):
    nd = len(shape)
    return pl.BlockSpec(shape, lambda *_: (0,) * nd, pipeline_mode=pl.Buffered(1))


def _split_bf16(x):
    hi = x.astype(BF16)
    lo = (x - hi.astype(F32)).astype(BF16)
    return hi, lo


def _dot(a, b):
    return jnp.dot(a, b, preferred_element_type=F32)


def _dot_nt(a, b):
    return lax.dot_general(a, b, (((1,), (1,)), ((), ())), preferred_element_type=F32)


def _dot_tn(a, b):
    return lax.dot_general(a, b, (((0,), (0,)), ((), ())), preferred_element_type=F32)


def _dot_f32(a, b, nt=False):
    d = _dot_nt if nt else _dot
    ah, al = _split_bf16(a)
    bh, bl = _split_bf16(b)
    return d(ah, bh) + d(ah, bl) + d(al, bh)


def _rms(x):
    return x * lax.rsqrt(jnp.mean(x * x, axis=-1, keepdims=True) + EPS)


def _mod_kernel(cond_ref, w_ref, b_ref, o_ref):
    cnd = cond_ref[...]
    s = (cnd * jax.nn.sigmoid(cnd))
    o_ref[...] = _dot_f32(s, w_ref[...]) + b_ref[...]


def _modulation(cond8, w_mod, b_mod):
    depth, d, n = w_mod.shape
    tn = 1536
    return pl.pallas_call(
        _mod_kernel,
        grid=(depth, n // tn),
        in_specs=[pl.BlockSpec((SUBLANES, d), lambda l, j: (0, 0)),
                  pl.BlockSpec((None, d, tn), lambda l, j: (l, 0, j)),
                  pl.BlockSpec((None, 1, tn), lambda l, j: (l, 0, j))],
        out_specs=pl.BlockSpec((None, SUBLANES, tn), lambda l, j: (l, 0, j)),
        out_shape=jax.ShapeDtypeStruct((depth, SUBLANES, n), F32),
        compiler_params=_cparams("parallel", "parallel"),
        name="modulation",
    )(cond8, w_mod, b_mod.reshape(depth, 1, n))


def _group_rms64(z, gsum, gspread, gain):
    hi, lo = _split_bf16(_dot((z * z).astype(BF16), gsum))
    ms = _dot(hi, gspread) + _dot(lo, gspread)
    return z * lax.rsqrt(ms + EPS) * gain


def _rope(z, c, s):
    n = z.shape[-1]
    lane = lax.broadcasted_iota(jnp.int32, z.shape, 1)
    first = (lane & 31) < 16
    partner = jnp.where(first, pltpu.roll(z, n - 16, 1), pltpu.roll(z, 16, 1))
    return z * c + partner * s


def _inproj_kernel(rope, x_ref, mod_ref, n1_ref, w_ref, gsum_ref, gspread_ref, qg_ref, kg_ref, wa_ref, ba_ref,
                   *rest):
    if rope:
        cos_ref, sin_ref = rest[:2]
        rest = rest[2:]
    q_ref, k_ref, v_ref, gqk_ref, gv_ref, gg_ref, la_ref, cu_ref, gate_ref = rest
    mod = mod_ref[...]
    sh1 = mod[:, 0:D_MODEL]
    sc1 = mod[:, D_MODEL:2 * D_MODEL]
    h = _rms(x_ref[...]) * n1_ref[...] * (1.0 + sc1) + sh1
    hb = h.astype(BF16)
    gsum = gsum_ref[...]
    gspread = gspread_ref[...]

    def seg(a, b):
        return _dot(hb, w_ref[:, a:b])

    q = _group_rms64(seg(OFF_AQ, OFF_AK), gsum, gspread, qg_ref[...])
    k = _group_rms64(seg(OFF_AK, OFF_AV), gsum, gspread, kg_ref[...])
    if rope:
        c = cos_ref[...]
        s = sin_ref[...]
        q = _rope(q, c, s)
        k = _rope(k, c, s)
    q_ref[...] = (q * (A_HEAD_DIM ** -0.5)).astype(q_ref.dtype)
    k_ref[...] = k.astype(k_ref.dtype)
    v_ref[...] = seg(OFF_AV, OFF_BQK).astype(v_ref.dtype)

    bqk = seg(OFF_BQK, OFF_BV)
    lane = lax.broadcasted_iota(jnp.int32, bqk.shape, 1)
    gqk_ref[...] = jnp.where(lane < B_HEADS * B_KDIM, bqk * (B_KDIM ** -0.5), bqk).astype(gqk_ref.dtype)
    gv_ref[...] = seg(OFF_BV, OFF_BG).astype(gv_ref.dtype)
    bg = seg(OFF_BG, OFF_CU)
    gg_ref[...] = (bg * jax.nn.sigmoid(bg)).astype(gg_ref.dtype)
    cu_ref[...] = seg(OFF_CU, OFF_GZ).astype(cu_ref.dtype)
    gate_ref[...] = jax.nn.sigmoid(seg(OFF_GZ, OFF_BR)).astype(gate_ref.dtype)

    r = seg(OFF_BR, IN_COLS)
    r_hi, r_lo = _split_bf16(r)
    copy = lax.broadcasted_iota(jnp.int32, r.shape, 1) // (2 * GLA_RANK)
    wa_hi, wa_lo = _split_bf16(wa_ref[...])
    wcopy = lax.broadcasted_iota(jnp.int32, wa_hi.shape, 0) // (2 * GLA_RANK)
    pre = _dot(jnp.where(copy < 2, r_hi, r_lo), jnp.where(wcopy == 1, wa_lo, wa_hi)) + ba_ref[...]
    la_ref[...] = (jnp.minimum(pre, 0.0) - jnp.log1p(jnp.exp(-jnp.abs(pre)))) * (1.0 / GLA_TAU)


def _token_tiling(n_tok, mod3, want):
    tm = math.gcd(want, n_tok // mod3.shape[0])
    tiles_per_mod = (n_tok // tm) // mod3.shape[0]
    return tm, pl.BlockSpec((None, 1, mod3.shape[-1]), lambda i: (i // tiles_per_mod, 0, 0))


def _inproj(x2, mod3, n1g, w_all, gsum, gspread, qg, kg, wa, ba, rope_tabs, kv_dtype):
    n_tok = x2.shape[0]
    tm, mod_spec = _token_tiling(n_tok, mod3, INPROJ_TILE)
    rope = rope_tabs is not None
    tok = lambda w: pl.BlockSpec((tm, w), lambda i: (i, 0))
    in_specs = [tok(D_MODEL), mod_spec,
                _resident(n1g.shape), _resident(w_all.shape), _resident(gsum.shape), _resident(gspread.shape),
                _resident(qg.shape), _resident(kg.shape), _resident(wa.shape), _resident(ba.shape)]
    args = [x2, mod3, n1g, w_all, gsum, gspread, qg, kg, wa, ba]
    if rope:
        tiles_per_seq = rope_tabs[0].shape[0] // tm
        for t in rope_tabs:
            in_specs.append(pl.BlockSpec((tm, MIX_W), lambda i: (i % tiles_per_seq, 0)))
            args.append(t)
    widths = [(MIX_W, BF16), (MIX_W, kv_dtype), (MIX_W, kv_dtype), (MIX_W, BF16), (MIX_W, BF16),
              (MIX_W, BF16), (MIX_W, F32), (MIX_W, F32), (3 * D_MODEL, BF16)]
    return pl.pallas_call(
        functools.partial(_inproj_kernel, rope),
        grid=(n_tok // tm,),
        in_specs=in_specs,
        out_specs=[tok(w) for w, _ in widths],
        out_shape=[jax.ShapeDtypeStruct((n_tok, w), dt) for w, dt in widths],
        compiler_params=_cparams("parallel"),
        name="inproj",
    )(*args)


def _attn_kernel(lam_init, n_seg, q_ref, *refs):
    k_refs = refs[:n_seg]
    v_refs = refs[n_seg:2 * n_seg]
    lamp_ref, subg_ref, o_ref = refs[2 * n_seg:]
    lv = lamp_ref[...]
    lam = (jnp.exp(jnp.sum(lv[0:1] * lv[1:2], axis=-1, keepdims=True))
           - jnp.exp(jnp.sum(lv[2:3] * lv[3:4], axis=-1, keepdims=True)) + lam_init)
    hd = 2 * A_HEAD_DIM
    tq = q_ref.shape[0]
    for h in range(A_HEADS):
        sl = slice(h * hd, (h + 1) * hd)
        qh = q_ref[:, sl]
        first = lax.broadcasted_iota(jnp.int32, qh.shape, 1) < A_HEAD_DIM
        zero = jnp.zeros_like(qh)
        q2 = jnp.concatenate([jnp.where(first, qh, zero), jnp.where(first, zero, qh)], axis=0)
        scores = [_dot_nt(q2, k_ref[:, sl].astype(BF16)) for k_ref in k_refs]
        m = functools.reduce(jnp.maximum, [jnp.max(s, axis=-1, keepdims=True) for s in scores])
        acc = None
        for s, v_ref in zip(scores, v_refs):
            e = jnp.exp(s - m).astype(BF16)
            v_ext = jnp.concatenate([v_ref[:, sl].astype(BF16), jnp.ones((v_ref.shape[0], hd), BF16)], axis=1)
            pv = _dot(e, v_ext)
            acc = pv if acc is None else acc + pv
        o2 = acc[:, :hd] / acc[:, hd:]
        o = o2[:tq] - lam * o2[tq:]
        o_ref[:, sl] = (_rms(o) * subg_ref[:, sl] * (1.0 - lam_init)).astype(o_ref.dtype)


def _diff_attention(q, ks, vs, lamp, subg, lam_init):
    bsz, lq, w = q.shape
    tq = ATTN_Q_TILE
    kv_spec = lambda a: pl.BlockSpec((None, a.shape[1], w), lambda b, i: (b, 0, 0))
    return pl.pallas_call(
        functools.partial(_attn_kernel, lam_init, len(ks)),
        grid=(bsz, lq // tq),
        in_specs=[pl.BlockSpec((None, tq, w), lambda b, i: (b, i, 0))]
                 + [kv_spec(a) for a in ks] + [kv_spec(a) for a in vs]
                 + [pl.BlockSpec(lamp.shape, lambda b, i: (0, 0)), pl.BlockSpec(subg.shape, lambda b, i: (0, 0))],
        out_specs=pl.BlockSpec((None, tq, w), lambda b, i: (b, i, 0)),
        out_shape=jax.ShapeDtypeStruct((bsz, lq, w), BF16),
        compiler_params=_cparams("parallel", "parallel"),
        name="diff_attention",
    )(q, *ks, *vs, lamp, subg)


def _gla_masks(chunk):
    nlev = int(math.log2(chunk))
    assert 1 << nlev == chunk
    t = np.arange(chunk)[:, None]
    r = np.arange(chunk)[None, :]
    cum, pair = [], []
    for j in range(nlev + 1):
        start = (t >> j) << j
        end = start + (1 << j) - 1
        if j > 0:
            cum.append((r >= start) & (r <= t))
            cum.append((r > t) & (r <= end))
        if j < nlev:
            pair.append(((t >> (j + 1)) == (r >> (j + 1))) & (((t >> j) & 1) == 1) & (((r >> j) & 1) == 0))
    pair.append(t == r)
    dup = lambda m: np.concatenate([m, m], axis=1)
    cum_f = dup(np.concatenate(cum, 0).astype(np.float32))
    pair_f = np.stack(pair, 0).astype(np.float32)
    cum_b = dup(np.concatenate([m[::-1, ::-1] for m in cum], 0).astype(np.float32))
    pair_b = pair_f[:, ::-1, ::-1]
    return (jnp.asarray(np.stack([cum_f, cum_b]), BF16), jnp.asarray(np.stack([pair_f, pair_b]), F32), nlev)


def _gla_chunk(qk, v, la, cum, pair_ref, d, st_ref, bd, nlev, last_row):
    c = qk.shape[0]
    kw = B_HEADS * B_KDIM
    q = qk[:, :kw].astype(F32)
    k = qk[:, kw:].astype(F32)
    hi, lo = _split_bf16(la)
    e = jnp.exp(_dot(cum, jnp.concatenate([hi, lo], axis=0)))
    lane = lax.broadcasted_iota(jnp.int32, (c, kw), 1)
    head_masks = [(lane >= h * B_KDIM) & (lane < (h + 1) * B_KDIM) for h in range(B_HEADS)]
    zero = jnp.zeros((c, kw), BF16)

    def scores(qf, kf, pm):
        qb = qf.astype(BF16)
        kb = kf.astype(BF16)
        stacked = _dot_nt(jnp.concatenate([jnp.where(m, qb, zero) for m in head_masks], axis=0), kb)
        return [pm * stacked[h * c:(h + 1) * c] for h in range(B_HEADS)]

    def factors(j):
        return e[(2 * j - 2) * c:(2 * j - 1) * c], e[(2 * j - 1) * c:(2 * j) * c]

    att = scores(q, k, pair_ref[d, nlev])
    for j in range(nlev):
        if j == 0:
            lev = scores(q * jnp.exp(la), k, pair_ref[d, 0])
        else:
            eq, ek = factors(j)
            lev = scores(q * eq, k * ek, pair_ref[d, j])
        att = [a + b for a, b in zip(att, lev)]
    eq, ek = factors(nlev)
    st = st_ref[...]
    o = _dot_nt((q * eq).astype(BF16), st.astype(BF16))
    outs = []
    for h in range(B_HEADS):
        sl = slice(h * B_VDIM, (h + 1) * B_VDIM)
        outs.append(o[:, sl] + _dot(att[h].astype(BF16), v[:, sl]))
    dec = eq[last_row:last_row + 1, :]
    st_ref[...] = st * dec + bd * _dot_tn(v, (k * ek).astype(BF16))
    return outs


def _gla_kernel(has_s0, nlev, *refs):
    if has_s0:
        s0_ref, refs = refs[0], refs[1:]
    (qkf_ref, vf_ref, laf_ref, qkb_ref, vb_ref, lab_ref, cum_ref, pair_ref, bd_ref,
     of_ref, ob_ref, sfin_ref, st_f, st_b) = refs
    i = pl.program_id(1)
    c = qkf_ref.shape[0]

    @pl.when(i == 0)
    def _():
        for d, st in enumerate((st_f, st_b)):
            if has_s0:
                zero = jnp.zeros((B_KDIM, B_VDIM), F32)
                full = jnp.concatenate(
                    [jnp.concatenate([s0_ref[d, h] if hh == h else zero for hh in range(B_HEADS)], axis=1)
                     for h in range(B_HEADS)], axis=0)
                st[...] = full.T
            else:
                st[...] = jnp.zeros_like(st)

    bd = bd_ref[...]
    outs = _gla_chunk(qkf_ref[...], vf_ref[...], laf_ref[...], cum_ref[0], pair_ref, 0, st_f, bd, nlev, c - 1)
    for h, o in enumerate(outs):
        of_ref[:, h * B_VDIM:(h + 1) * B_VDIM] = o
    outs = _gla_chunk(qkb_ref[...], vb_ref[...], lab_ref[...], cum_ref[1], pair_ref, 1, st_b, bd, nlev, 0)
    for h, o in enumerate(outs):
        ob_ref[:, h * B_VDIM:(h + 1) * B_VDIM] = o

    @pl.when(i == pl.num_programs(1) - 1)
    def _():
        for d, st in enumerate((st_f, st_b)):
            full = st[...].T
            for h in range(B_HEADS):
                sfin_ref[d, h] = full[h * B_KDIM:(h + 1) * B_KDIM, h * B_VDIM:(h + 1) * B_VDIM]


def _gla(gqk, gv, la, s0):
    bsz, n_tok, _ = gqk.shape
    c = GLA_CHUNK
    n = n_tok // c
    cum, pair, nlev = _gla_masks(c)
    kw = B_HEADS * B_KDIM
    vw = B_HEADS * B_VDIM
    rows = np.arange(vw)[:, None] // B_VDIM
    cols = np.arange(kw)[None, :] // B_KDIM
    bd = jnp.asarray((rows == cols).astype(np.float32))
    fwd = lambda w, off=0: pl.BlockSpec((None, c, w), lambda b, i: (b, i, off))
    bwd = lambda w, off=0: pl.BlockSpec((None, c, w), lambda b, i: (b, n - 1 - i, off))
    in_specs = [fwd(2 * kw), fwd(vw), fwd(kw, 0), bwd(2 * kw), bwd(vw), bwd(kw, 1),
                _resident(cum.shape), _resident(pair.shape), _resident(bd.shape)]
    args = [gqk, gv, la, gqk, gv, la, cum, pair, bd]
    state_spec = pl.BlockSpec((None, 2, B_HEADS, B_KDIM, B_VDIM), lambda b, i: (b, 0, 0, 0, 0))
    if s0 is not None:
        in_specs.insert(0, state_spec)
        args.insert(0, s0)
    return pl.pallas_call(
        functools.partial(_gla_kernel, s0 is not None, nlev),
        grid=(bsz, n),
        in_specs=in_specs,
        out_specs=[fwd(vw), bwd(vw), state_spec],
        out_shape=[jax.ShapeDtypeStruct((bsz, n_tok, vw), F32), jax.ShapeDtypeStruct((bsz, n_tok, vw), F32),
                   jax.ShapeDtypeStruct((bsz, 2, B_HEADS, B_KDIM, B_VDIM), F32)],
        scratch_shapes=[pltpu.VMEM((vw, kw), F32), pltpu.VMEM((vw, kw), F32)],
        compiler_params=_cparams("parallel", "arbitrary"),
        name="gla",
    )(*args)


def _s5_prep_kernel(lr_ref, li_ref, dt_ref, bre_ref, bim_ref, cre_ref, cim_ref,
                    toep_ref, wst_ref, cst_ref, apow_ref):
    t_len = S5_CHUNK
    grp = S5_GROUP
    rows = t_len * grp
    lr = lr_ref[...]
    li = li_ref[...]
    dt = jnp.exp(dt_ref[...])
    a = lr * dt
    th = li * dt
    mag = jnp.exp(a)
    ar = mag * jnp.cos(th)
    ai = mag * jnp.sin(th)
    den = lr * lr + li * li
    fr = ((ar - 1.0) * lr + ai * li) / den
    fi = (ai * lr - (ar - 1.0) * li) / den
    b_re = bre_ref[...]
    b_im = bim_ref[...]
    bbr = jnp.concatenate([fr * b_re - fi * b_im] * t_len, axis=0)
    bbi = jnp.concatenate([fr * b_im + fi * b_re] * t_len, axis=0)
    c_re = cre_ref[...]
    c_im = cim_ref[...]
    c_re_t = jnp.concatenate([c_re] * t_len, axis=0)
    c_im_t = jnp.concatenate([c_im] * t_len, axis=0)

    n_pow = -(-(t_len + 1) // SUBLANES) * SUBLANES
    ex = lax.broadcasted_iota(jnp.int32, (n_pow, LANES), 0).astype(F32)
    pmag = jnp.exp(ex * a)
    pow_re = pmag * jnp.cos(ex * th)
    pow_im = pmag * jnp.sin(ex * th)
    is_fwd = lax.broadcasted_iota(jnp.int32, (grp, LANES), 1) < S5_P

    def expand(tab, exp_fwd, exp_bwd):
        blocks = []
        for step in range(t_len):
            f = jnp.broadcast_to(tab[exp_fwd(step):exp_fwd(step) + 1, :], (grp, LANES))
            b = jnp.broadcast_to(tab[exp_bwd(step):exp_bwd(step) + 1, :], (grp, LANES))
            blocks.append(jnp.where(is_fwd, f, b))
        return jnp.concatenate(blocks, axis=0)

    pr = expand(pow_re, lambda s: t_len - 1 - s, lambda s: s)
    pi = expand(pow_im, lambda s: t_len - 1 - s, lambda s: s)
    xr = bbr * pr - bbi * pi
    xi = bbr * pi + bbi * pr
    wst_ref[...] = jnp.concatenate([xr, xi], axis=1).astype(wst_ref.dtype)

    fwd_rows = lax.broadcasted_iota(jnp.int32, (rows, LANES), 1) < S5_P
    c_re_rep = jnp.concatenate([c_re] * (LANES // grp), axis=0)
    c_im_rep = jnp.concatenate([c_im] * (LANES // grp), axis=0)
    zero = jnp.zeros_like(xr)

    def lag_kernel(keep):
        return (_dot_f32(jnp.where(keep, xr, zero), c_re_rep, nt=True)
                - _dot_f32(jnp.where(keep, xi, zero), c_im_rep, nt=True))

    ker_f = lag_kernel(fwd_rows)
    ker_b = lag_kernel(jnp.logical_not(fwd_rows))
    blk = lax.broadcasted_iota(jnp.int32, (rows, LANES), 1) // grp

    def column_block(t):
        up = (t_len - 1 - t) * grp
        down = t * grp
        f = ker_f[up:] if up == 0 else jnp.concatenate([ker_f[up:], jnp.zeros((up, LANES), F32)], axis=0)
        b = ker_b if down == 0 else jnp.concatenate([jnp.zeros((down, LANES), F32), ker_b[:rows - down]], axis=0)
        return f + b

    per_tile = LANES // grp
    for h in range(t_len // per_tile):
        tile = column_block(h * per_tile)
        for j in range(1, per_tile):
            tile = jnp.where(blk == j, column_block(h * per_tile + j), tile)
        toep_ref[:, h * LANES:(h + 1) * LANES] = tile.astype(toep_ref.dtype)

    pr = expand(pow_re, lambda t: t + 1, lambda t: t_len - t)
    pi = expand(pow_im, lambda t: t + 1, lambda t: t_len - t)
    cr = c_re_t * pr - c_im_t * pi
    ci = -(c_re_t * pi + c_im_t * pr)
    cst_ref[...] = jnp.concatenate([jnp.where(fwd_rows, cr, zero), jnp.where(fwd_rows, ci, zero),
                                    jnp.where(fwd_rows, zero, cr), jnp.where(fwd_rows, zero, ci)],
                                   axis=1).astype(cst_ref.dtype)
    apow_ref[...] = jnp.concatenate([pow_re[t_len:t_len + 1], pow_im[t_len:t_len + 1],
                                     jnp.zeros((SUBLANES - 2, LANES), F32)], axis=0)


def _s5_prep(lr, li, ldt, bre_t, bim_t, cre, cim):
    g = lr.shape[0]
    rows = S5_CHUNK * S5_GROUP
    grp = lambda *s: pl.BlockSpec((None,) + s, lambda i: (i,) + (0,) * len(s))
    return pl.pallas_call(
        _s5_prep_kernel,
        grid=(g,),
        in_specs=[grp(1, LANES)] * 3 + [grp(S5_GROUP, LANES)] * 4,
        out_specs=[grp(rows, rows), grp(rows, 4 * S5_P), grp(rows, 8 * S5_P), grp(SUBLANES, LANES)],
        out_shape=[jax.ShapeDtypeStruct((g, rows, rows), BF16),
                   jax.ShapeDtypeStruct((g, rows, 4 * S5_P), BF16),
                   jax.ShapeDtypeStruct((g, rows, 8 * S5_P), BF16),
                   jax.ShapeDtypeStruct((g, SUBLANES, LANES), F32)],
        compiler_params=_cparams("parallel"),
        name="s5_prep",
    )(lr, li, ldt, bre_t, bim_t, cre, cim)


def _s5_kernel(n_chunks, bsz, cu_ref, toep_ref, wst_ref, cst_ref, apow_ref, h0_ref, y_ref, hfin_ref,
               u_scr, s_scr, h_scr, yg_scr):
    t_len = S5_CHUNK
    ng = S5_TILE_GROUPS
    rows = bsz * n_chunks
    rb = math.gcd(S5_ROW_BLOCK, rows)
    blk = lax.broadcasted_iota(jnp.int32, (rb, LANES), 1) // S5_GROUP

    def gather_blocks(pieces):
        acc = None
        for piece, src_blk, dst_blk in pieces:
            shift = (S5_GROUP * (dst_blk - src_blk)) % LANES
            v = piece if shift == 0 else pltpu.roll(piece, shift, 1)
            acc = v if acc is None else jnp.where(blk == dst_blk, v, acc)
        return acc

    def relayout_in(r, carry):
        r0 = pl.multiple_of(r * rb, rb)
        for h in range(t_len // ng):
            xs = [cu_ref[pl.ds(r0 * t_len + ng * h + b, rb, stride=t_len), :] for b in range(ng)]
            for g in range(ng):
                tile = gather_blocks([(xs[b], g, b) for b in range(ng)])
                u_scr[g, pl.ds(r0, rb), h * LANES:(h + 1) * LANES] = tile.astype(BF16)
        return carry

    lax.fori_loop(0, rows // rb, relayout_in, 0)

    half = 2 * S5_P
    for g in range(ng):
        s = _dot(u_scr[g], wst_ref[g])
        s_scr[g, 0] = s[:, :half]
        s_scr[g, 1] = s[:, half:]

    is_fwd = lax.broadcasted_iota(jnp.int32, (bsz, half), 1) < S5_P

    def scan_step(kk, hs):
        fwd_rows = pl.ds(kk, bsz, stride=n_chunks)
        bwd_rows = pl.ds(n_chunks - 1 - kk, bsz, stride=n_chunks)
        new = []
        for g in range(ng):
            re, im = hs[g]
            h_scr[g, 0, fwd_rows, :] = re
            h_scr[g, 1, fwd_rows, :] = im
            h_scr[g, 2, bwd_rows, :] = re
            h_scr[g, 3, bwd_rows, :] = im
            s_re = jnp.where(is_fwd, s_scr[g, 0, fwd_rows, :], s_scr[g, 0, bwd_rows, :])
            s_im = jnp.where(is_fwd, s_scr[g, 1, fwd_rows, :], s_scr[g, 1, bwd_rows, :])
            a_re = apow_ref[g, 0:1, :]
            a_im = apow_ref[g, 1:2, :]
            new.append((a_re * re - a_im * im + s_re, a_re * im + a_im * re + s_im))
        return tuple(new)

    hs = lax.fori_loop(0, n_chunks, scan_step,
                       tuple((h0_ref[g, :, :half], h0_ref[g, :, half:]) for g in range(ng)))
    for g in range(ng):
        hfin_ref[g, :, :half] = hs[g][0]
        hfin_ref[g, :, half:] = hs[g][1]
        h_all = jnp.concatenate([h_scr[g, j] for j in range(4)], axis=1).astype(BF16)
        yg_scr[g] = _dot(u_scr[g], toep_ref[g]) + _dot_nt(h_all, cst_ref[g])

    def relayout_out(r, carry):
        r0 = pl.multiple_of(r * rb, rb)
        for h in range(t_len // ng):
            ys = [yg_scr[g, pl.ds(r0, rb), h * LANES:(h + 1) * LANES] for g in range(ng)]
            for b in range(ng):
                tile = gather_blocks([(ys[g], b, g) for g in range(ng)])
                y_ref[pl.ds(r0 * t_len + ng * h + b, rb, stride=t_len), :] = tile
        return carry

    lax.fori_loop(0, rows // rb, relayout_out, 0)


def _s5(cu, toep, wst, cst, apow, h0, bsz):
    n_rows, width = cu.shape
    n_chunks = n_rows // bsz // S5_CHUNK
    rows = bsz * n_chunks
    ng = S5_TILE_GROUPS
    w = S5_CHUNK * S5_GROUP
    tile = lambda *s: pl.BlockSpec((ng,) + s, lambda j: (j,) + (0,) * len(s))
    lanes = pl.BlockSpec((n_rows, LANES), lambda j: (0, j))
    return pl.pallas_call(
        functools.partial(_s5_kernel, n_chunks, bsz),
        grid=(width // LANES,),
        in_specs=[lanes, tile(w, w), tile(w, 4 * S5_P), tile(w, 8 * S5_P), tile(SUBLANES, 2 * S5_P),
                  tile(bsz, 4 * S5_P)],
        out_specs=[lanes, tile(bsz, 4 * S5_P)],
        out_shape=[jax.ShapeDtypeStruct((n_rows, width), F32), jax.ShapeDtypeStruct((S5_G, bsz, 4 * S5_P), F32)],
        scratch_shapes=[pltpu.VMEM((ng, rows, w), BF16), pltpu.VMEM((ng, 2, rows, LANES), F32),
                        pltpu.VMEM((ng, 4, rows, LANES), F32), pltpu.VMEM((ng, rows, w), F32)],
        compiler_params=_cparams("parallel"),
        name="s5_scan",
    )(cu, toep, wst, cst, apow, h0)


def _merge_kernel(x_ref, mod_ref, oa_ref, of_ref, ob_ref, gg_ref, y_ref, cu_ref, gate_ref,
                  ong_ref, s5d_ref, wglu_ref, bglu_ref, wbr_ref, wout_ref, o_ref):
    g1 = mod_ref[...][:, 2 * D_MODEL:3 * D_MODEL]
    o_gla = of_ref[...] + ob_ref[...]
    parts = []
    for h in range(B_HEADS):
        sl = slice(h * B_VDIM, (h + 1) * B_VDIM)
        parts.append((_rms(o_gla[:, sl]) * ong_ref[...] * gg_ref[:, sl].astype(F32)).astype(BF16))
    ob = jnp.concatenate(parts, axis=1)

    cu = cu_ref[...].astype(F32)
    z = y_ref[...] + s5d_ref[...] * cu
    yc = 0.5 * z * (1.0 + jnp.tanh(math.sqrt(2.0 / math.pi) * (z + 0.044715 * (z * z * z))))
    glu = _dot(yc.astype(BF16), wglu_ref[...]) + bglu_ref[...]
    oc = (glu[:, :MIX_W] * jax.nn.sigmoid(glu[:, MIX_W:])).astype(BF16)

    merged = None
    for r, br in enumerate((oa_ref[...], ob, oc)):
        term = gate_ref[:, r * D_MODEL:(r + 1) * D_MODEL].astype(F32) * _dot(br, wbr_ref[r])
        merged = term if merged is None else merged + term
    o_ref[...] = x_ref[...] + g1 * _dot(merged.astype(BF16), wout_ref[...])


def _merge(x2, mod3, oa, of, ob, gg, y, cu, gates, ong, s5d, wglu, bglu, wbr, wout):
    n_tok = x2.shape[0]
    tm, mod_spec = _token_tiling(n_tok, mod3, DENSE_TILE)
    tok = lambda w: pl.BlockSpec((tm, w), lambda i: (i, 0))
    return pl.pallas_call(
        _merge_kernel,
        grid=(n_tok // tm,),
        in_specs=[tok(D_MODEL), mod_spec,
                  tok(MIX_W), tok(MIX_W), tok(MIX_W), tok(MIX_W), tok(MIX_W), tok(MIX_W), tok(3 * D_MODEL),
                  _resident(ong.shape), _resident(s5d.shape), _resident(wglu.shape), _resident(bglu.shape),
                  _resident(wbr.shape), _resident(wout.shape)],
        out_specs=tok(D_MODEL),
        out_shape=jax.ShapeDtypeStruct((n_tok, D_MODEL), F32),
        compiler_params=_cparams("parallel"),
        name="merge",
    )(x2, mod3, oa, of, ob, gg, y, cu, gates, ong, s5d, wglu, bglu, wbr, wout)


def _ffn_kernel(x_ref, mod_ref, n2_ref, wg_ref, wu_ref, wd_ref, o_ref):
    mod = mod_ref[...]
    sh2 = mod[:, 3 * D_MODEL:4 * D_MODEL]
    sc2 = mod[:, 4 * D_MODEL:5 * D_MODEL]
    g2 = mod[:, 5 * D_MODEL:6 * D_MODEL]
    x = x_ref[...]
    hb = (_rms(x) * n2_ref[...] * (1.0 + sc2) + sh2).astype(BF16)
    gate = _dot(hb, wg_ref[...])
    act = (gate * jax.nn.sigmoid(gate) * _dot(hb, wu_ref[...])).astype(BF16)
    o_ref[...] = x + g2 * _dot(act, wd_ref[...])


def _ffn(x2, mod3, n2g, wg, wu, wd):
    n_tok = x2.shape[0]
    tm, mod_spec = _token_tiling(n_tok, mod3, DENSE_TILE)
    tok = pl.BlockSpec((tm, D_MODEL), lambda i: (i, 0))
    return pl.pallas_call(
        _ffn_kernel,
        grid=(n_tok // tm,),
        in_specs=[tok, mod_spec,
                  _resident(n2g.shape), _resident(wg.shape), _resident(wu.shape), _resident(wd.shape)],
        out_specs=tok,
        out_shape=jax.ShapeDtypeStruct((n_tok, D_MODEL), F32),
        compiler_params=_cparams("parallel"),
        name="swiglu",
    )(x2, mod3, n2g, wg, wu, wd)


def _rope_tables(n_tok):
    t = np.arange(n_tok)
    row = (t // GRID_W).astype(np.float32)
    col = (t % GRID_W).astype(np.float32)
    half = A_HEAD_DIM // 2
    inv = jnp.asarray(ROPE_THETA, F32) ** (-jnp.arange(0, half, 2, dtype=F32) / half)
    ang_r = jnp.asarray(row)[:, None] * inv
    ang_c = jnp.asarray(col)[:, None] * inv
    cos = jnp.concatenate([jnp.cos(ang_r)] * 2 + [jnp.cos(ang_c)] * 2, axis=-1)
    sin = jnp.concatenate([-jnp.sin(ang_r), jnp.sin(ang_r), -jnp.sin(ang_c), jnp.sin(ang_c)], axis=-1)
    reps = MIX_W // A_HEAD_DIM
    return jnp.tile(cos, (1, reps)), jnp.tile(sin, (1, reps))


def _prepare_layer(l, w):
    w_in = w['w_in'][l]
    br0, br1 = OFF_CU, OFF_CU + 2 * GLA_RANK
    w_br = w_in[:, br0:br1]
    w_all = jnp.concatenate([w_in[:, :br0], w_in[:, br1:], w_br, w_br, w_br,
                             jnp.zeros((D_MODEL, BR_PAD - 6 * GLA_RANK), F32)], axis=1).astype(BF16)
    kw = B_HEADS * B_KDIM
    zk = jnp.zeros((GLA_RANK, kw), F32)
    wa1 = jnp.concatenate([jnp.concatenate([w['gla_wa2'][l, 0], zk], axis=1),
                           jnp.concatenate([zk, w['gla_wa2'][l, 1]], axis=1)], axis=0)
    wa = jnp.concatenate([wa1, wa1, wa1, jnp.zeros((BR_PAD - 6 * GLA_RANK, 2 * kw), F32)], axis=0)
    ba = w['gla_ba'][l].reshape(1, 2 * kw)
    gidx = np.arange(MIX_W) // A_HEAD_DIM
    member = (gidx[:, None] == np.arange(LANES)[None, :]).astype(np.float32)
    gsum = jnp.asarray(member / A_HEAD_DIM, BF16)
    gspread = jnp.asarray(member.T, BF16)
    vec = lambda a: a.transpose(1, 0, 2).reshape(S5_G, 1, 2 * S5_P)
    mat = lambda a: a.transpose(1, 2, 0, 3).reshape(S5_G, S5_GROUP, 2 * S5_P)
    ldt = jnp.broadcast_to(w['s5_log_dt'][l][:, :, None], (2, S5_G, S5_P))
    toep, wst, cst, ap = _s5_prep(
        vec(w['s5_lam_re'][l]), vec(w['s5_lam_im'][l]), vec(ldt),
        mat(jnp.swapaxes(w['s5_b_re'][l], -1, -2)), mat(jnp.swapaxes(w['s5_b_im'][l], -1, -2)),
        mat(w['s5_c_re'][l]), mat(w['s5_c_im'][l]))
    return dict(
        w_all=w_all, wa=wa, ba=ba, gsum=gsum, gspread=gspread,
        n1g=w['norm1_g'][l].reshape(1, -1), n2g=w['norm2_g'][l].reshape(1, -1),
        qg=jnp.tile(w['diff_qn_g'][l], MIX_W // A_HEAD_DIM).reshape(1, -1),
        kg=jnp.tile(w['diff_kn_g'][l], MIX_W // A_HEAD_DIM).reshape(1, -1),
        lamp=w['diff_lam'][l], subg=jnp.tile(w['diff_subln_g'][l], A_HEADS).reshape(1, -1),
        ong=w['gla_on_g'][l].reshape(1, -1), s5d=w['s5_d'][l].reshape(1, -1),
        toep=toep, wst=wst, cst=cst, apow=ap,
        wglu=w['s5_w_glu'][l].astype(BF16), bglu=w['s5_b_glu'][l].reshape(1, -1),
        wbr=w['w_branch'][l].astype(BF16), wout=w['w_out'][l].astype(BF16),
        wg=w['w_ffn_gate'][l].astype(BF16), wu=w['w_ffn_up'][l].astype(BF16), wd=w['w_ffn_down'][l].astype(BF16),
    )


def _layer(x, mod3, p, lam_init, ctx, rope_tabs):
    bsz, n_tok, _ = x.shape
    x2 = x.reshape(bsz * n_tok, D_MODEL)
    latent = ctx is not None
    q, k, v, gqk, gv, gg, la, cu, gates = _inproj(
        x2, mod3, p['n1g'], p['w_all'], p['gsum'], p['gspread'], p['qg'], p['kg'], p['wa'], p['ba'],
        rope_tabs if latent else None, BF16 if latent else F32)
    sh = lambda a: a.reshape(bsz, n_tok, a.shape[-1])

    if latent:
        keys = [ctx['k'].reshape(bsz, -1, MIX_W), sh(k)]
        vals = [ctx['v'].reshape(bsz, -1, MIX_W), sh(v)]
    else:
        keys, vals = [sh(k)], [sh(v)]
    oa = _diff_attention(sh(q), keys, vals, p['lamp'], p['subg'], lam_init)

    of, ob, new_gla = _gla(sh(gqk), sh(gv), sh(la), ctx['gla'] if latent else None)

    if latent:
        h0 = ctx['s5'].transpose(3, 0, 2, 1, 4).reshape(S5_G, bsz, 4 * S5_P)
    else:
        h0 = jnp.zeros((S5_G, bsz, 4 * S5_P), F32)
    y, hfin = _s5(cu, p['toep'], p['wst'], p['cst'], p['apow'], h0, bsz)

    x1 = _merge(x2, mod3, oa.reshape(bsz * n_tok, MIX_W), of.reshape(bsz * n_tok, MIX_W),
                ob.reshape(bsz * n_tok, MIX_W), gg, y, cu, gates, p['ong'], p['s5d'], p['wglu'], p['bglu'],
                p['wbr'], p['wout'])
    x_out = _ffn(x1, mod3, p['n2g'], p['wg'], p['wu'], p['wd']).reshape(bsz, n_tok, D_MODEL)
    if latent:
        return x_out, None
    new_k = sh(k).reshape(bsz, n_tok, A_HEADS, 2, A_HEAD_DIM)
    new_v = sh(v).reshape(bsz, n_tok, A_HEADS, 2 * A_HEAD_DIM)
    hf = hfin.reshape(S5_G, bsz, 2, 2, S5_P)
    new_s5 = hf.transpose(1, 3, 2, 0, 4)
    return x_out, (new_k, new_v, new_gla, new_s5)


def kernel(x_prompt, x_sample, cache_diff_k, cache_diff_v, state_gla, state_s5, c, c_ctx, w_mod, b_mod, norm1_g, norm2_g, w_in, diff_qn_g, diff_kn_g, diff_lam, diff_subln_g, gla_wa2, gla_ba, gla_on_g, s5_lam_re, s5_lam_im, s5_log_dt, s5_b_re, s5_b_im, s5_c_re, s5_c_im, s5_d, s5_w_glu, s5_b_glu, w_branch, w_out, w_ffn_gate, w_ffn_up, w_ffn_down):
    weights = dict(norm1_g=norm1_g, norm2_g=norm2_g, w_in=w_in, diff_qn_g=diff_qn_g, diff_kn_g=diff_kn_g,
                   diff_lam=diff_lam, diff_subln_g=diff_subln_g, gla_wa2=gla_wa2, gla_ba=gla_ba,
                   gla_on_g=gla_on_g, s5_lam_re=s5_lam_re, s5_lam_im=s5_lam_im, s5_log_dt=s5_log_dt,
                   s5_b_re=s5_b_re, s5_b_im=s5_b_im, s5_c_re=s5_c_re, s5_c_im=s5_c_im, s5_d=s5_d,
                   s5_w_glu=s5_w_glu, s5_b_glu=s5_b_glu, w_branch=w_branch, w_out=w_out,
                   w_ffn_gate=w_ffn_gate, w_ffn_up=w_ffn_up, w_ffn_down=w_ffn_down)
    depth = w_mod.shape[0]
    dec_b = c.shape[0]
    cond8 = jnp.concatenate([c_ctx[None, :], c, jnp.zeros((SUBLANES - 1 - dec_b, D_MODEL), F32)], axis=0)
    mod = _modulation(cond8, w_mod, b_mod)
    rope_tabs = _rope_tables(x_sample.shape[1])
    y_prompt, y_sample = x_prompt, x_sample
    k_list, v_list, gla_list, s5_list = [], [], [], []
    for l in range(depth):
        p = _prepare_layer(l, weights)
        lam_init = 0.8 - 0.6 * math.exp(-0.3 * l)
        y_prompt, (k_l, v_l, g_l, s_l) = _layer(y_prompt, mod[l, 0:1][:, None, :], p, lam_init, None, None)
        k_list.append(k_l)
        v_list.append(v_l)
        gla_list.append(g_l)
        s5_list.append(s_l)
        ctx = dict(k=cache_diff_k[:, l], v=cache_diff_v[:, l], gla=state_gla[:, l], s5=state_s5[:, l])
        y_sample, _ = _layer(y_sample, mod[l, 1:1 + dec_b][:, None, :], p, lam_init, ctx, rope_tabs)
    return (y_prompt, y_sample, jnp.stack(k_list, axis=1), jnp.stack(v_list, axis=1),
            jnp.stack(gla_list, axis=1), jnp.stack(s5_list, axis=1))
```

```python
import functools
import math

import numpy as np
import jax
import jax.numpy as jnp
from jax import lax
from jax.experimental import pallas as pl
from jax.experimental.pallas import tpu as pltpu

F32 = jnp.float32
BF16 = jnp.bfloat16

D_MODEL = 1024
MIX_W = 512
A_HEADS = 4
A_HEAD_DIM = 64
GRID_W = 64
ROPE_THETA = 10000.0
B_HEADS = 4
B_KDIM = 64
B_VDIM = 128
GLA_RANK = 16
GLA_TAU = 16.0
S5_G = 32
S5_GROUP = 16
S5_P = 64
FFN_DIM = 2816
EPS = 1e-6

VMEM_LIMIT_BYTES = 56 * 1024 * 1024
INPROJ_TILE = 256
DENSE_TILE = 512
ATTN_Q_TILE = 256
GLA_CHUNK = 128
S5_CHUNK = 16
SUBLANES = 8
LANES = 128
S5_TILE_GROUPS = LANES // 16
S5_ROW_BLOCK = 64

OFF_AQ, OFF_AK, OFF_AV, OFF_BQK, OFF_BV, OFF_BG, OFF_CU, OFF_GZ, OFF_BR = (
    0, 512, 1024, 1536, 2048, 2560, 3072, 3584, 6656)
BR_PAD = 128
IN_COLS = OFF_BR + BR_PAD


def _cparams(*sem):
    return pltpu.CompilerParams(dimension_semantics=sem, vmem_limit_bytes=VMEM_LIMIT_BYTES)


def _resident(shape):
    nd = len(shape)
    return pl.BlockSpec(shape, lambda *_: (0,) * nd, pipeline_mode=pl.Buffered(1))


def _split_bf16(x):
    hi = x.astype(BF16)
    lo = (x - hi.astype(F32)).astype(BF16)
    return hi, lo


def _dot(a, b):
    return jnp.dot(a, b, preferred_element_type=F32)


def _dot_nt(a, b):
    return lax.dot_general(a, b, (((1,), (1,)), ((), ())), preferred_element_type=F32)


def _dot_tn(a, b):
    return lax.dot_general(a, b, (((0,), (0,)), ((), ())), preferred_element_type=F32)


def _dot_f32(a, b, nt=False):
    d = _dot_nt if nt else _dot
    ah, al = _split_bf16(a)
    bh, bl = _split_bf16(b)
    return d(ah, bh) + d(ah, bl) + d(al, bh)


def _rms(x):
    return x * lax.rsqrt(jnp.mean(x * x, axis=-1, keepdims=True) + EPS)


def _mod_kernel(cond_ref, w_ref, b_ref, o_ref):
    cnd = cond_ref[...]
    s = (cnd * jax.nn.sigmoid(cnd))
    o_ref[...] = _dot_f32(s, w_ref[...]) + b_ref[...]


def _modulation(cond8, w_mod, b_mod):
    depth, d, n = w_mod.shape
    tn = 1536
    return pl.pallas_call(
        _mod_kernel,
        grid=(depth, n // tn),
        in_specs=[pl.BlockSpec((SUBLANES, d), lambda l, j: (0, 0)),
                  pl.BlockSpec((None, d, tn), lambda l, j: (l, 0, j)),
                  pl.BlockSpec((None, 1, tn), lambda l, j: (l, 0, j))],
        out_specs=pl.BlockSpec((None, SUBLANES, tn), lambda l, j: (l, 0, j)),
        out_shape=jax.ShapeDtypeStruct((depth, SUBLANES, n), F32),
        compiler_params=_cparams("parallel", "parallel"),
        name="modulation",
    )(cond8, w_mod, b_mod.reshape(depth, 1, n))


def _group_rms64(z, gsum, gspread, gain):
    hi, lo = _split_bf16(_dot((z * z).astype(BF16), gsum))
    ms = _dot(hi, gspread) + _dot(lo, gspread)
    return z * lax.rsqrt(ms + EPS) * gain


def _rope(z, c, s):
    n = z.shape[-1]
    lane = lax.broadcasted_iota(jnp.int32, z.shape, 1)
    first = (lane & 31) < 16
    partner = jnp.where(first, pltpu.roll(z, n - 16, 1), pltpu.roll(z, 16, 1))
    return z * c + partner * s


def _inproj_kernel(rope, x_ref, mod_ref, n1_ref, w_ref, gsum_ref, gspread_ref, qg_ref, kg_ref, wa_ref, ba_ref,
                   *rest):
    if rope:
        cos_ref, sin_ref = rest[:2]
        rest = rest[2:]
    q_ref, k_ref, v_ref, gqk_ref, gv_ref, gg_ref, la_ref, cu_ref, gate_ref = rest
    mod = mod_ref[...]
    sh1 = mod[:, 0:D_MODEL]
    sc1 = mod[:, D_MODEL:2 * D_MODEL]
    h = _rms(x_ref[...]) * n1_ref[...] * (1.0 + sc1) + sh1
    hb = h.astype(BF16)
    gsum = gsum_ref[...]
    gspread = gspread_ref[...]

    def seg(a, b):
        return _dot(hb, w_ref[:, a:b])

    q = _group_rms64(seg(OFF_AQ, OFF_AK), gsum, gspread, qg_ref[...])
    k = _group_rms64(seg(OFF_AK, OFF_AV), gsum, gspread, kg_ref[...])
    if rope:
        c = cos_ref[...]
        s = sin_ref[...]
        q = _rope(q, c, s)
        k = _rope(k, c, s)
    q_ref[...] = (q * (A_HEAD_DIM ** -0.5)).astype(q_ref.dtype)
    k_ref[...] = k.astype(k_ref.dtype)
    v_ref[...] = seg(OFF_AV, OFF_BQK).astype(v_ref.dtype)

    bqk = seg(OFF_BQK, OFF_BV)
    lane = lax.broadcasted_iota(jnp.int32, bqk.shape, 1)
    gqk_ref[...] = jnp.where(lane < B_HEADS * B_KDIM, bqk * (B_KDIM ** -0.5), bqk).astype(gqk_ref.dtype)
    gv_ref[...] = seg(OFF_BV, OFF_BG).astype(gv_ref.dtype)
    bg = seg(OFF_BG, OFF_CU)
    gg_ref[...] = (bg * jax.nn.sigmoid(bg)).astype(gg_ref.dtype)
    cu_ref[...] = seg(OFF_CU, OFF_GZ).astype(cu_ref.dtype)
    gate_ref[...] = jax.nn.sigmoid(seg(OFF_GZ, OFF_BR)).astype(gate_ref.dtype)

    r = seg(OFF_BR, IN_COLS)
    r_hi, r_lo = _split_bf16(r)
    copy = lax.broadcasted_iota(jnp.int32, r.shape, 1) // (2 * GLA_RANK)
    wa_hi, wa_lo = _split_bf16(wa_ref[...])
    wcopy = lax.broadcasted_iota(jnp.int32, wa_hi.shape, 0) // (2 * GLA_RANK)
    pre = _dot(jnp.where(copy < 2, r_hi, r_lo), jnp.where(wcopy == 1, wa_lo, wa_hi)) + ba_ref[...]
    la_ref[...] = (jnp.minimum(pre, 0.0) - jnp.log1p(jnp.exp(-jnp.abs(pre)))) * (1.0 / GLA_TAU)


def _token_tiling(n_tok, mod3, want):
    tm = math.gcd(want, n_tok // mod3.shape[0])
    tiles_per_mod = (n_tok // tm) // mod3.shape[0]
    return tm, pl.BlockSpec((None, 1, mod3.shape[-1]), lambda i: (i // tiles_per_mod, 0, 0))


def _inproj(x2, mod3, n1g, w_all, gsum, gspread, qg, kg, wa, ba, rope_tabs, kv_dtype):
    n_tok = x2.shape[0]
    tm, mod_spec = _token_tiling(n_tok, mod3, INPROJ_TILE)
    rope = rope_tabs is not None
    tok = lambda w: pl.BlockSpec((tm, w), lambda i: (i, 0))
    in_specs = [tok(D_MODEL), mod_spec,
                _resident(n1g.shape), _resident(w_all.shape), _resident(gsum.shape), _resident(gspread.shape),
                _resident(qg.shape), _resident(kg.shape), _resident(wa.shape), _resident(ba.shape)]
    args = [x2, mod3, n1g, w_all, gsum, gspread, qg, kg, wa, ba]
    if rope:
        tiles_per_seq = rope_tabs[0].shape[0] // tm
        for t in rope_tabs:
            in_specs.append(pl.BlockSpec((tm, MIX_W), lambda i: (i % tiles_per_seq, 0)))
            args.append(t)
    widths = [(MIX_W, BF16), (MIX_W, kv_dtype), (MIX_W, kv_dtype), (MIX_W, BF16), (MIX_W, BF16),
              (MIX_W, BF16), (MIX_W, F32), (MIX_W, F32), (3 * D_MODEL, BF16)]
    return pl.pallas_call(
        functools.partial(_inproj_kernel, rope),
        grid=(n_tok // tm,),
        in_specs=in_specs,
        out_specs=[tok(w) for w, _ in widths],
        out_shape=[jax.ShapeDtypeStruct((n_tok, w), dt) for w, dt in widths],
        compiler_params=_cparams("parallel"),
        name="inproj",
    )(*args)


def _attn_kernel(lam_init, n_seg, q_ref, *refs):
    k_refs = refs[:n_seg]
    v_refs = refs[n_seg:2 * n_seg]
    lamp_ref, subg_ref, o_ref = refs[2 * n_seg:]
    lv = lamp_ref[...]
    lam = (jnp.exp(jnp.sum(lv[0:1] * lv[1:2], axis=-1, keepdims=True))
           - jnp.exp(jnp.sum(lv[2:3] * lv[3:4], axis=-1, keepdims=True)) + lam_init)
    hd = 2 * A_HEAD_DIM
    tq = q_ref.shape[0]
    for h in range(A_HEADS):
        sl = slice(h * hd, (h + 1) * hd)
        qh = q_ref[:, sl]
        first = lax.broadcasted_iota(jnp.int32, qh.shape, 1) < A_HEAD_DIM
        zero = jnp.zeros_like(qh)
        q2 = jnp.concatenate([jnp.where(first, qh, zero), jnp.where(first, zero, qh)], axis=0)
        scores = [_dot_nt(q2, k_ref[:, sl].astype(BF16)) for k_ref in k_refs]
        m = functools.reduce(jnp.maximum, [jnp.max(s, axis=-1, keepdims=True) for s in scores])
        acc = None
        for s, v_ref in zip(scores, v_refs):
            e = jnp.exp(s - m).astype(BF16)
            v_ext = jnp.concatenate([v_ref[:, sl].astype(BF16), jnp.ones((v_ref.shape[0], hd), BF16)], axis=1)
            pv = _dot(e, v_ext)
            acc = pv if acc is None else acc + pv
        o2 = acc[:, :hd] / acc[:, hd:]
        o = o2[:tq] - lam * o2[tq:]
        o_ref[:, sl] = (_rms(o) * subg_ref[:, sl] * (1.0 - lam_init)).astype(o_ref.dtype)


def _diff_attention(q, ks, vs, lamp, subg, lam_init):
    bsz, lq, w = q.shape
    tq = ATTN_Q_TILE
    kv_spec = lambda a: pl.BlockSpec((None, a.shape[1], w), lambda b, i: (b, 0, 0))
    return pl.pallas_call(
        functools.partial(_attn_kernel, lam_init, len(ks)),
        grid=(bsz, lq // tq),
        in_specs=[pl.BlockSpec((None, tq, w), lambda b, i: (b, i, 0))]
                 + [kv_spec(a) for a in ks] + [kv_spec(a) for a in vs]
                 + [pl.BlockSpec(lamp.shape, lambda b, i: (0, 0)), pl.BlockSpec(subg.shape, lambda b, i: (0, 0))],
        out_specs=pl.BlockSpec((None, tq, w), lambda b, i: (b, i, 0)),
        out_shape=jax.ShapeDtypeStruct((bsz, lq, w), BF16),
        compiler_params=_cparams("parallel", "parallel"),
        name="diff_attention",
    )(q, *ks, *vs, lamp, subg)


def _gla_masks(chunk):
    nlev = int(math.log2(chunk))
    assert 1 << nlev == chunk
    t = np.arange(chunk)[:, None]
    r = np.arange(chunk)[None, :]
    cum, pair = [], []
    for j in range(nlev + 1):
        start = (t >> j) << j
        end = start + (1 << j) - 1
        if j > 0:
            cum.append((r >= start) & (r <= t))
            cum.append((r > t) & (r <= end))
        if j < nlev:
            pair.append(((t >> (j + 1)) == (r >> (j + 1))) & (((t >> j) & 1) == 1) & (((r >> j) & 1) == 0))
    pair.append(t == r)
    dup = lambda m: np.concatenate([m, m], axis=1)
    cum_f = dup(np.concatenate(cum, 0).astype(np.float32))
    pair_f = np.stack(pair, 0).astype(np.float32)
    cum_b = dup(np.concatenate([m[::-1, ::-1] for m in cum], 0).astype(np.float32))
    pair_b = pair_f[:, ::-1, ::-1]
    return (jnp.asarray(np.stack([cum_f, cum_b]), BF16), jnp.asarray(np.stack([pair_f, pair_b]), F32), nlev)


def _gla_chunk(qk, v, la, cum, pair_ref, d, st_ref, bd, nlev, last_row):
    c = qk.shape[0]
    kw = B_HEADS * B_KDIM
    q = qk[:, :kw].astype(F32)
    k = qk[:, kw:].astype(F32)
    hi, lo = _split_bf16(la)
    e = jnp.exp(_dot(cum, jnp.concatenate([hi, lo], axis=0)))
    lane = lax.broadcasted_iota(jnp.int32, (c, kw), 1)
    head_masks = [(lane >= h * B_KDIM) & (lane < (h + 1) * B_KDIM) for h in range(B_HEADS)]
    zero = jnp.zeros((c, kw), BF16)

    def scores(qf, kf, pm):
        qb = qf.astype(BF16)
        kb = kf.astype(BF16)
        stacked = _dot_nt(jnp.concatenate([jnp.where(m, qb, zero) for m in head_masks], axis=0), kb)
        return [pm * stacked[h * c:(h + 1) * c] for h in range(B_HEADS)]

    def factors(j):
        return e[(2 * j - 2) * c:(2 * j - 1) * c], e[(2 * j - 1) * c:(2 * j) * c]

    att = scores(q, k, pair_ref[d, nlev])
    for j in range(nlev):
        if j == 0:
            lev = scores(q * jnp.exp(la), k, pair_ref[d, 0])
        else:
            eq, ek = factors(j)
            lev = scores(q * eq, k * ek, pair_ref[d, j])
        att = [a + b for a, b in zip(att, lev)]
    eq, ek = factors(nlev)
    st = st_ref[...]
    o = _dot_nt((q * eq).astype(BF16), st.astype(BF16))
    outs = []
    for h in range(B_HEADS):
        sl = slice(h * B_VDIM, (h + 1) * B_VDIM)
        outs.append(o[:, sl] + _dot(att[h].astype(BF16), v[:, sl]))
    dec = eq[last_row:last_row + 1, :]
    st_ref[...] = st * dec + bd * _dot_tn(v, (k * ek).astype(BF16))
    return outs


def _gla_kernel(has_s0, nlev, *refs):
    if has_s0:
        s0_ref, refs = refs[0], refs[1:]
    (qkf_ref, vf_ref, laf_ref, qkb_ref, vb_ref, lab_ref, cum_ref, pair_ref, bd_ref,
     of_ref, ob_ref, sfin_ref, st_f, st_b) = refs
    i = pl.program_id(1)
    c = qkf_ref.shape[0]

    @pl.when(i == 0)
    def _():
        for d, st in enumerate((st_f, st_b)):
            if has_s0:
                zero = jnp.zeros((B_KDIM, B_VDIM), F32)
                full = jnp.concatenate(
                    [jnp.concatenate([s0_ref[d, h] if hh == h else zero for hh in range(B_HEADS)], axis=1)
                     for h in range(B_HEADS)], axis=0)
                st[...] = full.T
            else:
                st[...] = jnp.zeros_like(st)

    bd = bd_ref[...]
    outs = _gla_chunk(qkf_ref[...], vf_ref[...], laf_ref[...], cum_ref[0], pair_ref, 0, st_f, bd, nlev, c - 1)
    for h, o in enumerate(outs):
        of_ref[:, h * B_VDIM:(h + 1) * B_VDIM] = o
    outs = _gla_chunk(qkb_ref[...], vb_ref[...], lab_ref[...], cum_ref[1], pair_ref, 1, st_b, bd, nlev, 0)
    for h, o in enumerate(outs):
        ob_ref[:, h * B_VDIM:(h + 1) * B_VDIM] = o

    @pl.when(i == pl.num_programs(1) - 1)
    def _():
        for d, st in enumerate((st_f, st_b)):
            full = st[...].T
            for h in range(B_HEADS):
                sfin_ref[d, h] = full[h * B_KDIM:(h + 1) * B_KDIM, h * B_VDIM:(h + 1) * B_VDIM]


def _gla(gqk, gv, la, s0):
    bsz, n_tok, _ = gqk.shape
    c = GLA_CHUNK
    n = n_tok // c
    cum, pair, nlev = _gla_masks(c)
    kw = B_HEADS * B_KDIM
    vw = B_HEADS * B_VDIM
    rows = np.arange(vw)[:, None] // B_VDIM
    cols = np.arange(kw)[None, :] // B_KDIM
    bd = jnp.asarray((rows == cols).astype(np.float32))
    fwd = lambda w, off=0: pl.BlockSpec((None, c, w), lambda b, i: (b, i, off))
    bwd = lambda w, off=0: pl.BlockSpec((None, c, w), lambda b, i: (b, n - 1 - i, off))
    in_specs = [fwd(2 * kw), fwd(vw), fwd(kw, 0), bwd(2 * kw), bwd(vw), bwd(kw, 1),
                _resident(cum.shape), _resident(pair.shape), _resident(bd.shape)]
    args = [gqk, gv, la, gqk, gv, la, cum, pair, bd]
    state_spec = pl.BlockSpec((None, 2, B_HEADS, B_KDIM, B_VDIM), lambda b, i: (b, 0, 0, 0, 0))
    if s0 is not None:
        in_specs.insert(0, state_spec)
        args.insert(0, s0)
    return pl.pallas_call(
        functools.partial(_gla_kernel, s0 is not None, nlev),
        grid=(bsz, n),
        in_specs=in_specs,
        out_specs=[fwd(vw), bwd(vw), state_spec],
        out_shape=[jax.ShapeDtypeStruct((bsz, n_tok, vw), F32), jax.ShapeDtypeStruct((bsz, n_tok, vw), F32),
                   jax.ShapeDtypeStruct((bsz, 2, B_HEADS, B_KDIM, B_VDIM), F32)],
        scratch_shapes=[pltpu.VMEM((vw, kw), F32), pltpu.VMEM((vw, kw), F32)],
        compiler_params=_cparams("parallel", "arbitrary"),
        name="gla",
    )(*args)


def _s5_prep_kernel(lr_ref, li_ref, dt_ref, bre_ref, bim_ref, cre_ref, cim_ref,
                    toep_ref, wst_ref, cst_ref, apow_ref):
    t_len = S5_CHUNK
    grp = S5_GROUP
    rows = t_len * grp
    lr = lr_ref[...]
    li = li_ref[...]
    dt = jnp.exp(dt_ref[...])
    a = lr * dt
    th = li * dt
    mag = jnp.exp(a)
    ar = mag * jnp.cos(th)
    ai = mag * jnp.sin(th)
    den = lr * lr + li * li
    fr = ((ar - 1.0) * lr + ai * li) / den
    fi = (ai * lr - (ar - 1.0) * li) / den
    b_re = bre_ref[...]
    b_im = bim_ref[...]
    bbr = jnp.concatenate([fr * b_re - fi * b_im] * t_len, axis=0)
    bbi = jnp.concatenate([fr * b_im + fi * b_re] * t_len, axis=0)
    c_re = cre_ref[...]
    c_im = cim_ref[...]
    c_re_t = jnp.concatenate([c_re] * t_len, axis=0)
    c_im_t = jnp.concatenate([c_im] * t_len, axis=0)

    n_pow = -(-(t_len + 1) // SUBLANES) * SUBLANES
    ex = lax.broadcasted_iota(jnp.int32, (n_pow, LANES), 0).astype(F32)
    pmag = jnp.exp(ex * a)
    pow_re = pmag * jnp.cos(ex * th)
    pow_im = pmag * jnp.sin(ex * th)
    is_fwd = lax.broadcasted_iota(jnp.int32, (grp, LANES), 1) < S5_P

    def expand(tab, exp_fwd, exp_bwd):
        blocks = []
        for step in range(t_len):
            f = jnp.broadcast_to(tab[exp_fwd(step):exp_fwd(step) + 1, :], (grp, LANES))
            b = jnp.broadcast_to(tab[exp_bwd(step):exp_bwd(step) + 1, :], (grp, LANES))
            blocks.append(jnp.where(is_fwd, f, b))
        return jnp.concatenate(blocks, axis=0)

    pr = expand(pow_re, lambda s: t_len - 1 - s, lambda s: s)
    pi = expand(pow_im, lambda s: t_len - 1 - s, lambda s: s)
    xr = bbr * pr - bbi * pi
    xi = bbr * pi + bbi * pr
    wst_ref[...] = jnp.concatenate([xr, xi], axis=1).astype(wst_ref.dtype)

    fwd_rows = lax.broadcasted_iota(jnp.int32, (rows, LANES), 1) < S5_P
    c_re_rep = jnp.concatenate([c_re] * (LANES // grp), axis=0)
    c_im_rep = jnp.concatenate([c_im] * (LANES // grp), axis=0)
    zero = jnp.zeros_like(xr)

    def lag_kernel(keep):
        return (_dot_f32(jnp.where(keep, xr, zero), c_re_rep, nt=True)
                - _dot_f32(jnp.where(keep, xi, zero), c_im_rep, nt=True))

    ker_f = lag_kernel(fwd_rows)
    ker_b = lag_kernel(jnp.logical_not(fwd_rows))
    blk = lax.broadcasted_iota(jnp.int32, (rows, LANES), 1) // grp

    def column_block(t):
        up = (t_len - 1 - t) * grp
        down = t * grp
        f = ker_f[up:] if up == 0 else jnp.concatenate([ker_f[up:], jnp.zeros((up, LANES), F32)], axis=0)
        b = ker_b if down == 0 else jnp.concatenate([jnp.zeros((down, LANES), F32), ker_b[:rows - down]], axis=0)
        return f + b

    per_tile = LANES // grp
    for h in range(t_len // per_tile):
        tile = column_block(h * per_tile)
        for j in range(1, per_tile):
            tile = jnp.where(blk == j, column_block(h * per_tile + j), tile)
        toep_ref[:, h * LANES:(h + 1) * LANES] = tile.astype(toep_ref.dtype)

    pr = expand(pow_re, lambda t: t + 1, lambda t: t_len - t)
    pi = expand(pow_im, lambda t: t + 1, lambda t: t_len - t)
    cr = c_re_t * pr - c_im_t * pi
    ci = -(c_re_t * pi + c_im_t * pr)
    cst_ref[...] = jnp.concatenate([jnp.where(fwd_rows, cr, zero), jnp.where(fwd_rows, ci, zero),
                                    jnp.where(fwd_rows, zero, cr), jnp.where(fwd_rows, zero, ci)],
                                   axis=1).astype(cst_ref.dtype)
    apow_ref[...] = jnp.concatenate([pow_re[t_len:t_len + 1], pow_im[t_len:t_len + 1],
                                     jnp.zeros((SUBLANES - 2, LANES), F32)], axis=0)


def _s5_prep(lr, li, ldt, bre_t, bim_t, cre, cim):
    g = lr.shape[0]
    rows = S5_CHUNK * S5_GROUP
    grp = lambda *s: pl.BlockSpec((None,) + s, lambda i: (i,) + (0,) * len(s))
    return pl.pallas_call(
        _s5_prep_kernel,
        grid=(g,),
        in_specs=[grp(1, LANES)] * 3 + [grp(S5_GROUP, LANES)] * 4,
        out_specs=[grp(rows, rows), grp(rows, 4 * S5_P), grp(rows, 8 * S5_P), grp(SUBLANES, LANES)],
        out_shape=[jax.ShapeDtypeStruct((g, rows, rows), BF16),
                   jax.ShapeDtypeStruct((g, rows, 4 * S5_P), BF16),
                   jax.ShapeDtypeStruct((g, rows, 8 * S5_P), BF16),
                   jax.ShapeDtypeStruct((g, SUBLANES, LANES), F32)],
        compiler_params=_cparams("parallel"),
        name="s5_prep",
    )(lr, li, ldt, bre_t, bim_t, cre, cim)


def _block_transpose(xs, blk):
    n = len(xs)
    d = n // 2
    while d >= 1:
        low = (blk & d) == 0
        new = list(xs)
        for i in range(n):
            if i & d == 0:
                a, b = xs[i], xs[i | d]
                new[i] = jnp.where(low, a, pltpu.roll(b, d * S5_GROUP, 1))
                new[i | d] = jnp.where(low, pltpu.roll(a, LANES - d * S5_GROUP, 1), b)
        xs = new
        d //= 2
    return xs


def _s5_kernel(n_chunks, bp, splits, cu_ref, toep_ref, wst_ref, cst_ref, apow_ref, h0_ref, y_ref, hfin_ref,
               u_scr, s_scr, h_scr, yg_scr):
    t_len = S5_CHUNK
    ng = S5_TILE_GROUPS
    seq = n_chunks * t_len
    rows = bp * n_chunks
    rb = max(bp, math.gcd(S5_ROW_BLOCK, rows))
    cpb = rb // bp
    blk = lax.broadcasted_iota(jnp.int32, (rb, LANES), 1) // S5_GROUP

    def relayout_in(r, carry):
        r0 = pl.multiple_of(r * rb, rb)
        for h in range(t_len // ng):
            xs = []
            for b in range(ng):
                pieces = [cu_ref[pl.ds((r * cpb + j) * t_len + ng * h + b, bp, stride=seq), :] for j in range(cpb)]
                xs.append(pieces[0] if cpb == 1 else jnp.concatenate(pieces, axis=0))
            for g, tile in enumerate(_block_transpose(xs, blk)):
                u_scr[g, pl.ds(r0, rb), h * LANES:(h + 1) * LANES] = tile.astype(BF16)
        return carry

    lax.fori_loop(0, rows // rb, relayout_in, 0)

    half = 2 * S5_P
    for g in range(ng):
        s_scr[g] = _dot(u_scr[g], wst_ref[g])

    is_fwd = lax.broadcasted_iota(jnp.int32, (bp, half), 1) < S5_P

    def scan(init, store):
        def step(kk, hs):
            f0 = pl.multiple_of(kk * bp, bp)
            b0 = pl.multiple_of((n_chunks - 1 - kk) * bp, bp)
            new = []
            for g in range(ng):
                re, im = hs[g]
                if store:
                    h_scr[g, pl.ds(f0, bp), 0:half] = re
                    h_scr[g, pl.ds(f0, bp), half:2 * half] = im
                    h_scr[g, pl.ds(b0, bp), 2 * half:3 * half] = re
                    h_scr[g, pl.ds(b0, bp), 3 * half:4 * half] = im
                s_f = s_scr[g, pl.ds(f0, bp), :]
                s_b = s_scr[g, pl.ds(b0, bp), :]
                a_re = apow_ref[g, 0:1, :]
                a_im = apow_ref[g, 1:2, :]
                new.append((a_re * re - a_im * im + jnp.where(is_fwd, s_f[:, :half], s_b[:, :half]),
                            a_re * im + a_im * re + jnp.where(is_fwd, s_f[:, half:], s_b[:, half:])))
            return tuple(new)

        return lax.fori_loop(0, n_chunks, step, init)

    h0 = tuple((h0_ref[g, :, :half], h0_ref[g, :, half:]) for g in range(ng))
    init = h0
    if splits > 1:
        seg = lax.broadcasted_iota(jnp.int32, (bp, half), 0) & (splits - 1)
        take_prev = is_fwd & (seg >= 1)
        take_next = jnp.logical_not(is_fwd) & (seg <= splits - 2)
        for _ in range(splits - 1):
            fin = scan(init, store=False)
            init = tuple(tuple(part0 + jnp.where(take_prev, pltpu.roll(part, 1, 0),
                                                 jnp.where(take_next, pltpu.roll(part, bp - 1, 0), 0.0))
                               for part0, part in zip(h0[g], fin[g])) for g in range(ng))
    hs = scan(init, store=True)
    for g in range(ng):
        hfin_ref[g, :, :half] = hs[g][0]
        hfin_ref[g, :, half:] = hs[g][1]
        yg_scr[g] = _dot(u_scr[g], toep_ref[g]) + _dot_nt(h_scr[g].astype(BF16), cst_ref[g])

    def relayout_out(r, carry):
        r0 = pl.multiple_of(r * rb, rb)
        for h in range(t_len // ng):
            ys = [yg_scr[g, pl.ds(r0, rb), h * LANES:(h + 1) * LANES] for g in range(ng)]
            for b, tile in enumerate(_block_transpose(ys, blk)):
                for j in range(cpb):
                    y_ref[pl.ds((r * cpb + j) * t_len + ng * h + b, bp, stride=seq), :] = tile[j * bp:(j + 1) * bp]
        return carry

    lax.fori_loop(0, rows // rb, relayout_out, 0)


def _s5(cu, toep, wst, cst, apow, h0, bsz):
    n_rows, width = cu.shape
    splits = SUBLANES // bsz if SUBLANES % bsz == 0 else 1
    bp = bsz * splits
    n_chunks = n_rows // bp // S5_CHUNK
    rows = bp * n_chunks
    ng = S5_TILE_GROUPS
    w = S5_CHUNK * S5_GROUP
    p2 = 2 * S5_P
    h04 = h0.reshape(S5_G, bsz, 2, 2, S5_P)
    h0p = jnp.zeros((S5_G, bsz, splits, 2, 2, S5_P), F32)
    h0p = h0p.at[:, :, 0, :, 0].set(h04[:, :, :, 0]).at[:, :, splits - 1, :, 1].set(h04[:, :, :, 1])
    h0p = h0p.reshape(S5_G, bp, 2 * p2)
    tile = lambda *s: pl.BlockSpec((ng,) + s, lambda j: (j,) + (0,) * len(s))
    lanes = pl.BlockSpec((n_rows, LANES), lambda j: (0, j))
    y, hfin = pl.pallas_call(
        functools.partial(_s5_kernel, n_chunks, bp, splits),
        grid=(width // LANES,),
        in_specs=[lanes, tile(w, w), tile(w, 2 * p2), tile(w, 4 * p2), tile(SUBLANES, p2), tile(bp, 2 * p2)],
        out_specs=[lanes, tile(bp, 2 * p2)],
        out_shape=[jax.ShapeDtypeStruct((n_rows, width), F32), jax.ShapeDtypeStruct((S5_G, bp, 2 * p2), F32)],
        scratch_shapes=[pltpu.VMEM((ng, rows, w), BF16), pltpu.VMEM((ng, rows, 2 * p2), F32),
                        pltpu.VMEM((ng, rows, 4 * p2), F32), pltpu.VMEM((ng, rows, w), F32)],
        compiler_params=_cparams("parallel"),
        name="s5_scan",
    )(cu, toep, wst, cst, apow, h0p)
    hf = hfin.reshape(S5_G, bsz, splits, 2, 2, S5_P)
    hfin = jnp.stack([hf[:, :, splits - 1, :, 0], hf[:, :, 0, :, 1]], axis=3)
    return y, hfin.reshape(S5_G, bsz, 2 * p2)


def _merge_kernel(x_ref, mod_ref, oa_ref, of_ref, ob_ref, gg_ref, y_ref, cu_ref, gate_ref,
                  ong_ref, s5d_ref, wglu_ref, bglu_ref, wbr_ref, wout_ref, o_ref):
    g1 = mod_ref[...][:, 2 * D_MODEL:3 * D_MODEL]
    o_gla = of_ref[...] + ob_ref[...]
    parts = []
    for h in range(B_HEADS):
        sl = slice(h * B_VDIM, (h + 1) * B_VDIM)
        parts.append((_rms(o_gla[:, sl]) * ong_ref[...] * gg_ref[:, sl].astype(F32)).astype(BF16))
    ob = jnp.concatenate(parts, axis=1)

    cu = cu_ref[...].astype(F32)
    z = y_ref[...] + s5d_ref[...] * cu
    yc = 0.5 * z * (1.0 + jnp.tanh(math.sqrt(2.0 / math.pi) * (z + 0.044715 * (z * z * z))))
    glu = _dot(yc.astype(BF16), wglu_ref[...]) + bglu_ref[...]
    oc = (glu[:, :MIX_W] * jax.nn.sigmoid(glu[:, MIX_W:])).astype(BF16)

    merged = None
    for r, br in enumerate((oa_ref[...], ob, oc)):
        term = gate_ref[:, r * D_MODEL:(r + 1) * D_MODEL].astype(F32) * _dot(br, wbr_ref[r])
        merged = term if merged is None else merged + term
    o_ref[...] = x_ref[...] + g1 * _dot(merged.astype(BF16), wout_ref[...])


def _merge(x2, mod3, oa, of, ob, gg, y, cu, gates, ong, s5d, wglu, bglu, wbr, wout):
    n_tok = x2.shape[0]
    tm, mod_spec = _token_tiling(n_tok, mod3, DENSE_TILE)
    tok = lambda w: pl.BlockSpec((tm, w), lambda i: (i, 0))
    return pl.pallas_call(
        _merge_kernel,
        grid=(n_tok // tm,),
        in_specs=[tok(D_MODEL), mod_spec,
                  tok(MIX_W), tok(MIX_W), tok(MIX_W), tok(MIX_W), tok(MIX_W), tok(MIX_W), tok(3 * D_MODEL),
                  _resident(ong.shape), _resident(s5d.shape), _resident(wglu.shape), _resident(bglu.shape),
                  _resident(wbr.shape), _resident(wout.shape)],
        out_specs=tok(D_MODEL),
        out_shape=jax.ShapeDtypeStruct((n_tok, D_MODEL), F32),
        compiler_params=_cparams("parallel"),
        name="merge",
    )(x2, mod3, oa, of, ob, gg, y, cu, gates, ong, s5d, wglu, bglu, wbr, wout)


def _ffn_kernel(x_ref, mod_ref, n2_ref, wg_ref, wu_ref, wd_ref, o_ref):
    mod = mod_ref[...]
    sh2 = mod[:, 3 * D_MODEL:4 * D_MODEL]
    sc2 = mod[:, 4 * D_MODEL:5 * D_MODEL]
    g2 = mod[:, 5 * D_MODEL:6 * D_MODEL]
    x = x_ref[...]
    hb = (_rms(x) * n2_ref[...] * (1.0 + sc2) + sh2).astype(BF16)
    gate = _dot(hb, wg_ref[...])
    act = (gate * jax.nn.sigmoid(gate) * _dot(hb, wu_ref[...])).astype(BF16)
    o_ref[...] = x + g2 * _dot(act, wd_ref[...])


def _ffn(x2, mod3, n2g, wg, wu, wd):
    n_tok = x2.shape[0]
    tm, mod_spec = _token_tiling(n_tok, mod3, DENSE_TILE)
    tok = pl.BlockSpec((tm, D_MODEL), lambda i: (i, 0))
    return pl.pallas_call(
        _ffn_kernel,
        grid=(n_tok // tm,),
        in_specs=[tok, mod_spec,
                  _resident(n2g.shape), _resident(wg.shape), _resident(wu.shape), _resident(wd.shape)],
        out_specs=tok,
        out_shape=jax.ShapeDtypeStruct((n_tok, D_MODEL), F32),
        compiler_params=_cparams("parallel"),
        name="swiglu",
    )(x2, mod3, n2g, wg, wu, wd)


def _rope_tables(n_tok):
    t = np.arange(n_tok)
    row = (t // GRID_W).astype(np.float32)
    col = (t % GRID_W).astype(np.float32)
    half = A_HEAD_DIM // 2
    inv = jnp.asarray(ROPE_THETA, F32) ** (-jnp.arange(0, half, 2, dtype=F32) / half)
    ang_r = jnp.asarray(row)[:, None] * inv
    ang_c = jnp.asarray(col)[:, None] * inv
    cos = jnp.concatenate([jnp.cos(ang_r)] * 2 + [jnp.cos(ang_c)] * 2, axis=-1)
    sin = jnp.concatenate([-jnp.sin(ang_r), jnp.sin(ang_r), -jnp.sin(ang_c), jnp.sin(ang_c)], axis=-1)
    reps = MIX_W // A_HEAD_DIM
    return jnp.tile(cos, (1, reps)), jnp.tile(sin, (1, reps))


def _prepare_layer(l, w):
    w_in = w['w_in'][l]
    br0, br1 = OFF_CU, OFF_CU + 2 * GLA_RANK
    w_br = w_in[:, br0:br1]
    w_all = jnp.concatenate([w_in[:, :br0], w_in[:, br1:], w_br, w_br, w_br,
                             jnp.zeros((D_MODEL, BR_PAD - 6 * GLA_RANK), F32)], axis=1).astype(BF16)
    kw = B_HEADS * B_KDIM
    zk = jnp.zeros((GLA_RANK, kw), F32)
    wa1 = jnp.concatenate([jnp.concatenate([w['gla_wa2'][l, 0], zk], axis=1),
                           jnp.concatenate([zk, w['gla_wa2'][l, 1]], axis=1)], axis=0)
    wa = jnp.concatenate([wa1, wa1, wa1, jnp.zeros((BR_PAD - 6 * GLA_RANK, 2 * kw), F32)], axis=0)
    ba = w['gla_ba'][l].reshape(1, 2 * kw)
    gidx = np.arange(MIX_W) // A_HEAD_DIM
    member = (gidx[:, None] == np.arange(LANES)[None, :]).astype(np.float32)
    gsum = jnp.asarray(member / A_HEAD_DIM, BF16)
    gspread = jnp.asarray(member.T, BF16)
    vec = lambda a: a.transpose(1, 0, 2).reshape(S5_G, 1, 2 * S5_P)
    mat = lambda a: a.transpose(1, 2, 0, 3).reshape(S5_G, S5_GROUP, 2 * S5_P)
    ldt = jnp.broadcast_to(w['s5_log_dt'][l][:, :, None], (2, S5_G, S5_P))
    toep, wst, cst, ap = _s5_prep(
        vec(w['s5_lam_re'][l]), vec(w['s5_lam_im'][l]), vec(ldt),
        mat(jnp.swapaxes(w['s5_b_re'][l], -1, -2)), mat(jnp.swapaxes(w['s5_b_im'][l], -1, -2)),
        mat(w['s5_c_re'][l]), mat(w['s5_c_im'][l]))
    return dict(
        w_all=w_all, wa=wa, ba=ba, gsum=gsum, gspread=gspread,
        n1g=w['norm1_g'][l].reshape(1, -1), n2g=w['norm2_g'][l].reshape(1, -1),
        qg=jnp.tile(w['diff_qn_g'][l], MIX_W // A_HEAD_DIM).reshape(1, -1),
        kg=jnp.tile(w['diff_kn_g'][l], MIX_W // A_HEAD_DIM).reshape(1, -1),
        lamp=w['diff_lam'][l], subg=jnp.tile(w['diff_subln_g'][l], A_HEADS).reshape(1, -1),
        ong=w['gla_on_g'][l].reshape(1, -1), s5d=w['s5_d'][l].reshape(1, -1),
        toep=toep, wst=wst, cst=cst, apow=ap,
        wglu=w['s5_w_glu'][l].astype(BF16), bglu=w['s5_b_glu'][l].reshape(1, -1),
        wbr=w['w_branch'][l].astype(BF16), wout=w['w_out'][l].astype(BF16),
        wg=w['w_ffn_gate'][l].astype(BF16), wu=w['w_ffn_up'][l].astype(BF16), wd=w['w_ffn_down'][l].astype(BF16),
    )


def _layer(x, mod3, p, lam_init, ctx, rope_tabs):
    bsz, n_tok, _ = x.shape
    x2 = x.reshape(bsz * n_tok, D_MODEL)
    latent = ctx is not None
    q, k, v, gqk, gv, gg, la, cu, gates = _inproj(
        x2, mod3, p['n1g'], p['w_all'], p['gsum'], p['gspread'], p['qg'], p['kg'], p['wa'], p['ba'],
        rope_tabs if latent else None, BF16 if latent else F32)
    sh = lambda a: a.reshape(bsz, n_tok, a.shape[-1])

    if latent:
        keys = [ctx['k'].reshape(bsz, -1, MIX_W), sh(k)]
        vals = [ctx['v'].reshape(bsz, -1, MIX_W), sh(v)]
    else:
        keys, vals = [sh(k)], [sh(v)]
    oa = _diff_attention(sh(q), keys, vals, p['lamp'], p['subg'], lam_init)

    of, ob, new_gla = _gla(sh(gqk), sh(gv), sh(la), ctx['gla'] if latent else None)

    if latent:
        h0 = ctx['s5'].transpose(3, 0, 2, 1, 4).reshape(S5_G, bsz, 4 * S5_P)
    else:
        h0 = jnp.zeros((S5_G, bsz, 4 * S5_P), F32)
    y, hfin = _s5(cu, p['toep'], p['wst'], p['cst'], p['apow'], h0, bsz)

    x1 = _merge(x2, mod3, oa.reshape(bsz * n_tok, MIX_W), of.reshape(bsz * n_tok, MIX_W),
                ob.reshape(bsz * n_tok, MIX_W), gg, y, cu, gates, p['ong'], p['s5d'], p['wglu'], p['bglu'],
                p['wbr'], p['wout'])
    x_out = _ffn(x1, mod3, p['n2g'], p['wg'], p['wu'], p['wd']).reshape(bsz, n_tok, D_MODEL)
    if latent:
        return x_out, None
    new_k = sh(k).reshape(bsz, n_tok, A_HEADS, 2, A_HEAD_DIM)
    new_v = sh(v).reshape(bsz, n_tok, A_HEADS, 2 * A_HEAD_DIM)
    hf = hfin.reshape(S5_G, bsz, 2, 2, S5_P)
    new_s5 = hf.transpose(1, 3, 2, 0, 4)
    return x_out, (new_k, new_v, new_gla, new_s5)


def kernel(x_prompt, x_sample, cache_diff_k, cache_diff_v, state_gla, state_s5, c, c_ctx, w_mod, b_mod, norm1_g, norm2_g, w_in, diff_qn_g, diff_kn_g, diff_lam, diff_subln_g, gla_wa2, gla_ba, gla_on_g, s5_lam_re, s5_lam_im, s5_log_dt, s5_b_re, s5_b_im, s5_c_re, s5_c_im, s5_d, s5_w_glu, s5_b_glu, w_branch, w_out, w_ffn_gate, w_ffn_up, w_ffn_down):
    weights = dict(norm1_g=norm1_g, norm2_g=norm2_g, w_in=w_in, diff_qn_g=diff_qn_g, diff_kn_g=diff_kn_g,
                   diff_lam=diff_lam, diff_subln_g=diff_subln_g, gla_wa2=gla_wa2, gla_ba=gla_ba,
                   gla_on_g=gla_on_g, s5_lam_re=s5_lam_re, s5_lam_im=s5_lam_im, s5_log_dt=s5_log_dt,
                   s5_b_re=s5_b_re, s5_b_im=s5_b_im, s5_c_re=s5_c_re, s5_c_im=s5_c_im, s5_d=s5_d,
                   s5_w_glu=s5_w_glu, s5_b_glu=s5_b_glu, w_branch=w_branch, w_out=w_out,
                   w_ffn_gate=w_ffn_gate, w_ffn_up=w_ffn_up, w_ffn_down=w_ffn_down)
    depth = w_mod.shape[0]
    dec_b = c.shape[0]
    cond8 = jnp.concatenate([c_ctx[None, :], c, jnp.zeros((SUBLANES - 1 - dec_b, D_MODEL), F32)], axis=0)
    mod = _modulation(cond8, w_mod, b_mod)
    rope_tabs = _rope_tables(x_sample.shape[1])
    y_prompt, y_sample = x_prompt, x_sample
    k_list, v_list, gla_list, s5_list = [], [], [], []
    for l in range(depth):
        p = _prepare_layer(l, weights)
        lam_init = 0.8 - 0.6 * math.exp(-0.3 * l)
        y_prompt, (k_l, v_l, g_l, s_l) = _layer(y_prompt, mod[l, 0:1][:, None, :], p, lam_init, None, None)
        k_list.append(k_l)
        v_list.append(v_l)
        gla_list.append(g_l)
        s5_list.append(s_l)
        ctx = dict(k=cache_diff_k[:, l], v=cache_diff_v[:, l], gla=state_gla[:, l], s5=state_s5[:, l])
        y_sample, _ = _layer(y_sample, mod[l, 1:1 + dec_b][:, None, :], p, lam_init, ctx, rope_tabs)
    return (y_prompt, y_sample, jnp.stack(k_list, axis=1), jnp.stack(v_list, axis=1),
            jnp.stack(gla_list, axis=1), jnp.stack(s5_list, axis=1))
```

```python
import functools
import math

import numpy as np
import jax
import jax.numpy as jnp
from jax import lax
from jax.experimental import pallas as pl
from jax.experimental.pallas import tpu as pltpu

F32 = jnp.float32
BF16 = jnp.bfloat16

D_MODEL = 1024
MIX_W = 512
A_HEADS = 4
A_HEAD_DIM = 64
GRID_W = 64
ROPE_THETA = 10000.0
B_HEADS = 4
B_KDIM = 64
B_VDIM = 128
GLA_RANK = 16
GLA_TAU = 16.0
S5_G = 32
S5_GROUP = 16
S5_P = 64
FFN_DIM = 2816
EPS = 1e-6

VMEM_LIMIT_BYTES = 56 * 1024 * 1024
INPROJ_TILE = 256
DENSE_TILE = 512
ATTN_Q_TILE = 256
GLA_CHUNK = 128
GLA_STEP_CHUNKS = 4
S5_CHUNK = 16
SUBLANES = 8
LANES = 128
S5_TILE_GROUPS = LANES // 16
S5_ROW_BLOCK = 64

OFF_AQ, OFF_AK, OFF_AV, OFF_BQK, OFF_BV, OFF_BG, HEAD_COLS = 0, 512, 1024, 1536, 2048, 2560, 3072
TAIL_START = HEAD_COLS + 2 * GLA_RANK
TAIL_CU, TAIL_GZ, TAIL_COLS = 0, MIX_W, MIX_W + 3 * D_MODEL
BR_PAD = 128


def _cparams(*sem):
    return pltpu.CompilerParams(dimension_semantics=sem, vmem_limit_bytes=VMEM_LIMIT_BYTES)


def _resident(shape):
    nd = len(shape)
    return pl.BlockSpec(shape, lambda *_: (0,) * nd, pipeline_mode=pl.Buffered(1))


def _layer_block(arr, l, last=None):
    tail = arr.shape[1:] if last is None else arr.shape[1:-1] + (last,)
    return pl.BlockSpec((None,) + tail, lambda *_: (l,) + (0,) * len(tail), pipeline_mode=pl.Buffered(1))


def _split_bf16(x):
    hi = x.astype(BF16)
    lo = (x - hi.astype(F32)).astype(BF16)
    return hi, lo


def _dot(a, b):
    return jnp.dot(a, b, preferred_element_type=F32)


def _dot_nt(a, b):
    return lax.dot_general(a, b, (((1,), (1,)), ((), ())), preferred_element_type=F32)


def _dot_tn(a, b):
    return lax.dot_general(a, b, (((0,), (0,)), ((), ())), preferred_element_type=F32)


def _dot_f32(a, b, nt=False):
    d = _dot_nt if nt else _dot
    ah, al = _split_bf16(a)
    bh, bl = _split_bf16(b)
    return d(ah, bh) + d(ah, bl) + d(al, bh)


def _rms(x):
    return x * lax.rsqrt(jnp.mean(x * x, axis=-1, keepdims=True) + EPS)


def _mod_kernel(cond_ref, w_ref, b_ref, o_ref):
    cnd = cond_ref[...]
    s = (cnd * jax.nn.sigmoid(cnd))
    o_ref[...] = _dot_f32(s, w_ref[...]) + b_ref[...]


def _modulation(cond8, w_mod, b_mod):
    depth, d, n = w_mod.shape
    tn = 1536
    return pl.pallas_call(
        _mod_kernel,
        grid=(depth, n // tn),
        in_specs=[pl.BlockSpec((SUBLANES, d), lambda l, j: (0, 0)),
                  pl.BlockSpec((None, d, tn), lambda l, j: (l, 0, j)),
                  pl.BlockSpec((None, 1, tn), lambda l, j: (l, 0, j))],
        out_specs=pl.BlockSpec((None, SUBLANES, tn), lambda l, j: (l, 0, j)),
        out_shape=jax.ShapeDtypeStruct((depth, SUBLANES, n), F32),
        compiler_params=_cparams("parallel", "parallel"),
        name="modulation",
    )(cond8, w_mod, b_mod.reshape(depth, 1, n))


def _group_rms64(z, gsum, gspread, gain):
    hi, lo = _split_bf16(_dot((z * z).astype(BF16), gsum))
    ms = _dot(hi, gspread) + _dot(lo, gspread)
    return z * lax.rsqrt(ms + EPS) * gain


def _rope(z, c, s):
    n = z.shape[-1]
    lane = lax.broadcasted_iota(jnp.int32, z.shape, 1)
    first = (lane & 31) < 16
    partner = jnp.where(first, pltpu.roll(z, n - 16, 1), pltpu.roll(z, 16, 1))
    return z * c + partner * s


def _inproj_kernel(rope, x_ref, mod_ref, n1_ref, w_head_ref, w_tail_ref, w_rank_ref, gsum_ref, gspread_ref,
                   qg_ref, kg_ref, wa_ref, ba_ref, *rest):
    if rope:
        cos_ref, sin_ref = rest[:2]
        rest = rest[2:]
    q_ref, k_ref, v_ref, gqk_ref, gv_ref, gg_ref, la_ref, cu_ref, gate_ref = rest
    mod = mod_ref[...]
    sh1 = mod[:, 0:D_MODEL]
    sc1 = mod[:, D_MODEL:2 * D_MODEL]
    h = _rms(x_ref[...]) * n1_ref[...] * (1.0 + sc1) + sh1
    hb = h.astype(BF16)
    gsum = gsum_ref[...]
    gspread = gspread_ref[...]

    def seg(a, b, w_ref=w_head_ref):
        return _dot(hb, w_ref[:, a:b])

    r = _dot(hb, w_rank_ref[...])
    r_hi, r_lo = _split_bf16(r)
    copy = lax.broadcasted_iota(jnp.int32, r.shape, 1) // (2 * GLA_RANK)
    wa_hi, wa_lo = _split_bf16(wa_ref[...])
    wcopy = lax.broadcasted_iota(jnp.int32, wa_hi.shape, 0) // (2 * GLA_RANK)
    pre = _dot(jnp.where(copy < 2, r_hi, r_lo), jnp.where(wcopy == 1, wa_lo, wa_hi)) + ba_ref[...]
    la_ref[...] = (jnp.minimum(pre, 0.0) - jnp.log1p(jnp.exp(-jnp.abs(pre)))) * (1.0 / GLA_TAU)

    q = _group_rms64(seg(OFF_AQ, OFF_AK), gsum, gspread, qg_ref[...])
    k = _group_rms64(seg(OFF_AK, OFF_AV), gsum, gspread, kg_ref[...])
    if rope:
        c = cos_ref[...]
        s = sin_ref[...]
        q = _rope(q, c, s)
        k = _rope(k, c, s)
    q_ref[...] = (q * (A_HEAD_DIM ** -0.5)).astype(q_ref.dtype)
    k_ref[...] = k.astype(k_ref.dtype)

    gate_ref[...] = jax.nn.sigmoid(seg(TAIL_GZ, TAIL_COLS, w_tail_ref)).astype(gate_ref.dtype)
    bg = seg(OFF_BG, HEAD_COLS)
    gg_ref[...] = (bg * jax.nn.sigmoid(bg)).astype(gg_ref.dtype)
    bqk = seg(OFF_BQK, OFF_BV)
    lane = lax.broadcasted_iota(jnp.int32, bqk.shape, 1)
    gqk_ref[...] = jnp.where(lane < B_HEADS * B_KDIM, bqk * (B_KDIM ** -0.5), bqk).astype(gqk_ref.dtype)
    v_ref[...] = seg(OFF_AV, OFF_BQK).astype(v_ref.dtype)
    gv_ref[...] = seg(OFF_BV, OFF_BG).astype(gv_ref.dtype)
    cu_ref[...] = seg(TAIL_CU, TAIL_GZ, w_tail_ref).astype(cu_ref.dtype)


def _token_tiling(n_tok, mod3, want):
    tm = math.gcd(want, n_tok // mod3.shape[0])
    tiles_per_mod = (n_tok // tm) // mod3.shape[0]
    return tm, pl.BlockSpec((None, 1, mod3.shape[-1]), lambda i: (i // tiles_per_mod, 0, 0))


def _inproj(l, x2, mod3, n1g, w_in, w_tail, w_rank, gsum, gspread, qg, kg, wa, ba, rope_tabs, kv_dtype):
    n_tok = x2.shape[0]
    tm, mod_spec = _token_tiling(n_tok, mod3, INPROJ_TILE)
    rope = rope_tabs is not None
    tok = lambda w: pl.BlockSpec((tm, w), lambda i: (i, 0))
    in_specs = [tok(D_MODEL), mod_spec, _resident(n1g.shape),
                _layer_block(w_in, l, HEAD_COLS), _layer_block(w_tail, l), _layer_block(w_rank, l),
                _resident(gsum.shape), _resident(gspread.shape),
                _resident(qg.shape), _resident(kg.shape), _resident(wa.shape), _resident(ba.shape)]
    args = [x2, mod3, n1g, w_in, w_tail, w_rank, gsum, gspread, qg, kg, wa, ba]
    if rope:
        tiles_per_seq = rope_tabs[0].shape[0] // tm
        for t in rope_tabs:
            in_specs.append(pl.BlockSpec((tm, MIX_W), lambda i: (i % tiles_per_seq, 0)))
            args.append(t)
    widths = [(MIX_W, BF16), (MIX_W, kv_dtype), (MIX_W, kv_dtype), (MIX_W, BF16), (MIX_W, BF16),
              (MIX_W, BF16), (MIX_W, F32), (MIX_W, F32), (3 * D_MODEL, BF16)]
    return pl.pallas_call(
        functools.partial(_inproj_kernel, rope),
        grid=(n_tok // tm,),
        in_specs=in_specs,
        out_specs=[tok(w) for w, _ in widths],
        out_shape=[jax.ShapeDtypeStruct((n_tok, w), dt) for w, dt in widths],
        compiler_params=_cparams("parallel"),
        name="inproj",
    )(*args)


def _attn_kernel(lam_init, n_seg, q_ref, *refs):
    k_refs = refs[:n_seg]
    v_refs = refs[n_seg:2 * n_seg]
    lamp_ref, subg_ref, o_ref = refs[2 * n_seg:]
    lv = lamp_ref[...]
    lam = (jnp.exp(jnp.sum(lv[0:1] * lv[1:2], axis=-1, keepdims=True))
           - jnp.exp(jnp.sum(lv[2:3] * lv[3:4], axis=-1, keepdims=True)) + lam_init)
    hd = 2 * A_HEAD_DIM
    tq = q_ref.shape[0]
    for h in range(A_HEADS):
        sl = slice(h * hd, (h + 1) * hd)
        qh = q_ref[:, sl]
        first = lax.broadcasted_iota(jnp.int32, qh.shape, 1) < A_HEAD_DIM
        zero = jnp.zeros_like(qh)
        q2 = jnp.concatenate([jnp.where(first, qh, zero), jnp.where(first, zero, qh)], axis=0)
        scores = [_dot_nt(q2, k_ref[:, sl].astype(BF16)) for k_ref in k_refs]
        m = functools.reduce(jnp.maximum, [jnp.max(s, axis=-1, keepdims=True) for s in scores])
        acc = None
        for s, v_ref in zip(scores, v_refs):
            e = jnp.exp((s - m).astype(BF16))
            v_ext = jnp.concatenate([v_ref[:, sl].astype(BF16), jnp.ones((v_ref.shape[0], hd), BF16)], axis=1)
            pv = _dot(e, v_ext)
            acc = pv if acc is None else acc + pv
        o2 = acc[:, :hd] / acc[:, hd:]
        o = o2[:tq] - lam * o2[tq:]
        o_ref[:, sl] = (_rms(o) * subg_ref[:, sl] * (1.0 - lam_init)).astype(o_ref.dtype)


def _diff_attention(q, ks, vs, lamp, subg, lam_init):
    bsz, lq, w = q.shape
    tq = ATTN_Q_TILE
    kv_spec = lambda a: pl.BlockSpec((None, a.shape[1], w), lambda b, i: (b, 0, 0))
    return pl.pallas_call(
        functools.partial(_attn_kernel, lam_init, len(ks)),
        grid=(bsz, lq // tq),
        in_specs=[pl.BlockSpec((None, tq, w), lambda b, i: (b, i, 0))]
                 + [kv_spec(a) for a in ks] + [kv_spec(a) for a in vs]
                 + [pl.BlockSpec(lamp.shape, lambda b, i: (0, 0)), pl.BlockSpec(subg.shape, lambda b, i: (0, 0))],
        out_specs=pl.BlockSpec((None, tq, w), lambda b, i: (b, i, 0)),
        out_shape=jax.ShapeDtypeStruct((bsz, lq, w), BF16),
        compiler_params=_cparams("parallel", "parallel"),
        name="diff_attention",
    )(q, *ks, *vs, lamp, subg)


def _gla_masks(chunk):
    nlev = int(math.log2(chunk))
    assert 1 << nlev == chunk
    t = np.arange(chunk)[:, None]
    r = np.arange(chunk)[None, :]
    cum, pair = [], []
    for j in range(nlev + 1):
        start = (t >> j) << j
        end = start + (1 << j) - 1
        if j > 0:
            cum.append((r >= start) & (r <= t))
            cum.append((r > t) & (r <= end))
        if j < nlev:
            pair.append(((t >> (j + 1)) == (r >> (j + 1))) & (((t >> j) & 1) == 1) & (((r >> j) & 1) == 0))
    pair.append(t == r)
    dup = lambda m: np.concatenate([m, m], axis=1)
    cum_f = dup(np.concatenate(cum, 0).astype(np.float32))
    pair_f = np.stack(pair, 0).astype(np.float32)
    cum_b = dup(np.concatenate([m[::-1, ::-1] for m in cum], 0).astype(np.float32))
    pair_b = pair_f[:, ::-1, ::-1]
    return (jnp.asarray(np.stack([cum_f, cum_b]), BF16), jnp.asarray(np.stack([pair_f, pair_b]), F32), nlev)


def _gla_chunk(qk, v, la, cum, pair_ref, d, st_ref, bd, nlev, last_row):
    c = qk.shape[0]
    kw = B_HEADS * B_KDIM
    q = qk[:, :kw].astype(F32)
    k = qk[:, kw:].astype(F32)
    hi, lo = _split_bf16(la)
    e = jnp.exp(_dot(cum, jnp.concatenate([hi, lo], axis=0)))
    lane = lax.broadcasted_iota(jnp.int32, (c, kw), 1)
    head_masks = [(lane >= h * B_KDIM) & (lane < (h + 1) * B_KDIM) for h in range(B_HEADS)]
    zero = jnp.zeros((c, kw), BF16)

    def scores(qf, kf, pm):
        qb = qf.astype(BF16)
        kb = kf.astype(BF16)
        stacked = _dot_nt(jnp.concatenate([jnp.where(m, qb, zero) for m in head_masks], axis=0), kb)
        return [pm * stacked[h * c:(h + 1) * c] for h in range(B_HEADS)]

    def factors(j):
        return e[(2 * j - 2) * c:(2 * j - 1) * c], e[(2 * j - 1) * c:(2 * j) * c]

    att = scores(q, k, pair_ref[d, nlev])
    for j in range(nlev):
        if j == 0:
            lev = scores(q * jnp.exp(la), k, pair_ref[d, 0])
        else:
            eq, ek = factors(j)
            lev = scores(q * eq, k * ek, pair_ref[d, j])
        att = [a + b for a, b in zip(att, lev)]
    eq, ek = factors(nlev)
    st = st_ref[...]
    o = _dot_nt((q * eq).astype(BF16), st.astype(BF16))
    outs = []
    for h in range(B_HEADS):
        sl = slice(h * B_VDIM, (h + 1) * B_VDIM)
        outs.append(o[:, sl] + _dot(att[h].astype(BF16), v[:, sl]))
    dec = eq[last_row:last_row + 1, :]
    st_ref[...] = st * dec + bd * _dot_tn(v, (k * ek).astype(BF16))
    return outs


def _gla_kernel(has_s0, nlev, *refs):
    if has_s0:
        s0_ref, refs = refs[0], refs[1:]
    (qkf_ref, vf_ref, laf_ref, qkb_ref, vb_ref, lab_ref, cum_ref, pair_ref, bd_ref,
     of_ref, ob_ref, sfin_ref, st_f, st_b) = refs
    i = pl.program_id(1)
    c = GLA_CHUNK
    n_sub = qkf_ref.shape[0] // c

    @pl.when(i == 0)
    def _():
        for d, st in enumerate((st_f, st_b)):
            if has_s0:
                zero = jnp.zeros((B_KDIM, B_VDIM), F32)
                full = jnp.concatenate(
                    [jnp.concatenate([s0_ref[d, h] if hh == h else zero for hh in range(B_HEADS)], axis=1)
                     for h in range(B_HEADS)], axis=0)
                st[...] = full.T
            else:
                st[...] = jnp.zeros_like(st)

    bd = bd_ref[...]
    for sub in range(n_sub):
        rf = slice(sub * c, (sub + 1) * c)
        outs = _gla_chunk(qkf_ref[rf, :], vf_ref[rf, :], laf_ref[rf, :], cum_ref[0], pair_ref, 0, st_f, bd, nlev,
                          c - 1)
        for h, o in enumerate(outs):
            of_ref[rf, h * B_VDIM:(h + 1) * B_VDIM] = o
        rb = slice((n_sub - 1 - sub) * c, (n_sub - sub) * c)
        outs = _gla_chunk(qkb_ref[rb, :], vb_ref[rb, :], lab_ref[rb, :], cum_ref[1], pair_ref, 1, st_b, bd, nlev, 0)
        for h, o in enumerate(outs):
            ob_ref[rb, h * B_VDIM:(h + 1) * B_VDIM] = o

    @pl.when(i == pl.num_programs(1) - 1)
    def _():
        for d, st in enumerate((st_f, st_b)):
            full = st[...].T
            for h in range(B_HEADS):
                sfin_ref[d, h] = full[h * B_KDIM:(h + 1) * B_KDIM, h * B_VDIM:(h + 1) * B_VDIM]


def _gla(gqk, gv, la, s0):
    bsz, n_tok, _ = gqk.shape
    c = math.gcd(GLA_STEP_CHUNKS * GLA_CHUNK, n_tok)
    n = n_tok // c
    cum, pair, nlev = _gla_masks(GLA_CHUNK)
    kw = B_HEADS * B_KDIM
    vw = B_HEADS * B_VDIM
    rows = np.arange(vw)[:, None] // B_VDIM
    cols = np.arange(kw)[None, :] // B_KDIM
    bd = jnp.asarray((rows == cols).astype(np.float32))
    fwd = lambda w, off=0: pl.BlockSpec((None, c, w), lambda b, i: (b, i, off))
    bwd = lambda w, off=0: pl.BlockSpec((None, c, w), lambda b, i: (b, n - 1 - i, off))
    in_specs = [fwd(2 * kw), fwd(vw), fwd(kw, 0), bwd(2 * kw), bwd(vw), bwd(kw, 1),
                _resident(cum.shape), _resident(pair.shape), _resident(bd.shape)]
    args = [gqk, gv, la, gqk, gv, la, cum, pair, bd]
    state_spec = pl.BlockSpec((None, 2, B_HEADS, B_KDIM, B_VDIM), lambda b, i: (b, 0, 0, 0, 0))
    if s0 is not None:
        in_specs.insert(0, state_spec)
        args.insert(0, s0)
    return pl.pallas_call(
        functools.partial(_gla_kernel, s0 is not None, nlev),
        grid=(bsz, n),
        in_specs=in_specs,
        out_specs=[fwd(vw), bwd(vw), state_spec],
        out_shape=[jax.ShapeDtypeStruct((bsz, n_tok, vw), F32), jax.ShapeDtypeStruct((bsz, n_tok, vw), F32),
                   jax.ShapeDtypeStruct((bsz, 2, B_HEADS, B_KDIM, B_VDIM), F32)],
        scratch_shapes=[pltpu.VMEM((vw, kw), F32), pltpu.VMEM((vw, kw), F32)],
        compiler_params=_cparams("parallel", "arbitrary"),
        name="gla",
    )(*args)


def _s5_prep_kernel(lr_ref, li_ref, dt_ref, bre_ref, bim_ref, cre_ref, cim_ref,
                    toep_ref, wst_ref, cst_ref, apow_ref):
    t_len = S5_CHUNK
    grp = S5_GROUP
    rows = t_len * grp
    lr = lr_ref[...]
    li = li_ref[...]
    dt = jnp.exp(dt_ref[...])
    a = lr * dt
    th = li * dt
    mag = jnp.exp(a)
    ar = mag * jnp.cos(th)
    ai = mag * jnp.sin(th)
    den = lr * lr + li * li
    fr = ((ar - 1.0) * lr + ai * li) / den
    fi = (ai * lr - (ar - 1.0) * li) / den
    b_re = bre_ref[...]
    b_im = bim_ref[...]
    bbr = jnp.concatenate([fr * b_re - fi * b_im] * t_len, axis=0)
    bbi = jnp.concatenate([fr * b_im + fi * b_re] * t_len, axis=0)
    c_re = cre_ref[...]
    c_im = cim_ref[...]
    c_re_t = jnp.concatenate([c_re] * t_len, axis=0)
    c_im_t = jnp.concatenate([c_im] * t_len, axis=0)

    n_pow = -(-(t_len + 1) // SUBLANES) * SUBLANES
    ex = lax.broadcasted_iota(jnp.int32, (n_pow, LANES), 0).astype(F32)
    pmag = jnp.exp(ex * a)
    pow_re = pmag * jnp.cos(ex * th)
    pow_im = pmag * jnp.sin(ex * th)
    is_fwd = lax.broadcasted_iota(jnp.int32, (grp, LANES), 1) < S5_P

    def expand(tab, exp_fwd, exp_bwd):
        blocks = []
        for step in range(t_len):
            f = jnp.broadcast_to(tab[exp_fwd(step):exp_fwd(step) + 1, :], (grp, LANES))
            b = jnp.broadcast_to(tab[exp_bwd(step):exp_bwd(step) + 1, :], (grp, LANES))
            blocks.append(jnp.where(is_fwd, f, b))
        return jnp.concatenate(blocks, axis=0)

    pr = expand(pow_re, lambda s: t_len - 1 - s, lambda s: s)
    pi = expand(pow_im, lambda s: t_len - 1 - s, lambda s: s)
    xr = bbr * pr - bbi * pi
    xi = bbr * pi + bbi * pr
    wst_ref[...] = jnp.concatenate([xr, xi], axis=1).astype(wst_ref.dtype)

    fwd_rows = lax.broadcasted_iota(jnp.int32, (rows, LANES), 1) < S5_P
    c_re_rep = jnp.concatenate([c_re] * (LANES // grp), axis=0)
    c_im_rep = jnp.concatenate([c_im] * (LANES // grp), axis=0)
    zero = jnp.zeros_like(xr)

    def lag_kernel(keep):
        return (_dot_f32(jnp.where(keep, xr, zero), c_re_rep, nt=True)
                - _dot_f32(jnp.where(keep, xi, zero), c_im_rep, nt=True))

    ker_f = lag_kernel(fwd_rows)
    ker_b = lag_kernel(jnp.logical_not(fwd_rows))
    blk = lax.broadcasted_iota(jnp.int32, (rows, LANES), 1) // grp

    def column_block(t):
        up = (t_len - 1 - t) * grp
        down = t * grp
        f = ker_f[up:] if up == 0 else jnp.concatenate([ker_f[up:], jnp.zeros((up, LANES), F32)], axis=0)
        b = ker_b if down == 0 else jnp.concatenate([jnp.zeros((down, LANES), F32), ker_b[:rows - down]], axis=0)
        return f + b

    per_tile = LANES // grp
    for h in range(t_len // per_tile):
        tile = column_block(h * per_tile)
        for j in range(1, per_tile):
            tile = jnp.where(blk == j, column_block(h * per_tile + j), tile)
        toep_ref[:, h * LANES:(h + 1) * LANES] = tile.astype(toep_ref.dtype)

    pr = expand(pow_re, lambda t: t + 1, lambda t: t_len - t)
    pi = expand(pow_im, lambda t: t + 1, lambda t: t_len - t)
    cr = c_re_t * pr - c_im_t * pi
    ci = -(c_re_t * pi + c_im_t * pr)
    cst_ref[...] = jnp.concatenate([jnp.where(fwd_rows, cr, zero), jnp.where(fwd_rows, ci, zero),
                                    jnp.where(fwd_rows, zero, cr), jnp.where(fwd_rows, zero, ci)],
                                   axis=1).astype(cst_ref.dtype)
    apow_ref[...] = jnp.concatenate([pow_re[t_len:t_len + 1], pow_im[t_len:t_len + 1],
                                     jnp.zeros((SUBLANES - 2, LANES), F32)], axis=0)


def _s5_prep(lr, li, ldt, bre_t, bim_t, cre, cim):
    g = lr.shape[0]
    rows = S5_CHUNK * S5_GROUP
    grp = lambda *s: pl.BlockSpec((None,) + s, lambda i: (i,) + (0,) * len(s))
    return pl.pallas_call(
        _s5_prep_kernel,
        grid=(g,),
        in_specs=[grp(1, LANES)] * 3 + [grp(S5_GROUP, LANES)] * 4,
        out_specs=[grp(rows, rows), grp(rows, 4 * S5_P), grp(rows, 8 * S5_P), grp(SUBLANES, LANES)],
        out_shape=[jax.ShapeDtypeStruct((g, rows, rows), BF16),
                   jax.ShapeDtypeStruct((g, rows, 4 * S5_P), BF16),
                   jax.ShapeDtypeStruct((g, rows, 8 * S5_P), BF16),
                   jax.ShapeDtypeStruct((g, SUBLANES, LANES), F32)],
        compiler_params=_cparams("parallel"),
        name="s5_prep",
    )(lr, li, ldt, bre_t, bim_t, cre, cim)


def _block_transpose(xs, blk):
    n = len(xs)
    d = n // 2
    while d >= 1:
        low = (blk & d) == 0
        new = list(xs)
        for i in range(n):
            if i & d == 0:
                a, b = xs[i], xs[i | d]
                new[i] = jnp.where(low, a, pltpu.roll(b, d * S5_GROUP, 1))
                new[i | d] = jnp.where(low, pltpu.roll(a, LANES - d * S5_GROUP, 1), b)
        xs = new
        d //= 2
    return xs


def _s5_kernel(n_chunks, bp, splits, cu_ref, toep_ref, wst_ref, cst_ref, apow_ref, h0_ref, y_ref, hfin_ref,
               u_scr, s_scr, h_scr, yg_scr):
    t_len = S5_CHUNK
    ng = S5_TILE_GROUPS
    seq = n_chunks * t_len
    rows = bp * n_chunks
    rb = max(bp, math.gcd(S5_ROW_BLOCK, rows))
    cpb = rb // bp
    blk = lax.broadcasted_iota(jnp.int32, (rb, LANES), 1) // S5_GROUP

    def relayout_in(r, carry):
        r0 = pl.multiple_of(r * rb, rb)
        for h in range(t_len // ng):
            xs = []
            for b in range(ng):
                pieces = [cu_ref[pl.ds((r * cpb + j) * t_len + ng * h + b, bp, stride=seq), :] for j in range(cpb)]
                xs.append(pieces[0] if cpb == 1 else jnp.concatenate(pieces, axis=0))
            for g, tile in enumerate(_block_transpose(xs, blk)):
                u_scr[g, pl.ds(r0, rb), h * LANES:(h + 1) * LANES] = tile.astype(BF16)
        return carry

    lax.fori_loop(0, rows // rb, relayout_in, 0)

    half = 2 * S5_P
    for g in range(ng):
        s_scr[g] = _dot(u_scr[g], wst_ref[g])

    is_fwd = lax.broadcasted_iota(jnp.int32, (bp, half), 1) < S5_P

    def scan(init, store):
        def step(kk, hs):
            f0 = pl.multiple_of(kk * bp, bp)
            b0 = pl.multiple_of((n_chunks - 1 - kk) * bp, bp)
            new = []
            for g in range(ng):
                re, im = hs[g]
                if store:
                    h_scr[g, pl.ds(f0, bp), 0:half] = re
                    h_scr[g, pl.ds(f0, bp), half:2 * half] = im
                    h_scr[g, pl.ds(b0, bp), 2 * half:3 * half] = re
                    h_scr[g, pl.ds(b0, bp), 3 * half:4 * half] = im
                s_f = s_scr[g, pl.ds(f0, bp), :]
                s_b = s_scr[g, pl.ds(b0, bp), :]
                a_re = apow_ref[g, 0:1, :]
                a_im = apow_ref[g, 1:2, :]
                new.append((a_re * re - a_im * im + jnp.where(is_fwd, s_f[:, :half], s_b[:, :half]),
                            a_re * im + a_im * re + jnp.where(is_fwd, s_f[:, half:], s_b[:, half:])))
            return tuple(new)

        return lax.fori_loop(0, n_chunks, step, init)

    h0 = tuple((h0_ref[g, :, :half], h0_ref[g, :, half:]) for g in range(ng))
    init = h0
    if splits > 1:
        seg = lax.broadcasted_iota(jnp.int32, (bp, half), 0) & (splits - 1)
        take_prev = is_fwd & (seg >= 1)
        take_next = jnp.logical_not(is_fwd) & (seg <= splits - 2)
        for _ in range(splits - 1):
            fin = scan(init, store=False)
            init = tuple(tuple(part0 + jnp.where(take_prev, pltpu.roll(part, 1, 0),
                                                 jnp.where(take_next, pltpu.roll(part, bp - 1, 0), 0.0))
                               for part0, part in zip(h0[g], fin[g])) for g in range(ng))
    hs = scan(init, store=True)
    for g in range(ng):
        hfin_ref[g, :, :half] = hs[g][0]
        hfin_ref[g, :, half:] = hs[g][1]
        yg_scr[g] = _dot(u_scr[g], toep_ref[g]) + _dot_nt(h_scr[g].astype(BF16), cst_ref[g])

    def relayout_out(r, carry):
        r0 = pl.multiple_of(r * rb, rb)
        for h in range(t_len // ng):
            ys = [yg_scr[g, pl.ds(r0, rb), h * LANES:(h + 1) * LANES] for g in range(ng)]
            for b, tile in enumerate(_block_transpose(ys, blk)):
                for j in range(cpb):
                    y_ref[pl.ds((r * cpb + j) * t_len + ng * h + b, bp, stride=seq), :] = tile[j * bp:(j + 1) * bp]
        return carry

    lax.fori_loop(0, rows // rb, relayout_out, 0)


def _s5(cu, toep, wst, cst, apow, h0, bsz):
    n_rows, width = cu.shape
    splits = SUBLANES // bsz if SUBLANES % bsz == 0 else 1
    bp = bsz * splits
    n_chunks = n_rows // bp // S5_CHUNK
    rows = bp * n_chunks
    ng = S5_TILE_GROUPS
    w = S5_CHUNK * S5_GROUP
    p2 = 2 * S5_P
    if h0 is None:
        h0p = jnp.zeros((S5_G, bp, 2 * p2), F32)
    elif splits == 1:
        h0p = h0
    else:
        h04 = h0.reshape(S5_G, bsz, 1, 2, 2, S5_P)
        zero = jnp.zeros((S5_G, bsz, splits - 1, 2, 1, S5_P), F32)
        h0p = jnp.concatenate([jnp.concatenate([h04[..., 0:1, :], zero], axis=2),
                               jnp.concatenate([zero, h04[..., 1:2, :]], axis=2)], axis=4)
        h0p = h0p.reshape(S5_G, bp, 2 * p2)
    tile = lambda *s: pl.BlockSpec((ng,) + s, lambda j: (j,) + (0,) * len(s))
    lanes = pl.BlockSpec((n_rows, LANES), lambda j: (0, j))
    y, hfin = pl.pallas_call(
        functools.partial(_s5_kernel, n_chunks, bp, splits),
        grid=(width // LANES,),
        in_specs=[lanes, tile(w, w), tile(w, 2 * p2), tile(w, 4 * p2), tile(SUBLANES, p2), tile(bp, 2 * p2)],
        out_specs=[lanes, tile(bp, 2 * p2)],
        out_shape=[jax.ShapeDtypeStruct((n_rows, width), F32), jax.ShapeDtypeStruct((S5_G, bp, 2 * p2), F32)],
        scratch_shapes=[pltpu.VMEM((ng, rows, w), BF16), pltpu.VMEM((ng, rows, 2 * p2), F32),
                        pltpu.VMEM((ng, rows, 4 * p2), F32), pltpu.VMEM((ng, rows, w), F32)],
        compiler_params=_cparams("parallel"),
        name="s5_scan",
    )(cu, toep, wst, cst, apow, h0p)
    hf = hfin.reshape(S5_G, bsz, splits, 2, 2, S5_P)
    hfin = jnp.stack([hf[:, :, splits - 1, :, 0], hf[:, :, 0, :, 1]], axis=3)
    return y, hfin.reshape(S5_G, bsz, 2 * p2)


def _merge_kernel(x_ref, mod_ref, oa_ref, of_ref, ob_ref, gg_ref, y_ref, cu_ref, gate_ref,
                  ong_ref, s5d_ref, wglu_ref, bglu_ref, wbr_ref, wout_ref, o_ref):
    g1 = mod_ref[...][:, 2 * D_MODEL:3 * D_MODEL]
    o_gla = of_ref[...] + ob_ref[...]
    parts = []
    for h in range(B_HEADS):
        sl = slice(h * B_VDIM, (h + 1) * B_VDIM)
        parts.append((_rms(o_gla[:, sl]) * ong_ref[...] * gg_ref[:, sl].astype(F32)).astype(BF16))
    ob = jnp.concatenate(parts, axis=1)

    cu = cu_ref[...].astype(F32)
    z = y_ref[...] + s5d_ref[...] * cu
    yc = 0.5 * z * (1.0 + jnp.tanh(math.sqrt(2.0 / math.pi) * (z + 0.044715 * (z * z * z))))
    glu = _dot(yc.astype(BF16), wglu_ref[...]) + bglu_ref[...]
    oc = (glu[:, :MIX_W] * jax.nn.sigmoid(glu[:, MIX_W:])).astype(BF16)

    merged = None
    for r, br in enumerate((oa_ref[...], ob, oc)):
        term = gate_ref[:, r * D_MODEL:(r + 1) * D_MODEL].astype(F32) * _dot(br, wbr_ref[r])
        merged = term if merged is None else merged + term
    o_ref[...] = x_ref[...] + g1 * _dot(merged.astype(BF16), wout_ref[...])


def _merge(l, x2, mod3, oa, of, ob, gg, y, cu, gates, ong, s5d, wglu, bglu, wbr, wout):
    n_tok = x2.shape[0]
    tm, mod_spec = _token_tiling(n_tok, mod3, DENSE_TILE)
    tok = lambda w: pl.BlockSpec((tm, w), lambda i: (i, 0))
    return pl.pallas_call(
        _merge_kernel,
        grid=(n_tok // tm,),
        in_specs=[tok(D_MODEL), mod_spec,
                  tok(MIX_W), tok(MIX_W), tok(MIX_W), tok(MIX_W), tok(MIX_W), tok(MIX_W), tok(3 * D_MODEL),
                  _resident(ong.shape), _resident(s5d.shape), _layer_block(wglu, l), _resident(bglu.shape),
                  _layer_block(wbr, l), _layer_block(wout, l)],
        out_specs=tok(D_MODEL),
        out_shape=jax.ShapeDtypeStruct((n_tok, D_MODEL), F32),
        compiler_params=_cparams("parallel"),
        name="merge",
    )(x2, mod3, oa, of, ob, gg, y, cu, gates, ong, s5d, wglu, bglu, wbr, wout)


def _ffn_kernel(x_ref, mod_ref, n2_ref, wg_ref, wu_ref, wd_ref, o_ref):
    mod = mod_ref[...]
    sh2 = mod[:, 3 * D_MODEL:4 * D_MODEL]
    sc2 = mod[:, 4 * D_MODEL:5 * D_MODEL]
    g2 = mod[:, 5 * D_MODEL:6 * D_MODEL]
    x = x_ref[...]
    hb = (_rms(x) * n2_ref[...] * (1.0 + sc2) + sh2).astype(BF16)
    gate = _dot(hb, wg_ref[...])
    act = (gate * jax.nn.sigmoid(gate) * _dot(hb, wu_ref[...])).astype(BF16)
    o_ref[...] = x + g2 * _dot(act, wd_ref[...])


def _ffn(l, x2, mod3, n2g, wg, wu, wd):
    n_tok = x2.shape[0]
    tm, mod_spec = _token_tiling(n_tok, mod3, DENSE_TILE)
    tok = pl.BlockSpec((tm, D_MODEL), lambda i: (i, 0))
    return pl.pallas_call(
        _ffn_kernel,
        grid=(n_tok // tm,),
        in_specs=[tok, mod_spec,
                  _resident(n2g.shape), _layer_block(wg, l), _layer_block(wu, l), _layer_block(wd, l)],
        out_specs=tok,
        out_shape=jax.ShapeDtypeStruct((n_tok, D_MODEL), F32),
        compiler_params=_cparams("parallel"),
        name="swiglu",
    )(x2, mod3, n2g, wg, wu, wd)


def _rope_tables(n_tok):
    t = np.arange(n_tok)
    row = (t // GRID_W).astype(np.float32)
    col = (t % GRID_W).astype(np.float32)
    half = A_HEAD_DIM // 2
    inv = jnp.asarray(ROPE_THETA, F32) ** (-jnp.arange(0, half, 2, dtype=F32) / half)
    ang_r = jnp.asarray(row)[:, None] * inv
    ang_c = jnp.asarray(col)[:, None] * inv
    cos = jnp.concatenate([jnp.cos(ang_r)] * 2 + [jnp.cos(ang_c)] * 2, axis=-1)
    sin = jnp.concatenate([-jnp.sin(ang_r), jnp.sin(ang_r), -jnp.sin(ang_c), jnp.sin(ang_c)], axis=-1)
    reps = MIX_W // A_HEAD_DIM
    return jnp.tile(cos, (1, reps)), jnp.tile(sin, (1, reps))


def _prepare_shared(w):
    w_in = w['w_in'].astype(BF16)
    w_rank = w_in[:, :, HEAD_COLS:TAIL_START]
    depth = w_in.shape[0]
    return dict(
        w_in=w_in, w_tail=w_in[:, :, TAIL_START:],
        w_rank=jnp.concatenate([w_rank, w_rank, w_rank,
                                jnp.zeros((depth, D_MODEL, BR_PAD - 6 * GLA_RANK), BF16)], axis=2),
        wglu=w['s5_w_glu'].astype(BF16), wbr=w['w_branch'].astype(BF16), wout=w['w_out'].astype(BF16),
        wg=w['w_ffn_gate'].astype(BF16), wu=w['w_ffn_up'].astype(BF16), wd=w['w_ffn_down'].astype(BF16))


def _prepare_layer(l, w):
    kw = B_HEADS * B_KDIM
    zk = jnp.zeros((GLA_RANK, kw), F32)
    wa1 = jnp.concatenate([jnp.concatenate([w['gla_wa2'][l, 0], zk], axis=1),
                           jnp.concatenate([zk, w['gla_wa2'][l, 1]], axis=1)], axis=0)
    wa = jnp.concatenate([wa1, wa1, wa1, jnp.zeros((BR_PAD - 6 * GLA_RANK, 2 * kw), F32)], axis=0)
    ba = w['gla_ba'][l].reshape(1, 2 * kw)
    gidx = np.arange(MIX_W) // A_HEAD_DIM
    member = (gidx[:, None] == np.arange(LANES)[None, :]).astype(np.float32)
    gsum = jnp.asarray(member / A_HEAD_DIM, BF16)
    gspread = jnp.asarray(member.T, BF16)
    vec = lambda a: a.transpose(1, 0, 2).reshape(S5_G, 1, 2 * S5_P)
    mat = lambda a: a.transpose(1, 2, 0, 3).reshape(S5_G, S5_GROUP, 2 * S5_P)
    ldt = jnp.broadcast_to(w['s5_log_dt'][l][:, :, None], (2, S5_G, S5_P))
    toep, wst, cst, ap = _s5_prep(
        vec(w['s5_lam_re'][l]), vec(w['s5_lam_im'][l]), vec(ldt),
        mat(jnp.swapaxes(w['s5_b_re'][l], -1, -2)), mat(jnp.swapaxes(w['s5_b_im'][l], -1, -2)),
        mat(w['s5_c_re'][l]), mat(w['s5_c_im'][l]))
    return dict(
        wa=wa, ba=ba, gsum=gsum, gspread=gspread,
        n1g=w['norm1_g'][l].reshape(1, -1), n2g=w['norm2_g'][l].reshape(1, -1),
        qg=jnp.tile(w['diff_qn_g'][l], MIX_W // A_HEAD_DIM).reshape(1, -1),
        kg=jnp.tile(w['diff_kn_g'][l], MIX_W // A_HEAD_DIM).reshape(1, -1),
        lamp=w['diff_lam'][l], subg=jnp.tile(w['diff_subln_g'][l], A_HEADS).reshape(1, -1),
        ong=w['gla_on_g'][l].reshape(1, -1), s5d=w['s5_d'][l].reshape(1, -1),
        toep=toep, wst=wst, cst=cst, apow=ap, bglu=w['s5_b_glu'][l].reshape(1, -1),
    )


def _layer(l, x, mod3, p, sw, lam_init, ctx, rope_tabs):
    bsz, n_tok, _ = x.shape
    x2 = x.reshape(bsz * n_tok, D_MODEL)
    latent = ctx is not None
    q, k, v, gqk, gv, gg, la, cu, gates = _inproj(
        l, x2, mod3, p['n1g'], sw['w_in'], sw['w_tail'], sw['w_rank'], p['gsum'], p['gspread'], p['qg'], p['kg'],
        p['wa'], p['ba'], rope_tabs if latent else None, BF16 if latent else F32)
    sh = lambda a: a.reshape(bsz, n_tok, a.shape[-1])

    if latent:
        keys = [ctx['k'].reshape(bsz, -1, MIX_W), sh(k)]
        vals = [ctx['v'].reshape(bsz, -1, MIX_W), sh(v)]
    else:
        keys, vals = [sh(k)], [sh(v)]
    oa = _diff_attention(sh(q), keys, vals, p['lamp'], p['subg'], lam_init)

    of, ob, new_gla = _gla(sh(gqk), sh(gv), sh(la), ctx['gla'] if latent else None)

    h0 = ctx['s5'].transpose(3, 0, 2, 1, 4).reshape(S5_G, bsz, 4 * S5_P) if latent else None
    y, hfin = _s5(cu, p['toep'], p['wst'], p['cst'], p['apow'], h0, bsz)

    x1 = _merge(l, x2, mod3, oa.reshape(bsz * n_tok, MIX_W), of.reshape(bsz * n_tok, MIX_W),
                ob.reshape(bsz * n_tok, MIX_W), gg, y, cu, gates, p['ong'], p['s5d'], sw['wglu'], p['bglu'],
                sw['wbr'], sw['wout'])
    x_out = _ffn(l, x1, mod3, p['n2g'], sw['wg'], sw['wu'], sw['wd']).reshape(bsz, n_tok, D_MODEL)
    if latent:
        return x_out, None
    new_k = sh(k).reshape(bsz, n_tok, A_HEADS, 2, A_HEAD_DIM)
    new_v = sh(v).reshape(bsz, n_tok, A_HEADS, 2 * A_HEAD_DIM)
    hf = hfin.reshape(S5_G, bsz, 2, 2, S5_P)
    new_s5 = hf.transpose(1, 3, 2, 0, 4)
    return x_out, (new_k, new_v, new_gla, new_s5)


def kernel(x_prompt, x_sample, cache_diff_k, cache_diff_v, state_gla, state_s5, c, c_ctx, w_mod, b_mod, norm1_g, norm2_g, w_in, diff_qn_g, diff_kn_g, diff_lam, diff_subln_g, gla_wa2, gla_ba, gla_on_g, s5_lam_re, s5_lam_im, s5_log_dt, s5_b_re, s5_b_im, s5_c_re, s5_c_im, s5_d, s5_w_glu, s5_b_glu, w_branch, w_out, w_ffn_gate, w_ffn_up, w_ffn_down):
    weights = dict(norm1_g=norm1_g, norm2_g=norm2_g, w_in=w_in, diff_qn_g=diff_qn_g, diff_kn_g=diff_kn_g,
                   diff_lam=diff_lam, diff_subln_g=diff_subln_g, gla_wa2=gla_wa2, gla_ba=gla_ba,
                   gla_on_g=gla_on_g, s5_lam_re=s5_lam_re, s5_lam_im=s5_lam_im, s5_log_dt=s5_log_dt,
                   s5_b_re=s5_b_re, s5_b_im=s5_b_im, s5_c_re=s5_c_re, s5_c_im=s5_c_im, s5_d=s5_d,
                   s5_w_glu=s5_w_glu, s5_b_glu=s5_b_glu, w_branch=w_branch, w_out=w_out,
                   w_ffn_gate=w_ffn_gate, w_ffn_up=w_ffn_up, w_ffn_down=w_ffn_down)
    depth = w_mod.shape[0]
    dec_b = c.shape[0]
    cond8 = jnp.concatenate([c_ctx[None, :], c, jnp.zeros((SUBLANES - 1 - dec_b, D_MODEL), F32)], axis=0)
    mod = _modulation(cond8, w_mod, b_mod)
    rope_tabs = _rope_tables(x_sample.shape[1])
    sw = _prepare_shared(weights)
    y_prompt, y_sample = x_prompt, x_sample
    k_list, v_list, gla_list, s5_list = [], [], [], []
    for l in range(depth):
        p = _prepare_layer(l, weights)
        lam_init = 0.8 - 0.6 * math.exp(-0.3 * l)
        y_prompt, (k_l, v_l, g_l, s_l) = _layer(l, y_prompt, mod[l, 0:1][:, None, :], p, sw, lam_init, None, None)
        k_list.append(k_l)
        v_list.append(v_l)
        gla_list.append(g_l)
        s5_list.append(s_l)
        ctx = dict(k=cache_diff_k[:, l], v=cache_diff_v[:, l], gla=state_gla[:, l], s5=state_s5[:, l])
        y_sample, _ = _layer(l, y_sample, mod[l, 1:1 + dec_b][:, None, :], p, sw, lam_init, ctx, rope_tabs)
    return (y_prompt, y_sample, jnp.stack(k_list, axis=1), jnp.stack(v_list, axis=1),
            jnp.stack(gla_list, axis=1), jnp.stack(s5_list, axis=1))
```

```python
import functools
import math

import numpy as np
import jax
import jax.numpy as jnp
from jax import lax
from jax.experimental import pallas as pl
from jax.experimental.pallas import tpu as pltpu

F32 = jnp.float32
BF16 = jnp.bfloat16

D_MODEL = 1024
MIX_W = 512
A_HEADS = 4
A_HEAD_DIM = 64
GRID_W = 64
ROPE_THETA = 10000.0
B_HEADS = 4
B_KDIM = 64
B_VDIM = 128
GLA_RANK = 16
GLA_TAU = 16.0
S5_G = 32
S5_GROUP = 16
S5_P = 64
FFN_DIM = 2816
EPS = 1e-6

VMEM_LIMIT_BYTES = 56 * 1024 * 1024
INPROJ_TILE = 256
DENSE_TILE = 512
ATTN_Q_TILE = 1024
ATTN_SUB_TILE = 256
GLA_CHUNK = 128
GLA_STEP_CHUNKS = 4
S5_CHUNK = 16
SUBLANES = 8
LANES = 128
S5_TILE_GROUPS = LANES // 16
S5_ROW_BLOCK = 64

OFF_AQ, OFF_AK, OFF_AV, OFF_BQK, OFF_BV, OFF_BG, HEAD_COLS = 0, 512, 1024, 1536, 2048, 2560, 3072
TAIL_START = HEAD_COLS + 2 * GLA_RANK
TAIL_CU, TAIL_GZ, TAIL_COLS = 0, MIX_W, MIX_W + 3 * D_MODEL
BR_PAD = 128


def _cparams(*sem):
    return pltpu.CompilerParams(dimension_semantics=sem, vmem_limit_bytes=VMEM_LIMIT_BYTES)


def _resident(shape):
    nd = len(shape)
    return pl.BlockSpec(shape, lambda *_: (0,) * nd, pipeline_mode=pl.Buffered(1))


def _layer_block(arr, l, last=None):
    tail = arr.shape[1:] if last is None else arr.shape[1:-1] + (last,)
    return pl.BlockSpec((None,) + tail, lambda *_: (l,) + (0,) * len(tail), pipeline_mode=pl.Buffered(1))


def _split_bf16(x):
    hi = x.astype(BF16)
    lo = (x - hi.astype(F32)).astype(BF16)
    return hi, lo


def _dot(a, b):
    return jnp.dot(a, b, preferred_element_type=F32)


def _dot_nt(a, b):
    return lax.dot_general(a, b, (((1,), (1,)), ((), ())), preferred_element_type=F32)


def _dot_tn(a, b):
    return lax.dot_general(a, b, (((0,), (0,)), ((), ())), preferred_element_type=F32)


def _dot_f32(a, b, nt=False):
    d = _dot_nt if nt else _dot
    ah, al = _split_bf16(a)
    bh, bl = _split_bf16(b)
    return d(ah, bh) + d(ah, bl) + d(al, bh)


def _rms(x):
    return x * lax.rsqrt(jnp.mean(x * x, axis=-1, keepdims=True) + EPS)


def _mod_kernel(cond_ref, w_ref, b_ref, o_ref):
    cnd = cond_ref[...]
    s = (cnd * jax.nn.sigmoid(cnd))
    o_ref[...] = _dot_f32(s, w_ref[...]) + b_ref[...]


def _modulation(cond8, w_mod, b_mod):
    depth, d, n = w_mod.shape
    tn = 1536
    return pl.pallas_call(
        _mod_kernel,
        grid=(depth, n // tn),
        in_specs=[pl.BlockSpec((SUBLANES, d), lambda l, j: (0, 0)),
                  pl.BlockSpec((None, d, tn), lambda l, j: (l, 0, j)),
                  pl.BlockSpec((None, 1, tn), lambda l, j: (l, 0, j))],
        out_specs=pl.BlockSpec((None, SUBLANES, tn), lambda l, j: (l, 0, j)),
        out_shape=jax.ShapeDtypeStruct((depth, SUBLANES, n), F32),
        compiler_params=_cparams("parallel", "parallel"),
        name="modulation",
    )(cond8, w_mod, b_mod.reshape(depth, 1, n))


def _group_rms64(z, gsum, gspread, gain):
    hi, lo = _split_bf16(_dot((z * z).astype(BF16), gsum))
    ms = _dot(hi, gspread) + _dot(lo, gspread)
    return z * lax.rsqrt(ms + EPS) * gain


def _rope(z, c, s):
    n = z.shape[-1]
    lane = lax.broadcasted_iota(jnp.int32, z.shape, 1)
    first = (lane & 31) < 16
    partner = jnp.where(first, pltpu.roll(z, n - 16, 1), pltpu.roll(z, 16, 1))
    return z * c + partner * s


def _inproj_kernel(rope, x_ref, mod_ref, n1_ref, w_head_ref, w_tail_ref, w_rank_ref, gsum_ref, gspread_ref,
                   qg_ref, kg_ref, wa_ref, ba_ref, *rest):
    if rope:
        cos_ref, sin_ref = rest[:2]
        rest = rest[2:]
    q_ref, k_ref, v_ref, gqk_ref, gv_ref, gg_ref, la_ref, cu_ref, gate_ref = rest
    mod = mod_ref[...]
    sh1 = mod[:, 0:D_MODEL]
    sc1 = mod[:, D_MODEL:2 * D_MODEL]
    h = _rms(x_ref[...]) * n1_ref[...] * (1.0 + sc1) + sh1
    hb = h.astype(BF16)
    gsum = gsum_ref[...]
    gspread = gspread_ref[...]

    def seg(a, b, w_ref=w_head_ref):
        return _dot(hb, w_ref[:, a:b])

    r = _dot(hb, w_rank_ref[...])
    r_hi, r_lo = _split_bf16(r)
    copy = lax.broadcasted_iota(jnp.int32, r.shape, 1) // (2 * GLA_RANK)
    wa_hi, wa_lo = _split_bf16(wa_ref[...])
    wcopy = lax.broadcasted_iota(jnp.int32, wa_hi.shape, 0) // (2 * GLA_RANK)
    pre = _dot(jnp.where(copy < 2, r_hi, r_lo), jnp.where(wcopy == 1, wa_lo, wa_hi)) + ba_ref[...]
    la_ref[...] = (jnp.minimum(pre, 0.0) - jnp.log1p(jnp.exp(-jnp.abs(pre)))) * (1.0 / GLA_TAU)

    q = _group_rms64(seg(OFF_AQ, OFF_AK), gsum, gspread, qg_ref[...])
    k = _group_rms64(seg(OFF_AK, OFF_AV), gsum, gspread, kg_ref[...])
    if rope:
        c = cos_ref[...]
        s = sin_ref[...]
        q = _rope(q, c, s)
        k = _rope(k, c, s)
    q_ref[...] = (q * (A_HEAD_DIM ** -0.5)).astype(q_ref.dtype)
    k_ref[...] = k.astype(k_ref.dtype)

    gate_ref[...] = jax.nn.sigmoid(seg(TAIL_GZ, TAIL_COLS, w_tail_ref)).astype(gate_ref.dtype)
    bg = seg(OFF_BG, HEAD_COLS)
    gg_ref[...] = (bg * jax.nn.sigmoid(bg)).astype(gg_ref.dtype)
    bqk = seg(OFF_BQK, OFF_BV)
    lane = lax.broadcasted_iota(jnp.int32, bqk.shape, 1)
    gqk_ref[...] = jnp.where(lane < B_HEADS * B_KDIM, bqk * (B_KDIM ** -0.5), bqk).astype(gqk_ref.dtype)
    v_ref[...] = seg(OFF_AV, OFF_BQK).astype(v_ref.dtype)
    gv_ref[...] = seg(OFF_BV, OFF_BG).astype(gv_ref.dtype)
    cu_ref[...] = seg(TAIL_CU, TAIL_GZ, w_tail_ref).astype(cu_ref.dtype)


def _token_tiling(n_tok, mod3, want):
    tm = math.gcd(want, n_tok // mod3.shape[0])
    tiles_per_mod = (n_tok // tm) // mod3.shape[0]
    return tm, pl.BlockSpec((None, 1, mod3.shape[-1]), lambda i: (i // tiles_per_mod, 0, 0))


def _inproj(l, x2, mod3, n1g, w_in, w_tail, w_rank, gsum, gspread, qg, kg, wa, ba, rope_tabs, kv_dtype):
    n_tok = x2.shape[0]
    tm, mod_spec = _token_tiling(n_tok, mod3, INPROJ_TILE)
    rope = rope_tabs is not None
    tok = lambda w: pl.BlockSpec((tm, w), lambda i: (i, 0))
    in_specs = [tok(D_MODEL), mod_spec, _resident(n1g.shape),
                _layer_block(w_in, l, HEAD_COLS), _layer_block(w_tail, l), _layer_block(w_rank, l),
                _resident(gsum.shape), _resident(gspread.shape),
                _resident(qg.shape), _resident(kg.shape), _resident(wa.shape), _resident(ba.shape)]
    args = [x2, mod3, n1g, w_in, w_tail, w_rank, gsum, gspread, qg, kg, wa, ba]
    if rope:
        tiles_per_seq = rope_tabs[0].shape[0] // tm
        for t in rope_tabs:
            in_specs.append(pl.BlockSpec((tm, MIX_W), lambda i: (i % tiles_per_seq, 0)))
            args.append(t)
    widths = [(MIX_W, BF16), (MIX_W, kv_dtype), (MIX_W, kv_dtype), (MIX_W, BF16), (MIX_W, BF16),
              (MIX_W, BF16), (MIX_W, F32), (MIX_W, F32), (3 * D_MODEL, BF16)]
    return pl.pallas_call(
        functools.partial(_inproj_kernel, rope),
        grid=(n_tok // tm,),
        in_specs=in_specs,
        out_specs=[tok(w) for w, _ in widths],
        out_shape=[jax.ShapeDtypeStruct((n_tok, w), dt) for w, dt in widths],
        compiler_params=_cparams("parallel"),
        name="inproj",
    )(*args)


def _attn_kernel(lam_init, n_seg, q_ref, *refs):
    k_refs = refs[:n_seg]
    v_refs = refs[n_seg:2 * n_seg]
    lamp_ref, subg_ref, o_ref = refs[2 * n_seg:]
    lv = lamp_ref[...]
    lam = (jnp.exp(jnp.sum(lv[0:1] * lv[1:2], axis=-1, keepdims=True))
           - jnp.exp(jnp.sum(lv[2:3] * lv[3:4], axis=-1, keepdims=True)) + lam_init)
    hd = 2 * A_HEAD_DIM
    n_batch, n_rows = q_ref.shape[0], q_ref.shape[1]
    tq = math.gcd(ATTN_SUB_TILE, n_rows)
    for bi, sub, h in [(a, b, c) for a in range(n_batch) for b in range(n_rows // tq) for c in range(A_HEADS)]:
        rows = slice(sub * tq, (sub + 1) * tq)
        sl = slice(h * hd, (h + 1) * hd)
        qh = q_ref[bi, rows, sl]
        first = lax.broadcasted_iota(jnp.int32, qh.shape, 1) < A_HEAD_DIM
        zero = jnp.zeros_like(qh)
        q2 = jnp.concatenate([jnp.where(first, qh, zero), jnp.where(first, zero, qh)], axis=0)
        scores = [_dot_nt(q2, k_ref[bi, :, sl].astype(BF16)) for k_ref in k_refs]
        m = functools.reduce(jnp.maximum, [jnp.max(s, axis=-1, keepdims=True) for s in scores])
        acc = None
        for s, v_ref in zip(scores, v_refs):
            e = jnp.exp((s - m).astype(BF16))
            v_ext = jnp.concatenate([v_ref[bi, :, sl].astype(BF16), jnp.ones((v_ref.shape[1], hd), BF16)], axis=1)
            pv = _dot(e, v_ext)
            acc = pv if acc is None else acc + pv
        o2 = acc[:, :hd] / acc[:, hd:]
        o = o2[:tq] - lam * o2[tq:]
        o_ref[bi, rows, sl] = (_rms(o) * subg_ref[:, sl] * (1.0 - lam_init)).astype(o_ref.dtype)


def _diff_attention(q, ks, vs, lamp, subg, lam_init):
    bsz, lq, w = q.shape
    tq = math.gcd(ATTN_Q_TILE, lq)
    bb = math.gcd(max(1, ATTN_Q_TILE // lq), bsz)
    kv_spec = lambda a: pl.BlockSpec((bb, a.shape[1], w), lambda b, i: (b, 0, 0))
    return pl.pallas_call(
        functools.partial(_attn_kernel, lam_init, len(ks)),
        grid=(bsz // bb, lq // tq),
        in_specs=[pl.BlockSpec((bb, tq, w), lambda b, i: (b, i, 0))]
                 + [kv_spec(a) for a in ks] + [kv_spec(a) for a in vs]
                 + [pl.BlockSpec(lamp.shape, lambda b, i: (0, 0)), pl.BlockSpec(subg.shape, lambda b, i: (0, 0))],
        out_specs=pl.BlockSpec((bb, tq, w), lambda b, i: (b, i, 0)),
        out_shape=jax.ShapeDtypeStruct((bsz, lq, w), BF16),
        compiler_params=_cparams("parallel", "parallel"),
        name="diff_attention",
    )(q, *ks, *vs, lamp, subg)


def _gla_masks(chunk):
    nlev = int(math.log2(chunk))
    assert 1 << nlev == chunk
    t = np.arange(chunk)[:, None]
    r = np.arange(chunk)[None, :]
    cum, pair = [], []
    for j in range(nlev + 1):
        start = (t >> j) << j
        end = start + (1 << j) - 1
        if j > 0:
            cum.append((r >= start) & (r <= t))
            cum.append((r > t) & (r <= end))
        if j < nlev:
            pair.append(((t >> (j + 1)) == (r >> (j + 1))) & (((t >> j) & 1) == 1) & (((r >> j) & 1) == 0))
    pair.append(t == r)
    dup = lambda m: np.concatenate([m, m], axis=1)
    cum_f = dup(np.concatenate(cum, 0).astype(np.float32))
    pair_f = np.stack(pair, 0).astype(np.float32)
    cum_b = dup(np.concatenate([m[::-1, ::-1] for m in cum], 0).astype(np.float32))
    pair_b = pair_f[:, ::-1, ::-1]
    return (jnp.asarray(np.stack([cum_f, cum_b]), BF16), jnp.asarray(np.stack([pair_f, pair_b]), F32), nlev)


def _gla_chunk(qk, v, la, cum, pair_ref, d, st_ref, bd, nlev, last_row):
    c = qk.shape[0]
    kw = B_HEADS * B_KDIM
    q = qk[:, :kw].astype(F32)
    k = qk[:, kw:].astype(F32)
    hi, lo = _split_bf16(la)
    e = jnp.exp(_dot(cum, jnp.concatenate([hi, lo], axis=0)))
    lane = lax.broadcasted_iota(jnp.int32, (c, kw), 1)
    head_masks = [(lane >= h * B_KDIM) & (lane < (h + 1) * B_KDIM) for h in range(B_HEADS)]
    zero = jnp.zeros((c, kw), BF16)

    def scores(qf, kf, pm):
        qb = qf.astype(BF16)
        kb = kf.astype(BF16)
        stacked = _dot_nt(jnp.concatenate([jnp.where(m, qb, zero) for m in head_masks], axis=0), kb)
        return [pm * stacked[h * c:(h + 1) * c] for h in range(B_HEADS)]

    def factors(j):
        return e[(2 * j - 2) * c:(2 * j - 1) * c], e[(2 * j - 1) * c:(2 * j) * c]

    att = scores(q, k, pair_ref[d, nlev])
    for j in range(nlev):
        if j == 0:
            lev = scores(q * jnp.exp(la), k, pair_ref[d, 0])
        else:
            eq, ek = factors(j)
            lev = scores(q * eq, k * ek, pair_ref[d, j])
        att = [a + b for a, b in zip(att, lev)]
    eq, ek = factors(nlev)
    st = st_ref[...]
    o = _dot_nt((q * eq).astype(BF16), st.astype(BF16))
    outs = []
    for h in range(B_HEADS):
        sl = slice(h * B_VDIM, (h + 1) * B_VDIM)
        outs.append(o[:, sl] + _dot(att[h].astype(BF16), v[:, sl]))
    dec = eq[last_row:last_row + 1, :]
    st_ref[...] = st * dec + bd * _dot_tn(v, (k * ek).astype(BF16))
    return outs


def _gla_kernel(has_s0, nlev, *refs):
    if has_s0:
        s0_ref, refs = refs[0], refs[1:]
    (qkf_ref, vf_ref, laf_ref, qkb_ref, vb_ref, lab_ref, cum_ref, pair_ref, bd_ref,
     of_ref, ob_ref, sfin_ref, st_f, st_b) = refs
    i = pl.program_id(1)
    c = GLA_CHUNK
    n_sub = qkf_ref.shape[0] // c

    @pl.when(i == 0)
    def _():
        for d, st in enumerate((st_f, st_b)):
            if has_s0:
                zero = jnp.zeros((B_KDIM, B_VDIM), F32)
                full = jnp.concatenate(
                    [jnp.concatenate([s0_ref[d, h] if hh == h else zero for hh in range(B_HEADS)], axis=1)
                     for h in range(B_HEADS)], axis=0)
                st[...] = full.T
            else:
                st[...] = jnp.zeros_like(st)

    bd = bd_ref[...]
    for sub in range(n_sub):
        rf = slice(sub * c, (sub + 1) * c)
        outs = _gla_chunk(qkf_ref[rf, :], vf_ref[rf, :], laf_ref[rf, :], cum_ref[0], pair_ref, 0, st_f, bd, nlev,
                          c - 1)
        for h, o in enumerate(outs):
            of_ref[rf, h * B_VDIM:(h + 1) * B_VDIM] = o
        rb = slice((n_sub - 1 - sub) * c, (n_sub - sub) * c)
        outs = _gla_chunk(qkb_ref[rb, :], vb_ref[rb, :], lab_ref[rb, :], cum_ref[1], pair_ref, 1, st_b, bd, nlev, 0)
        for h, o in enumerate(outs):
            ob_ref[rb, h * B_VDIM:(h + 1) * B_VDIM] = o

    @pl.when(i == pl.num_programs(1) - 1)
    def _():
        for d, st in enumerate((st_f, st_b)):
            full = st[...].T
            for h in range(B_HEADS):
                sfin_ref[d, h] = full[h * B_KDIM:(h + 1) * B_KDIM, h * B_VDIM:(h + 1) * B_VDIM]


def _gla(gqk, gv, la, s0):
    bsz, n_tok, _ = gqk.shape
    c = math.gcd(GLA_STEP_CHUNKS * GLA_CHUNK, n_tok)
    n = n_tok // c
    cum, pair, nlev = _gla_masks(GLA_CHUNK)
    kw = B_HEADS * B_KDIM
    vw = B_HEADS * B_VDIM
    rows = np.arange(vw)[:, None] // B_VDIM
    cols = np.arange(kw)[None, :] // B_KDIM
    bd = jnp.asarray((rows == cols).astype(np.float32))
    fwd = lambda w, off=0: pl.BlockSpec((None, c, w), lambda b, i: (b, i, off))
    bwd = lambda w, off=0: pl.BlockSpec((None, c, w), lambda b, i: (b, n - 1 - i, off))
    in_specs = [fwd(2 * kw), fwd(vw), fwd(kw, 0), bwd(2 * kw), bwd(vw), bwd(kw, 1),
                _resident(cum.shape), _resident(pair.shape), _resident(bd.shape)]
    args = [gqk, gv, la, gqk, gv, la, cum, pair, bd]
    state_spec = pl.BlockSpec((None, 2, B_HEADS, B_KDIM, B_VDIM), lambda b, i: (b, 0, 0, 0, 0))
    if s0 is not None:
        in_specs.insert(0, state_spec)
        args.insert(0, s0)
    return pl.pallas_call(
        functools.partial(_gla_kernel, s0 is not None, nlev),
        grid=(bsz, n),
        in_specs=in_specs,
        out_specs=[fwd(vw), bwd(vw), state_spec],
        out_shape=[jax.ShapeDtypeStruct((bsz, n_tok, vw), F32), jax.ShapeDtypeStruct((bsz, n_tok, vw), F32),
                   jax.ShapeDtypeStruct((bsz, 2, B_HEADS, B_KDIM, B_VDIM), F32)],
        scratch_shapes=[pltpu.VMEM((vw, kw), F32), pltpu.VMEM((vw, kw), F32)],
        compiler_params=_cparams("parallel", "arbitrary"),
        name="gla",
    )(*args)


def _s5_prep_kernel(lr_ref, li_ref, dt_ref, bre_ref, bim_ref, cre_ref, cim_ref,
                    toep_ref, wst_ref, cst_ref, apow_ref):
    t_len = S5_CHUNK
    grp = S5_GROUP
    rows = t_len * grp
    lr = lr_ref[...]
    li = li_ref[...]
    dt = jnp.exp(dt_ref[...])
    a = lr * dt
    th = li * dt
    mag = jnp.exp(a)
    ar = mag * jnp.cos(th)
    ai = mag * jnp.sin(th)
    den = lr * lr + li * li
    fr = ((ar - 1.0) * lr + ai * li) / den
    fi = (ai * lr - (ar - 1.0) * li) / den
    b_re = bre_ref[...]
    b_im = bim_ref[...]
    bbr = jnp.concatenate([fr * b_re - fi * b_im] * t_len, axis=0)
    bbi = jnp.concatenate([fr * b_im + fi * b_re] * t_len, axis=0)
    c_re = cre_ref[...]
    c_im = cim_ref[...]
    c_re_t = jnp.concatenate([c_re] * t_len, axis=0)
    c_im_t = jnp.concatenate([c_im] * t_len, axis=0)

    n_pow = -(-(t_len + 1) // SUBLANES) * SUBLANES
    ex = lax.broadcasted_iota(jnp.int32, (n_pow, LANES), 0).astype(F32)
    pmag = jnp.exp(ex * a)
    pow_re = pmag * jnp.cos(ex * th)
    pow_im = pmag * jnp.sin(ex * th)
    is_fwd = lax.broadcasted_iota(jnp.int32, (grp, LANES), 1) < S5_P

    def expand(tab, exp_fwd, exp_bwd):
        blocks = []
        for step in range(t_len):
            f = jnp.broadcast_to(tab[exp_fwd(step):exp_fwd(step) + 1, :], (grp, LANES))
            b = jnp.broadcast_to(tab[exp_bwd(step):exp_bwd(step) + 1, :], (grp, LANES))
            blocks.append(jnp.where(is_fwd, f, b))
        return jnp.concatenate(blocks, axis=0)

    pr = expand(pow_re, lambda s: t_len - 1 - s, lambda s: s)
    pi = expand(pow_im, lambda s: t_len - 1 - s, lambda s: s)
    xr = bbr * pr - bbi * pi
    xi = bbr * pi + bbi * pr
    wst_ref[...] = jnp.concatenate([xr, xi], axis=1).astype(wst_ref.dtype)

    fwd_rows = lax.broadcasted_iota(jnp.int32, (rows, LANES), 1) < S5_P
    c_re_rep = jnp.concatenate([c_re] * (LANES // grp), axis=0)
    c_im_rep = jnp.concatenate([c_im] * (LANES // grp), axis=0)
    zero = jnp.zeros_like(xr)

    def lag_kernel(keep):
        return (_dot_f32(jnp.where(keep, xr, zero), c_re_rep, nt=True)
                - _dot_f32(jnp.where(keep, xi, zero), c_im_rep, nt=True))

    ker_f = lag_kernel(fwd_rows)
    ker_b = lag_kernel(jnp.logical_not(fwd_rows))
    blk = lax.broadcasted_iota(jnp.int32, (rows, LANES), 1) // grp

    def column_block(t):
        up = (t_len - 1 - t) * grp
        down = t * grp
        f = ker_f[up:] if up == 0 else jnp.concatenate([ker_f[up:], jnp.zeros((up, LANES), F32)], axis=0)
        b = ker_b if down == 0 else jnp.concatenate([jnp.zeros((down, LANES), F32), ker_b[:rows - down]], axis=0)
        return f + b

    per_tile = LANES // grp
    for h in range(t_len // per_tile):
        tile = column_block(h * per_tile)
        for j in range(1, per_tile):
            tile = jnp.where(blk == j, column_block(h * per_tile + j), tile)
        toep_ref[:, h * LANES:(h + 1) * LANES] = tile.astype(toep_ref.dtype)

    pr = expand(pow_re, lambda t: t + 1, lambda t: t_len - t)
    pi = expand(pow_im, lambda t: t + 1, lambda t: t_len - t)
    cr = c_re_t * pr - c_im_t * pi
    ci = -(c_re_t * pi + c_im_t * pr)
    cst_ref[...] = jnp.concatenate([jnp.where(fwd_rows, cr, zero), jnp.where(fwd_rows, ci, zero),
                                    jnp.where(fwd_rows, zero, cr), jnp.where(fwd_rows, zero, ci)],
                                   axis=1).astype(cst_ref.dtype)
    apow_ref[...] = jnp.concatenate([pow_re[t_len:t_len + 1], pow_im[t_len:t_len + 1],
                                     jnp.zeros((SUBLANES - 2, LANES), F32)], axis=0)


def _s5_prep(lr, li, ldt, bre_t, bim_t, cre, cim):
    g = lr.shape[0]
    rows = S5_CHUNK * S5_GROUP
    grp = lambda *s: pl.BlockSpec((None,) + s, lambda i: (i,) + (0,) * len(s))
    return pl.pallas_call(
        _s5_prep_kernel,
        grid=(g,),
        in_specs=[grp(1, LANES)] * 3 + [grp(S5_GROUP, LANES)] * 4,
        out_specs=[grp(rows, rows), grp(rows, 4 * S5_P), grp(rows, 8 * S5_P), grp(SUBLANES, LANES)],
        out_shape=[jax.ShapeDtypeStruct((g, rows, rows), BF16),
                   jax.ShapeDtypeStruct((g, rows, 4 * S5_P), BF16),
                   jax.ShapeDtypeStruct((g, rows, 8 * S5_P), BF16),
                   jax.ShapeDtypeStruct((g, SUBLANES, LANES), F32)],
        compiler_params=_cparams("parallel"),
        name="s5_prep",
    )(lr, li, ldt, bre_t, bim_t, cre, cim)


def _block_transpose(xs, blk):
    n = len(xs)
    d = n // 2
    while d >= 1:
        low = (blk & d) == 0
        new = list(xs)
        for i in range(n):
            if i & d == 0:
                a, b = xs[i], xs[i | d]
                new[i] = jnp.where(low, a, pltpu.roll(b, d * S5_GROUP, 1))
                new[i | d] = jnp.where(low, pltpu.roll(a, LANES - d * S5_GROUP, 1), b)
        xs = new
        d //= 2
    return xs


def _s5_kernel(n_chunks, bp, splits, cu_ref, toep_ref, wst_ref, cst_ref, apow_ref, h0_ref, y_ref, hfin_ref,
               u_scr, s_scr, h_scr, yg_scr):
    t_len = S5_CHUNK
    ng = S5_TILE_GROUPS
    seq = n_chunks * t_len
    rows = bp * n_chunks
    rb = max(bp, math.gcd(S5_ROW_BLOCK, rows))
    cpb = rb // bp
    blk = lax.broadcasted_iota(jnp.int32, (rb, LANES), 1) // S5_GROUP

    def relayout_in(r, carry):
        r0 = pl.multiple_of(r * rb, rb)
        for h in range(t_len // ng):
            xs = []
            for b in range(ng):
                pieces = [cu_ref[pl.ds((r * cpb + j) * t_len + ng * h + b, bp, stride=seq), :] for j in range(cpb)]
                xs.append(pieces[0] if cpb == 1 else jnp.concatenate(pieces, axis=0))
            for g, tile in enumerate(_block_transpose(xs, blk)):
                u_scr[g, pl.ds(r0, rb), h * LANES:(h + 1) * LANES] = tile.astype(BF16)
        return carry

    lax.fori_loop(0, rows // rb, relayout_in, 0)

    half = 2 * S5_P
    for g in range(ng):
        s_scr[g] = _dot(u_scr[g], wst_ref[g])

    is_fwd = lax.broadcasted_iota(jnp.int32, (bp, half), 1) < S5_P

    def scan(init, store):
        def step(kk, hs):
            f0 = pl.multiple_of(kk * bp, bp)
            b0 = pl.multiple_of((n_chunks - 1 - kk) * bp, bp)
            new = []
            for g in range(ng):
                re, im = hs[g]
                if store:
                    h_scr[g, pl.ds(f0, bp), 0:half] = re
                    h_scr[g, pl.ds(f0, bp), half:2 * half] = im
                    h_scr[g, pl.ds(b0, bp), 2 * half:3 * half] = re
                    h_scr[g, pl.ds(b0, bp), 3 * half:4 * half] = im
                s_f = s_scr[g, pl.ds(f0, bp), :]
                s_b = s_scr[g, pl.ds(b0, bp), :]
                a_re = apow_ref[g, 0:1, :]
                a_im = apow_ref[g, 1:2, :]
                new.append((a_re * re - a_im * im + jnp.where(is_fwd, s_f[:, :half], s_b[:, :half]),
                            a_re * im + a_im * re + jnp.where(is_fwd, s_f[:, half:], s_b[:, half:])))
            return tuple(new)

        return lax.fori_loop(0, n_chunks, step, init)

    h0 = tuple((h0_ref[g, :, :half], h0_ref[g, :, half:]) for g in range(ng))
    init = h0
    if splits > 1:
        seg = lax.broadcasted_iota(jnp.int32, (bp, half), 0) & (splits - 1)
        take_prev = is_fwd & (seg >= 1)
        take_next = jnp.logical_not(is_fwd) & (seg <= splits - 2)
        for _ in range(splits - 1):
            fin = scan(init, store=False)
            init = tuple(tuple(part0 + jnp.where(take_prev, pltpu.roll(part, 1, 0),
                                                 jnp.where(take_next, pltpu.roll(part, bp - 1, 0), 0.0))
                               for part0, part in zip(h0[g], fin[g])) for g in range(ng))
    hs = scan(init, store=True)
    for g in range(ng):
        hfin_ref[g, :, :half] = hs[g][0]
        hfin_ref[g, :, half:] = hs[g][1]
        yg_scr[g] = _dot(u_scr[g], toep_ref[g]) + _dot_nt(h_scr[g].astype(BF16), cst_ref[g])

    def relayout_out(r, carry):
        r0 = pl.multiple_of(r * rb, rb)
        for h in range(t_len // ng):
            ys = [yg_scr[g, pl.ds(r0, rb), h * LANES:(h + 1) * LANES] for g in range(ng)]
            for b, tile in enumerate(_block_transpose(ys, blk)):
                for j in range(cpb):
                    y_ref[pl.ds((r * cpb + j) * t_len + ng * h + b, bp, stride=seq), :] = tile[j * bp:(j + 1) * bp]
        return carry

    lax.fori_loop(0, rows // rb, relayout_out, 0)


def _s5(cu, toep, wst, cst, apow, h0, bsz):
    n_rows, width = cu.shape
    splits = SUBLANES // bsz if SUBLANES % bsz == 0 else 1
    bp = bsz * splits
    n_chunks = n_rows // bp // S5_CHUNK
    rows = bp * n_chunks
    ng = S5_TILE_GROUPS
    w = S5_CHUNK * S5_GROUP
    p2 = 2 * S5_P
    if h0 is None:
        h0p = jnp.zeros((S5_G, bp, 2 * p2), F32)
    elif splits == 1:
        h0p = h0
    else:
        h04 = h0.reshape(S5_G, bsz, 1, 2, 2, S5_P)
        zero = jnp.zeros((S5_G, bsz, splits - 1, 2, 1, S5_P), F32)
        h0p = jnp.concatenate([jnp.concatenate([h04[..., 0:1, :], zero], axis=2),
                               jnp.concatenate([zero, h04[..., 1:2, :]], axis=2)], axis=4)
        h0p = h0p.reshape(S5_G, bp, 2 * p2)
    tile = lambda *s: pl.BlockSpec((ng,) + s, lambda j: (j,) + (0,) * len(s))
    lanes = pl.BlockSpec((n_rows, LANES), lambda j: (0, j))
    y, hfin = pl.pallas_call(
        functools.partial(_s5_kernel, n_chunks, bp, splits),
        grid=(width // LANES,),
        in_specs=[lanes, tile(w, w), tile(w, 2 * p2), tile(w, 4 * p2), tile(SUBLANES, p2), tile(bp, 2 * p2)],
        out_specs=[lanes, tile(bp, 2 * p2)],
        out_shape=[jax.ShapeDtypeStruct((n_rows, width), F32), jax.ShapeDtypeStruct((S5_G, bp, 2 * p2), F32)],
        scratch_shapes=[pltpu.VMEM((ng, rows, w), BF16), pltpu.VMEM((ng, rows, 2 * p2), F32),
                        pltpu.VMEM((ng, rows, 4 * p2), F32), pltpu.VMEM((ng, rows, w), F32)],
        compiler_params=_cparams("parallel"),
        name="s5_scan",
    )(cu, toep, wst, cst, apow, h0p)
    hf = hfin.reshape(S5_G, bsz, splits, 2, 2, S5_P)
    hfin = jnp.stack([hf[:, :, splits - 1, :, 0], hf[:, :, 0, :, 1]], axis=3)
    return y, hfin.reshape(S5_G, bsz, 2 * p2)


def _merge_kernel(x_ref, mod_ref, oa_ref, of_ref, ob_ref, gg_ref, y_ref, cu_ref, gate_ref,
                  ong_ref, s5d_ref, wglu_ref, bglu_ref, wbr_ref, wout_ref, o_ref):
    g1 = mod_ref[...][:, 2 * D_MODEL:3 * D_MODEL]
    o_gla = of_ref[...] + ob_ref[...]
    parts = []
    for h in range(B_HEADS):
        sl = slice(h * B_VDIM, (h + 1) * B_VDIM)
        parts.append((_rms(o_gla[:, sl]) * ong_ref[...] * gg_ref[:, sl].astype(F32)).astype(BF16))
    ob = jnp.concatenate(parts, axis=1)

    cu = cu_ref[...].astype(F32)
    z = y_ref[...] + s5d_ref[...] * cu
    yc = 0.5 * z * (1.0 + jnp.tanh(math.sqrt(2.0 / math.pi) * (z + 0.044715 * (z * z * z))))
    glu = _dot(yc.astype(BF16), wglu_ref[...]) + bglu_ref[...]
    oc = (glu[:, :MIX_W] * jax.nn.sigmoid(glu[:, MIX_W:])).astype(BF16)

    merged = None
    for r, br in enumerate((oa_ref[...], ob, oc)):
        term = gate_ref[:, r * D_MODEL:(r + 1) * D_MODEL].astype(F32) * _dot(br, wbr_ref[r])
        merged = term if merged is None else merged + term
    o_ref[...] = x_ref[...] + g1 * _dot(merged.astype(BF16), wout_ref[...])


def _merge(l, x2, mod3, oa, of, ob, gg, y, cu, gates, ong, s5d, wglu, bglu, wbr, wout):
    n_tok = x2.shape[0]
    tm, mod_spec = _token_tiling(n_tok, mod3, DENSE_TILE)
    tok = lambda w: pl.BlockSpec((tm, w), lambda i: (i, 0))
    return pl.pallas_call(
        _merge_kernel,
        grid=(n_tok // tm,),
        in_specs=[tok(D_MODEL), mod_spec,
                  tok(MIX_W), tok(MIX_W), tok(MIX_W), tok(MIX_W), tok(MIX_W), tok(MIX_W), tok(3 * D_MODEL),
                  _resident(ong.shape), _resident(s5d.shape), _layer_block(wglu, l), _resident(bglu.shape),
                  _layer_block(wbr, l), _layer_block(wout, l)],
        out_specs=tok(D_MODEL),
        out_shape=jax.ShapeDtypeStruct((n_tok, D_MODEL), F32),
        compiler_params=_cparams("parallel"),
        name="merge",
    )(x2, mod3, oa, of, ob, gg, y, cu, gates, ong, s5d, wglu, bglu, wbr, wout)


def _ffn_kernel(x_ref, mod_ref, n2_ref, wg_ref, wu_ref, wd_ref, o_ref):
    mod = mod_ref[...]
    sh2 = mod[:, 3 * D_MODEL:4 * D_MODEL]
    sc2 = mod[:, 4 * D_MODEL:5 * D_MODEL]
    g2 = mod[:, 5 * D_MODEL:6 * D_MODEL]
    x = x_ref[...]
    hb = (_rms(x) * n2_ref[...] * (1.0 + sc2) + sh2).astype(BF16)
    gate = _dot(hb, wg_ref[...])
    act = (gate * jax.nn.sigmoid(gate) * _dot(hb, wu_ref[...])).astype(BF16)
    o_ref[...] = x + g2 * _dot(act, wd_ref[...])


def _ffn(l, x2, mod3, n2g, wg, wu, wd):
    n_tok = x2.shape[0]
    tm, mod_spec = _token_tiling(n_tok, mod3, DENSE_TILE)
    tok = pl.BlockSpec((tm, D_MODEL), lambda i: (i, 0))
    return pl.pallas_call(
        _ffn_kernel,
        grid=(n_tok // tm,),
        in_specs=[tok, mod_spec,
                  _resident(n2g.shape), _layer_block(wg, l), _layer_block(wu, l), _layer_block(wd, l)],
        out_specs=tok,
        out_shape=jax.ShapeDtypeStruct((n_tok, D_MODEL), F32),
        compiler_params=_cparams("parallel"),
        name="swiglu",
    )(x2, mod3, n2g, wg, wu, wd)


def _rope_tables(n_tok):
    t = np.arange(n_tok)
    row = (t // GRID_W).astype(np.float32)
    col = (t % GRID_W).astype(np.float32)
    half = A_HEAD_DIM // 2
    inv = jnp.asarray(ROPE_THETA, F32) ** (-jnp.arange(0, half, 2, dtype=F32) / half)
    ang_r = jnp.asarray(row)[:, None] * inv
    ang_c = jnp.asarray(col)[:, None] * inv
    cos = jnp.concatenate([jnp.cos(ang_r)] * 2 + [jnp.cos(ang_c)] * 2, axis=-1)
    sin = jnp.concatenate([-jnp.sin(ang_r), jnp.sin(ang_r), -jnp.sin(ang_c), jnp.sin(ang_c)], axis=-1)
    reps = MIX_W // A_HEAD_DIM
    return jnp.tile(cos, (1, reps)), jnp.tile(sin, (1, reps))


def _prepare_shared(w):
    w_in = w['w_in'].astype(BF16)
    w_rank = w_in[:, :, HEAD_COLS:TAIL_START]
    depth = w_in.shape[0]
    return dict(
        w_in=w_in, w_tail=w_in[:, :, TAIL_START:],
        w_rank=jnp.concatenate([w_rank, w_rank, w_rank,
                                jnp.zeros((depth, D_MODEL, BR_PAD - 6 * GLA_RANK), BF16)], axis=2),
        wglu=w['s5_w_glu'].astype(BF16), wbr=w['w_branch'].astype(BF16), wout=w['w_out'].astype(BF16),
        wg=w['w_ffn_gate'].astype(BF16), wu=w['w_ffn_up'].astype(BF16), wd=w['w_ffn_down'].astype(BF16))


def _prepare_layer(l, w):
    kw = B_HEADS * B_KDIM
    zk = jnp.zeros((GLA_RANK, kw), F32)
    wa1 = jnp.concatenate([jnp.concatenate([w['gla_wa2'][l, 0], zk], axis=1),
                           jnp.concatenate([zk, w['gla_wa2'][l, 1]], axis=1)], axis=0)
    wa = jnp.concatenate([wa1, wa1, wa1, jnp.zeros((BR_PAD - 6 * GLA_RANK, 2 * kw), F32)], axis=0)
    ba = w['gla_ba'][l].reshape(1, 2 * kw)
    gidx = np.arange(MIX_W) // A_HEAD_DIM
    member = (gidx[:, None] == np.arange(LANES)[None, :]).astype(np.float32)
    gsum = jnp.asarray(member / A_HEAD_DIM, BF16)
    gspread = jnp.asarray(member.T, BF16)
    vec = lambda a: a.transpose(1, 0, 2).reshape(S5_G, 1, 2 * S5_P)
    mat = lambda a: a.transpose(1, 2, 0, 3).reshape(S5_G, S5_GROUP, 2 * S5_P)
    ldt = jnp.broadcast_to(w['s5_log_dt'][l][:, :, None], (2, S5_G, S5_P))
    toep, wst, cst, ap = _s5_prep(
        vec(w['s5_lam_re'][l]), vec(w['s5_lam_im'][l]), vec(ldt),
        mat(jnp.swapaxes(w['s5_b_re'][l], -1, -2)), mat(jnp.swapaxes(w['s5_b_im'][l], -1, -2)),
        mat(w['s5_c_re'][l]), mat(w['s5_c_im'][l]))
    return dict(
        wa=wa, ba=ba, gsum=gsum, gspread=gspread,
        n1g=w['norm1_g'][l].reshape(1, -1), n2g=w['norm2_g'][l].reshape(1, -1),
        qg=jnp.tile(w['diff_qn_g'][l], MIX_W // A_HEAD_DIM).reshape(1, -1),
        kg=jnp.tile(w['diff_kn_g'][l], MIX_W // A_HEAD_DIM).reshape(1, -1),
        lamp=w['diff_lam'][l], subg=jnp.tile(w['diff_subln_g'][l], A_HEADS).reshape(1, -1),
        ong=w['gla_on_g'][l].reshape(1, -1), s5d=w['s5_d'][l].reshape(1, -1),
        toep=toep, wst=wst, cst=cst, apow=ap, bglu=w['s5_b_glu'][l].reshape(1, -1),
    )


def _layer(l, x, mod3, p, sw, lam_init, ctx, rope_tabs):
    bsz, n_tok, _ = x.shape
    x2 = x.reshape(bsz * n_tok, D_MODEL)
    latent = ctx is not None
    q, k, v, gqk, gv, gg, la, cu, gates = _inproj(
        l, x2, mod3, p['n1g'], sw['w_in'], sw['w_tail'], sw['w_rank'], p['gsum'], p['gspread'], p['qg'], p['kg'],
        p['wa'], p['ba'], rope_tabs if latent else None, BF16 if latent else F32)
    sh = lambda a: a.reshape(bsz, n_tok, a.shape[-1])

    if latent:
        keys = [ctx['k'].reshape(bsz, -1, MIX_W), sh(k)]
        vals = [ctx['v'].reshape(bsz, -1, MIX_W), sh(v)]
    else:
        keys, vals = [sh(k)], [sh(v)]
    oa = _diff_attention(sh(q), keys, vals, p['lamp'], p['subg'], lam_init)

    of, ob, new_gla = _gla(sh(gqk), sh(gv), sh(la), ctx['gla'] if latent else None)

    h0 = ctx['s5'].transpose(3, 0, 2, 1, 4).reshape(S5_G, bsz, 4 * S5_P) if latent else None
    y, hfin = _s5(cu, p['toep'], p['wst'], p['cst'], p['apow'], h0, bsz)

    x1 = _merge(l, x2, mod3, oa.reshape(bsz * n_tok, MIX_W), of.reshape(bsz * n_tok, MIX_W),
                ob.reshape(bsz * n_tok, MIX_W), gg, y, cu, gates, p['ong'], p['s5d'], sw['wglu'], p['bglu'],
                sw['wbr'], sw['wout'])
    x_out = _ffn(l, x1, mod3, p['n2g'], sw['wg'], sw['wu'], sw['wd']).reshape(bsz, n_tok, D_MODEL)
    if latent:
        return x_out, None
    new_k = sh(k).reshape(bsz, n_tok, A_HEADS, 2, A_HEAD_DIM)
    new_v = sh(v).reshape(bsz, n_tok, A_HEADS, 2 * A_HEAD_DIM)
    hf = hfin.reshape(S5_G, bsz, 2, 2, S5_P)
    new_s5 = hf.transpose(1, 3, 2, 0, 4)
    return x_out, (new_k, new_v, new_gla, new_s5)


def kernel(x_prompt, x_sample, cache_diff_k, cache_diff_v, state_gla, state_s5, c, c_ctx, w_mod, b_mod, norm1_g, norm2_g, w_in, diff_qn_g, diff_kn_g, diff_lam, diff_subln_g, gla_wa2, gla_ba, gla_on_g, s5_lam_re, s5_lam_im, s5_log_dt, s5_b_re, s5_b_im, s5_c_re, s5_c_im, s5_d, s5_w_glu, s5_b_glu, w_branch, w_out, w_ffn_gate, w_ffn_up, w_ffn_down):
    weights = dict(norm1_g=norm1_g, norm2_g=norm2_g, w_in=w_in, diff_qn_g=diff_qn_g, diff_kn_g=diff_kn_g,
                   diff_lam=diff_lam, diff_subln_g=diff_subln_g, gla_wa2=gla_wa2, gla_ba=gla_ba,
                   gla_on_g=gla_on_g, s5_lam_re=s5_lam_re, s5_lam_im=s5_lam_im, s5_log_dt=s5_log_dt,
                   s5_b_re=s5_b_re, s5_b_im=s5_b_im, s5_c_re=s5_c_re, s5_c_im=s5_c_im, s5_d=s5_d,
                   s5_w_glu=s5_w_glu, s5_b_glu=s5_b_glu, w_branch=w_branch, w_out=w_out,
                   w_ffn_gate=w_ffn_gate, w_ffn_up=w_ffn_up, w_ffn_down=w_ffn_down)
    depth = w_mod.shape[0]
    dec_b = c.shape[0]
    cond8 = jnp.concatenate([c_ctx[None, :], c, jnp.zeros((SUBLANES - 1 - dec_b, D_MODEL), F32)], axis=0)
    mod = _modulation(cond8, w_mod, b_mod)
    rope_tabs = _rope_tables(x_sample.shape[1])
    sw = _prepare_shared(weights)
    y_prompt, y_sample = x_prompt, x_sample
    k_list, v_list, gla_list, s5_list = [], [], [], []
    for l in range(depth):
        p = _prepare_layer(l, weights)
        lam_init = 0.8 - 0.6 * math.exp(-0.3 * l)
        y_prompt, (k_l, v_l, g_l, s_l) = _layer(l, y_prompt, mod[l, 0:1][:, None, :], p, sw, lam_init, None, None)
        k_list.append(k_l)
        v_list.append(v_l)
        gla_list.append(g_l)
        s5_list.append(s_l)
        ctx = dict(k=cache_diff_k[:, l], v=cache_diff_v[:, l], gla=state_gla[:, l], s5=state_s5[:, l])
        y_sample, _ = _layer(l, y_sample, mod[l, 1:1 + dec_b][:, None, :], p, sw, lam_init, ctx, rope_tabs)
    return (y_prompt, y_sample, jnp.stack(k_list, axis=1), jnp.stack(v_list, axis=1),
            jnp.stack(gla_list, axis=1), jnp.stack(s5_list, axis=1))
```

```python
import functools
import math

import numpy as np
import jax
import jax.numpy as jnp
from jax import lax
from jax.experimental import pallas as pl
from jax.experimental.pallas import tpu as pltpu

F32 = jnp.float32
BF16 = jnp.bfloat16

D_MODEL = 1024
MIX_W = 512
A_HEADS = 4
A_HEAD_DIM = 64
GRID_W = 64
ROPE_THETA = 10000.0
B_HEADS = 4
B_KDIM = 64
B_VDIM = 128
GLA_RANK = 16
GLA_TAU = 16.0
S5_G = 32
S5_GROUP = 16
S5_P = 64
FFN_DIM = 2816
EPS = 1e-6

VMEM_LIMIT_BYTES = 56 * 1024 * 1024
INPROJ_TILE = 512
INPROJ_SUB_TILE = 256
DENSE_TILE = 512
ATTN_Q_TILE = 1024
ATTN_SUB_TILE = 256
GLA_CHUNK = 128
GLA_STEP_CHUNKS = 4
GLA_FIRST_DIFF_LEVEL = 3
S5_CHUNK = 16
SUBLANES = 8
LANES = 128
S5_TILE_GROUPS = LANES // 16
S5_ROW_BLOCK = 64

OFF_AQ, OFF_AK, OFF_AV, OFF_BQK, OFF_BV, OFF_BG, HEAD_COLS = 0, 512, 1024, 1536, 2048, 2560, 3072
TAIL_START = HEAD_COLS + 2 * GLA_RANK
TAIL_CU, TAIL_GZ, TAIL_COLS = 0, MIX_W, MIX_W + 3 * D_MODEL
BR_PAD = 128


def _cparams(*sem):
    return pltpu.CompilerParams(dimension_semantics=sem, vmem_limit_bytes=VMEM_LIMIT_BYTES)


def _resident(shape):
    nd = len(shape)
    return pl.BlockSpec(shape, lambda *_: (0,) * nd, pipeline_mode=pl.Buffered(1))


def _layer_block(arr, l, last=None):
    tail = arr.shape[1:] if last is None else arr.shape[1:-1] + (last,)
    return pl.BlockSpec((None,) + tail, lambda *_: (l,) + (0,) * len(tail), pipeline_mode=pl.Buffered(1))


def _split_bf16(x):
    hi = x.astype(BF16)
    lo = (x - hi.astype(F32)).astype(BF16)
    return hi, lo


def _dot(a, b):
    return jnp.dot(a, b, preferred_element_type=F32)


def _dot_nt(a, b):
    return lax.dot_general(a, b, (((1,), (1,)), ((), ())), preferred_element_type=F32)


def _dot_tn(a, b):
    return lax.dot_general(a, b, (((0,), (0,)), ((), ())), preferred_element_type=F32)


def _dot_f32(a, b, nt=False):
    d = _dot_nt if nt else _dot
    ah, al = _split_bf16(a)
    bh, bl = _split_bf16(b)
    return d(ah, bh) + d(ah, bl) + d(al, bh)


def _rms(x):
    return x * lax.rsqrt(jnp.mean(x * x, axis=-1, keepdims=True) + EPS)


def _mod_kernel(cond_ref, w_ref, b_ref, o_ref):
    cnd = cond_ref[...]
    s = (cnd * jax.nn.sigmoid(cnd))
    o_ref[...] = _dot_f32(s, w_ref[...]) + b_ref[...]


def _modulation(cond8, w_mod, b_mod):
    depth, d, n = w_mod.shape
    tn = 1536
    return pl.pallas_call(
        _mod_kernel,
        grid=(depth, n // tn),
        in_specs=[pl.BlockSpec((SUBLANES, d), lambda l, j: (0, 0)),
                  pl.BlockSpec((None, d, tn), lambda l, j: (l, 0, j)),
                  pl.BlockSpec((None, 1, tn), lambda l, j: (l, 0, j))],
        out_specs=pl.BlockSpec((None, SUBLANES, tn), lambda l, j: (l, 0, j)),
        out_shape=jax.ShapeDtypeStruct((depth, SUBLANES, n), F32),
        compiler_params=_cparams("parallel", "parallel"),
        name="modulation",
    )(cond8, w_mod, b_mod.reshape(depth, 1, n))


def _group_rms64(z, gsum, gspread, gain):
    hi, lo = _split_bf16(_dot((z * z).astype(BF16), gsum))
    ms = _dot(hi, gspread) + _dot(lo, gspread)
    return z * lax.rsqrt(ms + EPS) * gain


def _rope(z, c, s):
    n = z.shape[-1]
    lane = lax.broadcasted_iota(jnp.int32, z.shape, 1)
    first = (lane & 31) < 16
    partner = jnp.where(first, pltpu.roll(z, n - 16, 1), pltpu.roll(z, 16, 1))
    return z * c + partner * s


def _inproj_kernel(rope, x_ref, mod_ref, n1_ref, w_head_ref, w_tail_ref, w_rank_ref, gsum_ref, gspread_ref,
                   qg_ref, kg_ref, wa_ref, ba_ref, *rest):
    if rope:
        cos_ref, sin_ref = rest[:2]
        rest = rest[2:]
    q_ref, k_ref, v_ref, gqk_ref, gv_ref, gg_ref, la_ref, cu_ref, gate_ref = rest
    mod = mod_ref[...]
    sh1 = mod[:, 0:D_MODEL]
    sc1 = mod[:, D_MODEL:2 * D_MODEL]
    gsum = gsum_ref[...]
    gspread = gspread_ref[...]
    wa_hi, wa_lo = _split_bf16(wa_ref[...])
    wcopy = lax.broadcasted_iota(jnp.int32, wa_hi.shape, 0) // (2 * GLA_RANK)
    wa3 = jnp.where(wcopy == 1, wa_lo, wa_hi)
    sub = math.gcd(INPROJ_SUB_TILE, x_ref.shape[0])

    for rows in [slice(j * sub, (j + 1) * sub) for j in range(x_ref.shape[0] // sub)]:
        h = _rms(x_ref[rows, :]) * n1_ref[...] * (1.0 + sc1) + sh1
        hb = h.astype(BF16)

        def seg(a, b, w_ref=w_head_ref):
            return _dot(hb, w_ref[:, a:b])

        r = _dot(hb, w_rank_ref[...])
        r_hi, r_lo = _split_bf16(r)
        copy = lax.broadcasted_iota(jnp.int32, r.shape, 1) // (2 * GLA_RANK)
        pre = _dot(jnp.where(copy < 2, r_hi, r_lo), wa3) + ba_ref[...]
        la_ref[rows, :] = (jnp.minimum(pre, 0.0) - jnp.log1p(jnp.exp(-jnp.abs(pre)))) * (1.0 / GLA_TAU)

        q = _group_rms64(seg(OFF_AQ, OFF_AK), gsum, gspread, qg_ref[...])
        k = _group_rms64(seg(OFF_AK, OFF_AV), gsum, gspread, kg_ref[...])
        if rope:
            c = cos_ref[rows, :]
            s = sin_ref[rows, :]
            q = _rope(q, c, s)
            k = _rope(k, c, s)
        q_ref[rows, :] = (q * (A_HEAD_DIM ** -0.5)).astype(q_ref.dtype)
        k_ref[rows, :] = k.astype(k_ref.dtype)

        gate_ref[rows, :] = jax.nn.sigmoid(seg(TAIL_GZ, TAIL_COLS, w_tail_ref)).astype(gate_ref.dtype)
        bg = seg(OFF_BG, HEAD_COLS)
        gg_ref[rows, :] = (bg * jax.nn.sigmoid(bg)).astype(gg_ref.dtype)
        bqk = seg(OFF_BQK, OFF_BV)
        lane = lax.broadcasted_iota(jnp.int32, bqk.shape, 1)
        gqk_ref[rows, :] = jnp.where(lane < B_HEADS * B_KDIM, bqk * (B_KDIM ** -0.5), bqk).astype(gqk_ref.dtype)
        v_ref[rows, :] = seg(OFF_AV, OFF_BQK).astype(v_ref.dtype)
        gv_ref[rows, :] = seg(OFF_BV, OFF_BG).astype(gv_ref.dtype)
        cu_ref[rows, :] = seg(TAIL_CU, TAIL_GZ, w_tail_ref).astype(cu_ref.dtype)


def _token_tiling(n_tok, mod3, want):
    tm = math.gcd(want, n_tok // mod3.shape[0])
    tiles_per_mod = (n_tok // tm) // mod3.shape[0]
    return tm, pl.BlockSpec((None, 1, mod3.shape[-1]), lambda i: (i // tiles_per_mod, 0, 0))


def _inproj(l, x2, mod3, n1g, w_in, w_tail, w_rank, gsum, gspread, qg, kg, wa, ba, rope_tabs, kv_dtype):
    n_tok = x2.shape[0]
    tm, mod_spec = _token_tiling(n_tok, mod3, INPROJ_TILE)
    rope = rope_tabs is not None
    tok = lambda w: pl.BlockSpec((tm, w), lambda i: (i, 0))
    in_specs = [tok(D_MODEL), mod_spec, _resident(n1g.shape),
                _layer_block(w_in, l, HEAD_COLS), _layer_block(w_tail, l), _layer_block(w_rank, l),
                _resident(gsum.shape), _resident(gspread.shape),
                _resident(qg.shape), _resident(kg.shape), _resident(wa.shape), _resident(ba.shape)]
    args = [x2, mod3, n1g, w_in, w_tail, w_rank, gsum, gspread, qg, kg, wa, ba]
    if rope:
        tiles_per_seq = rope_tabs[0].shape[0] // tm
        for t in rope_tabs:
            in_specs.append(pl.BlockSpec((tm, MIX_W), lambda i: (i % tiles_per_seq, 0)))
            args.append(t)
    widths = [(MIX_W, BF16), (MIX_W, kv_dtype), (MIX_W, kv_dtype), (MIX_W, BF16), (MIX_W, BF16),
              (MIX_W, BF16), (MIX_W, F32), (MIX_W, F32), (3 * D_MODEL, BF16)]
    return pl.pallas_call(
        functools.partial(_inproj_kernel, rope),
        grid=(n_tok // tm,),
        in_specs=in_specs,
        out_specs=[tok(w) for w, _ in widths],
        out_shape=[jax.ShapeDtypeStruct((n_tok, w), dt) for w, dt in widths],
        compiler_params=_cparams("parallel"),
        name="inproj",
    )(*args)


def _attn_kernel(lam_init, n_seg, q_ref, *refs):
    k_refs = refs[:n_seg]
    v_refs = refs[n_seg:2 * n_seg]
    lamp_ref, subg_ref, o_ref = refs[2 * n_seg:]
    lv = lamp_ref[...]
    lam = (jnp.exp(jnp.sum(lv[0:1] * lv[1:2], axis=-1, keepdims=True))
           - jnp.exp(jnp.sum(lv[2:3] * lv[3:4], axis=-1, keepdims=True)) + lam_init)
    hd = 2 * A_HEAD_DIM
    n_batch, n_rows = q_ref.shape[0], q_ref.shape[1]
    tq = math.gcd(ATTN_SUB_TILE, n_rows)
    for bi, sub, h in [(a, b, c) for a in range(n_batch) for b in range(n_rows // tq) for c in range(A_HEADS)]:
        rows = slice(sub * tq, (sub + 1) * tq)
        sl = slice(h * hd, (h + 1) * hd)
        qh = q_ref[bi, rows, sl]
        first = lax.broadcasted_iota(jnp.int32, qh.shape, 1) < A_HEAD_DIM
        zero = jnp.zeros_like(qh)
        q2 = jnp.concatenate([jnp.where(first, qh, zero), jnp.where(first, zero, qh)], axis=0)
        scores = [_dot_nt(q2, k_ref[bi, :, sl].astype(BF16)) for k_ref in k_refs]
        m = functools.reduce(jnp.maximum, [jnp.max(s, axis=-1, keepdims=True) for s in scores])
        acc = None
        for s, v_ref in zip(scores, v_refs):
            e = jnp.exp((s - m).astype(BF16))
            v_ext = jnp.concatenate([v_ref[bi, :, sl].astype(BF16), jnp.ones((v_ref.shape[1], hd), BF16)], axis=1)
            pv = _dot(e, v_ext)
            acc = pv if acc is None else acc + pv
        o2 = acc[:, :hd] / acc[:, hd:]
        o = o2[:tq] - lam * o2[tq:]
        o_ref[bi, rows, sl] = (_rms(o) * subg_ref[:, sl] * (1.0 - lam_init)).astype(o_ref.dtype)


def _diff_attention(q, ks, vs, lamp, subg, lam_init):
    bsz, lq, w = q.shape
    tq = math.gcd(ATTN_Q_TILE, lq)
    bb = math.gcd(max(1, ATTN_Q_TILE // lq), bsz)
    kv_spec = lambda a: pl.BlockSpec((bb, a.shape[1], w), lambda b, i: (b, 0, 0))
    return pl.pallas_call(
        functools.partial(_attn_kernel, lam_init, len(ks)),
        grid=(bsz // bb, lq // tq),
        in_specs=[pl.BlockSpec((bb, tq, w), lambda b, i: (b, i, 0))]
                 + [kv_spec(a) for a in ks] + [kv_spec(a) for a in vs]
                 + [pl.BlockSpec(lamp.shape, lambda b, i: (0, 0)), pl.BlockSpec(subg.shape, lambda b, i: (0, 0))],
        out_specs=pl.BlockSpec((bb, tq, w), lambda b, i: (b, i, 0)),
        out_shape=jax.ShapeDtypeStruct((bsz, lq, w), BF16),
        compiler_params=_cparams("parallel", "parallel"),
        name="diff_attention",
    )(q, *ks, *vs, lamp, subg)


def _gla_masks(chunk):
    nlev = int(math.log2(chunk))
    assert 1 << nlev == chunk
    t = np.arange(chunk)[:, None]
    r = np.arange(chunk)[None, :]
    cum, pair = [], []
    for j in range(nlev + 1):
        start = (t >> j) << j
        end = start + (1 << j) - 1
        if 0 < j < GLA_FIRST_DIFF_LEVEL:
            cum.append((r >= start) & (r <= t))
            cum.append((r > t) & (r <= end))
        if j < nlev:
            pair.append(((t >> (j + 1)) == (r >> (j + 1))) & (((t >> j) & 1) == 1) & (((r >> j) & 1) == 0))
    cum.append(r <= t)
    pair.append(t == r)
    dup = lambda m: np.concatenate([m, m], axis=1)
    cum_f = dup(np.concatenate(cum, 0).astype(np.float32))
    pair_f = np.stack(pair, 0).astype(np.float32)
    cum_b = dup(np.concatenate([m[::-1, ::-1] for m in cum], 0).astype(np.float32))
    pair_b = pair_f[:, ::-1, ::-1]
    return (jnp.asarray(np.stack([cum_f, cum_b]), BF16), jnp.asarray(np.stack([pair_f, pair_b]), F32), nlev)


def _gla_chunk(qk, v, la, cum, pair_ref, d, st_ref, run_ref, bd, nlev, last_row):
    c = qk.shape[0]
    kw = B_HEADS * B_KDIM
    q = qk[:, :kw].astype(F32)
    k = qk[:, kw:].astype(F32)
    hi, lo = _split_bf16(la)
    sums = _dot(cum, jnp.concatenate([hi, lo], axis=0))
    n_mxu = GLA_FIRST_DIFF_LEVEL - 1
    run = sums[2 * n_mxu * c:]
    run_ref[...] = run

    def run_at(size, row_of_block):
        pieces = []
        for kb in range(c // size):
            r = row_of_block(kb)
            pieces.append(jnp.broadcast_to(run_ref[r:r + 1, :], (size, kw)) if 0 <= r < c
                          else jnp.zeros((size, kw), F32))
        return pieces[0] if len(pieces) == 1 else jnp.concatenate(pieces, axis=0)

    def factors(j):
        if j < GLA_FIRST_DIFF_LEVEL:
            return jnp.exp(sums[(2 * j - 2) * c:(2 * j - 1) * c]), jnp.exp(sums[(2 * j - 1) * c:(2 * j) * c])
        size = 1 << j
        if d == 0:
            before = run_at(size, lambda kb: kb * size - 1)
            last = run_at(size, lambda kb: (kb + 1) * size - 1)
            return jnp.exp(run - before), jnp.exp(last - run)
        after = run_at(size, lambda kb: (kb + 1) * size)
        first = run_at(size, lambda kb: kb * size)
        return jnp.exp(run - after), jnp.exp(first - run)

    lane = lax.broadcasted_iota(jnp.int32, (c, kw), 1)
    head_masks = [(lane >= h * B_KDIM) & (lane < (h + 1) * B_KDIM) for h in range(B_HEADS)]
    zero = jnp.zeros((c, kw), BF16)

    def scores(qf, kf, pm):
        qb = qf.astype(BF16)
        kb = kf.astype(BF16)
        stacked = _dot_nt(jnp.concatenate([jnp.where(m, qb, zero) for m in head_masks], axis=0), kb)
        return [pm * stacked[h * c:(h + 1) * c] for h in range(B_HEADS)]

    att = scores(q, k, pair_ref[d, nlev])
    for j in range(nlev):
        if j == 0:
            lev = scores(q * jnp.exp(la), k, pair_ref[d, 0])
        else:
            eq, ek = factors(j)
            lev = scores(q * eq, k * ek, pair_ref[d, j])
        att = [a + b for a, b in zip(att, lev)]
    eq, ek = factors(nlev)
    st = st_ref[...]
    o = _dot_nt((q * eq).astype(BF16), st.astype(BF16))
    outs = []
    for h in range(B_HEADS):
        sl = slice(h * B_VDIM, (h + 1) * B_VDIM)
        outs.append(o[:, sl] + _dot(att[h].astype(BF16), v[:, sl]))
    dec = eq[last_row:last_row + 1, :]
    st_ref[...] = st * dec + bd * _dot_tn(v, (k * ek).astype(BF16))
    return outs


def _gla_kernel(has_s0, nlev, *refs):
    if has_s0:
        s0_ref, refs = refs[0], refs[1:]
    (qkf_ref, vf_ref, laf_ref, qkb_ref, vb_ref, lab_ref, cum_ref, pair_ref, bd_ref,
     of_ref, ob_ref, sfin_ref, st_f, st_b, run_f, run_b) = refs
    i = pl.program_id(1)
    c = GLA_CHUNK
    n_sub = qkf_ref.shape[0] // c

    @pl.when(i == 0)
    def _():
        for d, st in enumerate((st_f, st_b)):
            if has_s0:
                zero = jnp.zeros((B_KDIM, B_VDIM), F32)
                full = jnp.concatenate(
                    [jnp.concatenate([s0_ref[d, h] if hh == h else zero for hh in range(B_HEADS)], axis=1)
                     for h in range(B_HEADS)], axis=0)
                st[...] = full.T
            else:
                st[...] = jnp.zeros_like(st)

    bd = bd_ref[...]
    for sub in range(n_sub):
        rf = slice(sub * c, (sub + 1) * c)
        outs = _gla_chunk(qkf_ref[rf, :], vf_ref[rf, :], laf_ref[rf, :], cum_ref[0], pair_ref, 0, st_f, run_f, bd,
                          nlev, c - 1)
        for h, o in enumerate(outs):
            of_ref[rf, h * B_VDIM:(h + 1) * B_VDIM] = o
        rb = slice((n_sub - 1 - sub) * c, (n_sub - sub) * c)
        outs = _gla_chunk(qkb_ref[rb, :], vb_ref[rb, :], lab_ref[rb, :], cum_ref[1], pair_ref, 1, st_b, run_b, bd,
                          nlev, 0)
        for h, o in enumerate(outs):
            ob_ref[rb, h * B_VDIM:(h + 1) * B_VDIM] = o

    @pl.when(i == pl.num_programs(1) - 1)
    def _():
        for d, st in enumerate((st_f, st_b)):
            full = st[...].T
            for h in range(B_HEADS):
                sfin_ref[d, h] = full[h * B_KDIM:(h + 1) * B_KDIM, h * B_VDIM:(h + 1) * B_VDIM]


def _gla(gqk, gv, la, s0):
    bsz, n_tok, _ = gqk.shape
    c = math.gcd(GLA_STEP_CHUNKS * GLA_CHUNK, n_tok)
    n = n_tok // c
    cum, pair, nlev = _gla_masks(GLA_CHUNK)
    kw = B_HEADS * B_KDIM
    vw = B_HEADS * B_VDIM
    rows = np.arange(vw)[:, None] // B_VDIM
    cols = np.arange(kw)[None, :] // B_KDIM
    bd = jnp.asarray((rows == cols).astype(np.float32))
    fwd = lambda w, off=0: pl.BlockSpec((None, c, w), lambda b, i: (b, i, off))
    bwd = lambda w, off=0: pl.BlockSpec((None, c, w), lambda b, i: (b, n - 1 - i, off))
    in_specs = [fwd(2 * kw), fwd(vw), fwd(kw, 0), bwd(2 * kw), bwd(vw), bwd(kw, 1),
                _resident(cum.shape), _resident(pair.shape), _resident(bd.shape)]
    args = [gqk, gv, la, gqk, gv, la, cum, pair, bd]
    state_spec = pl.BlockSpec((None, 2, B_HEADS, B_KDIM, B_VDIM), lambda b, i: (b, 0, 0, 0, 0))
    if s0 is not None:
        in_specs.insert(0, state_spec)
        args.insert(0, s0)
    return pl.pallas_call(
        functools.partial(_gla_kernel, s0 is not None, nlev),
        grid=(bsz, n),
        in_specs=in_specs,
        out_specs=[fwd(vw), bwd(vw), state_spec],
        out_shape=[jax.ShapeDtypeStruct((bsz, n_tok, vw), F32), jax.ShapeDtypeStruct((bsz, n_tok, vw), F32),
                   jax.ShapeDtypeStruct((bsz, 2, B_HEADS, B_KDIM, B_VDIM), F32)],
        scratch_shapes=[pltpu.VMEM((vw, kw), F32), pltpu.VMEM((vw, kw), F32),
                        pltpu.VMEM((GLA_CHUNK, kw), F32), pltpu.VMEM((GLA_CHUNK, kw), F32)],
        compiler_params=_cparams("parallel", "arbitrary"),
        name="gla",
    )(*args)


def _s5_prep_kernel(lr_ref, li_ref, dt_ref, bre_ref, bim_ref, cre_ref, cim_ref,
                    toep_ref, wst_ref, cst_ref, apow_ref):
    t_len = S5_CHUNK
    grp = S5_GROUP
    rows = t_len * grp
    lr = lr_ref[...]
    li = li_ref[...]
    dt = jnp.exp(dt_ref[...])
    a = lr * dt
    th = li * dt
    mag = jnp.exp(a)
    ar = mag * jnp.cos(th)
    ai = mag * jnp.sin(th)
    den = lr * lr + li * li
    fr = ((ar - 1.0) * lr + ai * li) / den
    fi = (ai * lr - (ar - 1.0) * li) / den
    b_re = bre_ref[...]
    b_im = bim_ref[...]
    bbr = jnp.concatenate([fr * b_re - fi * b_im] * t_len, axis=0)
    bbi = jnp.concatenate([fr * b_im + fi * b_re] * t_len, axis=0)
    c_re = cre_ref[...]
    c_im = cim_ref[...]
    c_re_t = jnp.concatenate([c_re] * t_len, axis=0)
    c_im_t = jnp.concatenate([c_im] * t_len, axis=0)

    n_pow = -(-(t_len + 1) // SUBLANES) * SUBLANES
    ex = lax.broadcasted_iota(jnp.int32, (n_pow, LANES), 0).astype(F32)
    pmag = jnp.exp(ex * a)
    pow_re = pmag * jnp.cos(ex * th)
    pow_im = pmag * jnp.sin(ex * th)
    is_fwd = lax.broadcasted_iota(jnp.int32, (grp, LANES), 1) < S5_P

    def expand(tab, exp_fwd, exp_bwd):
        blocks = []
        for step in range(t_len):
            f = jnp.broadcast_to(tab[exp_fwd(step):exp_fwd(step) + 1, :], (grp, LANES))
            b = jnp.broadcast_to(tab[exp_bwd(step):exp_bwd(step) + 1, :], (grp, LANES))
            blocks.append(jnp.where(is_fwd, f, b))
        return jnp.concatenate(blocks, axis=0)

    pr = expand(pow_re, lambda s: t_len - 1 - s, lambda s: s)
    pi = expand(pow_im, lambda s: t_len - 1 - s, lambda s: s)
    xr = bbr * pr - bbi * pi
    xi = bbr * pi + bbi * pr
    wst_ref[...] = jnp.concatenate([xr, xi], axis=1).astype(wst_ref.dtype)

    fwd_rows = lax.broadcasted_iota(jnp.int32, (rows, LANES), 1) < S5_P
    c_re_rep = jnp.concatenate([c_re] * (LANES // grp), axis=0)
    c_im_rep = jnp.concatenate([c_im] * (LANES // grp), axis=0)
    zero = jnp.zeros_like(xr)

    def lag_kernel(keep):
        return (_dot_f32(jnp.where(keep, xr, zero), c_re_rep, nt=True)
                - _dot_f32(jnp.where(keep, xi, zero), c_im_rep, nt=True))

    ker_f = lag_kernel(fwd_rows)
    ker_b = lag_kernel(jnp.logical_not(fwd_rows))
    blk = lax.broadcasted_iota(jnp.int32, (rows, LANES), 1) // grp

    def column_block(t):
        up = (t_len - 1 - t) * grp
        down = t * grp
        f = ker_f[up:] if up == 0 else jnp.concatenate([ker_f[up:], jnp.zeros((up, LANES), F32)], axis=0)
        b = ker_b if down == 0 else jnp.concatenate([jnp.zeros((down, LANES), F32), ker_b[:rows - down]], axis=0)
        return f + b

    per_tile = LANES // grp
    for h in range(t_len // per_tile):
        tile = column_block(h * per_tile)
        for j in range(1, per_tile):
            tile = jnp.where(blk == j, column_block(h * per_tile + j), tile)
        toep_ref[:, h * LANES:(h + 1) * LANES] = tile.astype(toep_ref.dtype)

    pr = expand(pow_re, lambda t: t + 1, lambda t: t_len - t)
    pi = expand(pow_im, lambda t: t + 1, lambda t: t_len - t)
    cr = c_re_t * pr - c_im_t * pi
    ci = -(c_re_t * pi + c_im_t * pr)
    cst_ref[...] = jnp.concatenate([jnp.where(fwd_rows, cr, zero), jnp.where(fwd_rows, ci, zero),
                                    jnp.where(fwd_rows, zero, cr), jnp.where(fwd_rows, zero, ci)],
                                   axis=1).astype(cst_ref.dtype)
    apow_ref[...] = jnp.concatenate([pow_re[t_len:t_len + 1], pow_im[t_len:t_len + 1],
                                     jnp.zeros((SUBLANES - 2, LANES), F32)], axis=0)


def _s5_prep(lr, li, ldt, bre_t, bim_t, cre, cim):
    g = lr.shape[0]
    rows = S5_CHUNK * S5_GROUP
    grp = lambda *s: pl.BlockSpec((None,) + s, lambda i: (i,) + (0,) * len(s))
    return pl.pallas_call(
        _s5_prep_kernel,
        grid=(g,),
        in_specs=[grp(1, LANES)] * 3 + [grp(S5_GROUP, LANES)] * 4,
        out_specs=[grp(rows, rows), grp(rows, 4 * S5_P), grp(rows, 8 * S5_P), grp(SUBLANES, LANES)],
        out_shape=[jax.ShapeDtypeStruct((g, rows, rows), BF16),
                   jax.ShapeDtypeStruct((g, rows, 4 * S5_P), BF16),
                   jax.ShapeDtypeStruct((g, rows, 8 * S5_P), BF16),
                   jax.ShapeDtypeStruct((g, SUBLANES, LANES), F32)],
        compiler_params=_cparams("parallel"),
        name="s5_prep",
    )(lr, li, ldt, bre_t, bim_t, cre, cim)


def _block_transpose(xs, blk):
    n = len(xs)
    d = n // 2
    while d >= 1:
        low = (blk & d) == 0
        new = list(xs)
        for i in range(n):
            if i & d == 0:
                a, b = xs[i], xs[i | d]
                new[i] = jnp.where(low, a, pltpu.roll(b, d * S5_GROUP, 1))
                new[i | d] = jnp.where(low, pltpu.roll(a, LANES - d * S5_GROUP, 1), b)
        xs = new
        d //= 2
    return xs


def _s5_kernel(n_chunks, bp, splits, cu_ref, toep_ref, wst_ref, cst_ref, apow_ref, h0_ref, y_ref, hfin_ref,
               u_scr, s_scr, h_scr, yg_scr):
    t_len = S5_CHUNK
    ng = S5_TILE_GROUPS
    seq = n_chunks * t_len
    rows = bp * n_chunks
    rb = max(bp, math.gcd(S5_ROW_BLOCK, rows))
    cpb = rb // bp
    blk = lax.broadcasted_iota(jnp.int32, (rb, LANES), 1) // S5_GROUP

    def relayout_in(r, carry):
        r0 = pl.multiple_of(r * rb, rb)
        for h in range(t_len // ng):
            xs = []
            for b in range(ng):
                pieces = [cu_ref[pl.ds((r * cpb + j) * t_len + ng * h + b, bp, stride=seq), :] for j in range(cpb)]
                xs.append(pieces[0] if cpb == 1 else jnp.concatenate(pieces, axis=0))
            for g, tile in enumerate(_block_transpose(xs, blk)):
                u_scr[g, pl.ds(r0, rb), h * LANES:(h + 1) * LANES] = tile.astype(BF16)
        return carry

    lax.fori_loop(0, rows // rb, relayout_in, 0)

    half = 2 * S5_P
    for g in range(ng):
        s_scr[g] = _dot(u_scr[g], wst_ref[g])

    is_fwd = lax.broadcasted_iota(jnp.int32, (bp, half), 1) < S5_P

    def scan(init, store):
        def step(kk, hs):
            f0 = pl.multiple_of(kk * bp, bp)
            b0 = pl.multiple_of((n_chunks - 1 - kk) * bp, bp)
            new = []
            for g in range(ng):
                re, im = hs[g]
                if store:
                    h_scr[g, pl.ds(f0, bp), 0:half] = re
                    h_scr[g, pl.ds(f0, bp), half:2 * half] = im
                    h_scr[g, pl.ds(b0, bp), 2 * half:3 * half] = re
                    h_scr[g, pl.ds(b0, bp), 3 * half:4 * half] = im
                s_f = s_scr[g, pl.ds(f0, bp), :]
                s_b = s_scr[g, pl.ds(b0, bp), :]
                a_re = apow_ref[g, 0:1, :]
                a_im = apow_ref[g, 1:2, :]
                new.append((a_re * re - a_im * im + jnp.where(is_fwd, s_f[:, :half], s_b[:, :half]),
                            a_re * im + a_im * re + jnp.where(is_fwd, s_f[:, half:], s_b[:, half:])))
            return tuple(new)

        return lax.fori_loop(0, n_chunks, step, init)

    h0 = tuple((h0_ref[g, :, :half], h0_ref[g, :, half:]) for g in range(ng))
    init = h0
    if splits > 1:
        seg = lax.broadcasted_iota(jnp.int32, (bp, half), 0) & (splits - 1)
        take_prev = is_fwd & (seg >= 1)
        take_next = jnp.logical_not(is_fwd) & (seg <= splits - 2)
        for _ in range(splits - 1):
            fin = scan(init, store=False)
            init = tuple(tuple(part0 + jnp.where(take_prev, pltpu.roll(part, 1, 0),
                                                 jnp.where(take_next, pltpu.roll(part, bp - 1, 0), 0.0))
                               for part0, part in zip(h0[g], fin[g])) for g in range(ng))
    hs = scan(init, store=True)
    for g in range(ng):
        hfin_ref[g, :, :half] = hs[g][0]
        hfin_ref[g, :, half:] = hs[g][1]
        yg_scr[g] = _dot(u_scr[g], toep_ref[g]) + _dot_nt(h_scr[g].astype(BF16), cst_ref[g])

    def relayout_out(r, carry):
        r0 = pl.multiple_of(r * rb, rb)
        for h in range(t_len // ng):
            ys = [yg_scr[g, pl.ds(r0, rb), h * LANES:(h + 1) * LANES] for g in range(ng)]
            for b, tile in enumerate(_block_transpose(ys, blk)):
                for j in range(cpb):
                    y_ref[pl.ds((r * cpb + j) * t_len + ng * h + b, bp, stride=seq), :] = tile[j * bp:(j + 1) * bp]
        return carry

    lax.fori_loop(0, rows // rb, relayout_out, 0)


def _s5(cu, toep, wst, cst, apow, h0, bsz):
    n_rows, width = cu.shape
    splits = SUBLANES // bsz if SUBLANES % bsz == 0 else 1
    bp = bsz * splits
    n_chunks = n_rows // bp // S5_CHUNK
    rows = bp * n_chunks
    ng = S5_TILE_GROUPS
    w = S5_CHUNK * S5_GROUP
    p2 = 2 * S5_P
    if h0 is None:
        h0p = jnp.zeros((S5_G, bp, 2 * p2), F32)
    elif splits == 1:
        h0p = h0
    else:
        h04 = h0.reshape(S5_G, bsz, 1, 2, 2, S5_P)
        zero = jnp.zeros((S5_G, bsz, splits - 1, 2, 1, S5_P), F32)
        h0p = jnp.concatenate([jnp.concatenate([h04[..., 0:1, :], zero], axis=2),
                               jnp.concatenate([zero, h04[..., 1:2, :]], axis=2)], axis=4)
        h0p = h0p.reshape(S5_G, bp, 2 * p2)
    tile = lambda *s: pl.BlockSpec((ng,) + s, lambda j: (j,) + (0,) * len(s))
    lanes = pl.BlockSpec((n_rows, LANES), lambda j: (0, j))
    y, hfin = pl.pallas_call(
        functools.partial(_s5_kernel, n_chunks, bp, splits),
        grid=(width // LANES,),
        in_specs=[lanes, tile(w, w), tile(w, 2 * p2), tile(w, 4 * p2), tile(SUBLANES, p2), tile(bp, 2 * p2)],
        out_specs=[lanes, tile(bp, 2 * p2)],
        out_shape=[jax.ShapeDtypeStruct((n_rows, width), F32), jax.ShapeDtypeStruct((S5_G, bp, 2 * p2), F32)],
        scratch_shapes=[pltpu.VMEM((ng, rows, w), BF16), pltpu.VMEM((ng, rows, 2 * p2), F32),
                        pltpu.VMEM((ng, rows, 4 * p2), F32), pltpu.VMEM((ng, rows, w), F32)],
        compiler_params=_cparams("parallel"),
        name="s5_scan",
    )(cu, toep, wst, cst, apow, h0p)
    hf = hfin.reshape(S5_G, bsz, splits, 2, 2, S5_P)
    hfin = jnp.stack([hf[:, :, splits - 1, :, 0], hf[:, :, 0, :, 1]], axis=3)
    return y, hfin.reshape(S5_G, bsz, 2 * p2)


def _merge_kernel(x_ref, mod_ref, oa_ref, of_ref, ob_ref, gg_ref, y_ref, cu_ref, gate_ref,
                  ong_ref, s5d_ref, wglu_ref, bglu_ref, wbr_ref, wout_ref, o_ref):
    g1 = mod_ref[...][:, 2 * D_MODEL:3 * D_MODEL]
    o_gla = of_ref[...] + ob_ref[...]
    parts = []
    for h in range(B_HEADS):
        sl = slice(h * B_VDIM, (h + 1) * B_VDIM)
        parts.append((_rms(o_gla[:, sl]) * ong_ref[...] * gg_ref[:, sl].astype(F32)).astype(BF16))
    ob = jnp.concatenate(parts, axis=1)

    cu = cu_ref[...].astype(F32)
    z = y_ref[...] + s5d_ref[...] * cu
    yc = 0.5 * z * (1.0 + jnp.tanh(math.sqrt(2.0 / math.pi) * (z + 0.044715 * (z * z * z))))
    glu = _dot(yc.astype(BF16), wglu_ref[...]) + bglu_ref[...]
    oc = (glu[:, :MIX_W] * jax.nn.sigmoid(glu[:, MIX_W:])).astype(BF16)

    merged = None
    for r, br in enumerate((oa_ref[...], ob, oc)):
        term = gate_ref[:, r * D_MODEL:(r + 1) * D_MODEL].astype(F32) * _dot(br, wbr_ref[r])
        merged = term if merged is None else merged + term
    o_ref[...] = x_ref[...] + g1 * _dot(merged.astype(BF16), wout_ref[...])


def _merge(l, x2, mod3, oa, of, ob, gg, y, cu, gates, ong, s5d, wglu, bglu, wbr, wout):
    n_tok = x2.shape[0]
    tm, mod_spec = _token_tiling(n_tok, mod3, DENSE_TILE)
    tok = lambda w: pl.BlockSpec((tm, w), lambda i: (i, 0))
    return pl.pallas_call(
        _merge_kernel,
        grid=(n_tok // tm,),
        in_specs=[tok(D_MODEL), mod_spec,
                  tok(MIX_W), tok(MIX_W), tok(MIX_W), tok(MIX_W), tok(MIX_W), tok(MIX_W), tok(3 * D_MODEL),
                  _resident(ong.shape), _resident(s5d.shape), _layer_block(wglu, l), _resident(bglu.shape),
                  _layer_block(wbr, l), _layer_block(wout, l)],
        out_specs=tok(D_MODEL),
        out_shape=jax.ShapeDtypeStruct((n_tok, D_MODEL), F32),
        compiler_params=_cparams("parallel"),
        name="merge",
    )(x2, mod3, oa, of, ob, gg, y, cu, gates, ong, s5d, wglu, bglu, wbr, wout)


def _ffn_kernel(x_ref, mod_ref, n2_ref, wg_ref, wu_ref, wd_ref, o_ref):
    mod = mod_ref[...]
    sh2 = mod[:, 3 * D_MODEL:4 * D_MODEL]
    sc2 = mod[:, 4 * D_MODEL:5 * D_MODEL]
    g2 = mod[:, 5 * D_MODEL:6 * D_MODEL]
    x = x_ref[...]
    hb = (_rms(x) * n2_ref[...] * (1.0 + sc2) + sh2).astype(BF16)
    gate = _dot(hb, wg_ref[...])
    act = (gate * jax.nn.sigmoid(gate) * _dot(hb, wu_ref[...])).astype(BF16)
    o_ref[...] = x + g2 * _dot(act, wd_ref[...])


def _ffn(l, x2, mod3, n2g, wg, wu, wd):
    n_tok = x2.shape[0]
    tm, mod_spec = _token_tiling(n_tok, mod3, DENSE_TILE)
    tok = pl.BlockSpec((tm, D_MODEL), lambda i: (i, 0))
    return pl.pallas_call(
        _ffn_kernel,
        grid=(n_tok // tm,),
        in_specs=[tok, mod_spec,
                  _resident(n2g.shape), _layer_block(wg, l), _layer_block(wu, l), _layer_block(wd, l)],
        out_specs=tok,
        out_shape=jax.ShapeDtypeStruct((n_tok, D_MODEL), F32),
        compiler_params=_cparams("parallel"),
        name="swiglu",
    )(x2, mod3, n2g, wg, wu, wd)


def _rope_tables(n_tok):
    t = np.arange(n_tok)
    row = (t // GRID_W).astype(np.float32)
    col = (t % GRID_W).astype(np.float32)
    half = A_HEAD_DIM // 2
    inv = jnp.asarray(ROPE_THETA, F32) ** (-jnp.arange(0, half, 2, dtype=F32) / half)
    ang_r = jnp.asarray(row)[:, None] * inv
    ang_c = jnp.asarray(col)[:, None] * inv
    cos = jnp.concatenate([jnp.cos(ang_r)] * 2 + [jnp.cos(ang_c)] * 2, axis=-1)
    sin = jnp.concatenate([-jnp.sin(ang_r), jnp.sin(ang_r), -jnp.sin(ang_c), jnp.sin(ang_c)], axis=-1)
    reps = MIX_W // A_HEAD_DIM
    return jnp.tile(cos, (1, reps)), jnp.tile(sin, (1, reps))


def _prepare_shared(w):
    w_in = w['w_in'].astype(BF16)
    w_rank = w_in[:, :, HEAD_COLS:TAIL_START]
    depth = w_in.shape[0]
    return dict(
        w_in=w_in, w_tail=w_in[:, :, TAIL_START:],
        w_rank=jnp.concatenate([w_rank, w_rank, w_rank,
                                jnp.zeros((depth, D_MODEL, BR_PAD - 6 * GLA_RANK), BF16)], axis=2),
        wglu=w['s5_w_glu'].astype(BF16), wbr=w['w_branch'].astype(BF16), wout=w['w_out'].astype(BF16),
        wg=w['w_ffn_gate'].astype(BF16), wu=w['w_ffn_up'].astype(BF16), wd=w['w_ffn_down'].astype(BF16))


def _prepare_layer(l, w):
    kw = B_HEADS * B_KDIM
    zk = jnp.zeros((GLA_RANK, kw), F32)
    wa1 = jnp.concatenate([jnp.concatenate([w['gla_wa2'][l, 0], zk], axis=1),
                           jnp.concatenate([zk, w['gla_wa2'][l, 1]], axis=1)], axis=0)
    wa = jnp.concatenate([wa1, wa1, wa1, jnp.zeros((BR_PAD - 6 * GLA_RANK, 2 * kw), F32)], axis=0)
    ba = w['gla_ba'][l].reshape(1, 2 * kw)
    gidx = np.arange(MIX_W) // A_HEAD_DIM
    member = (gidx[:, None] == np.arange(LANES)[None, :]).astype(np.float32)
    gsum = jnp.asarray(member / A_HEAD_DIM, BF16)
    gspread = jnp.asarray(member.T, BF16)
    vec = lambda a: a.transpose(1, 0, 2).reshape(S5_G, 1, 2 * S5_P)
    mat = lambda a: a.transpose(1, 2, 0, 3).reshape(S5_G, S5_GROUP, 2 * S5_P)
    ldt = jnp.broadcast_to(w['s5_log_dt'][l][:, :, None], (2, S5_G, S5_P))
    toep, wst, cst, ap = _s5_prep(
        vec(w['s5_lam_re'][l]), vec(w['s5_lam_im'][l]), vec(ldt),
        mat(jnp.swapaxes(w['s5_b_re'][l], -1, -2)), mat(jnp.swapaxes(w['s5_b_im'][l], -1, -2)),
        mat(w['s5_c_re'][l]), mat(w['s5_c_im'][l]))
    return dict(
        wa=wa, ba=ba, gsum=gsum, gspread=gspread,
        n1g=w['norm1_g'][l].reshape(1, -1), n2g=w['norm2_g'][l].reshape(1, -1),
        qg=jnp.tile(w['diff_qn_g'][l], MIX_W // A_HEAD_DIM).reshape(1, -1),
        kg=jnp.tile(w['diff_kn_g'][l], MIX_W // A_HEAD_DIM).reshape(1, -1),
        lamp=w['diff_lam'][l], subg=jnp.tile(w['diff_subln_g'][l], A_HEADS).reshape(1, -1),
        ong=w['gla_on_g'][l].reshape(1, -1), s5d=w['s5_d'][l].reshape(1, -1),
        toep=toep, wst=wst, cst=cst, apow=ap, bglu=w['s5_b_glu'][l].reshape(1, -1),
    )


def _layer(l, x, mod3, p, sw, lam_init, ctx, rope_tabs):
    bsz, n_tok, _ = x.shape
    x2 = x.reshape(bsz * n_tok, D_MODEL)
    latent = ctx is not None
    q, k, v, gqk, gv, gg, la, cu, gates = _inproj(
        l, x2, mod3, p['n1g'], sw['w_in'], sw['w_tail'], sw['w_rank'], p['gsum'], p['gspread'], p['qg'], p['kg'],
        p['wa'], p['ba'], rope_tabs if latent else None, BF16 if latent else F32)
    sh = lambda a: a.reshape(bsz, n_tok, a.shape[-1])

    if latent:
        keys = [ctx['k'].reshape(bsz, -1, MIX_W), sh(k)]
        vals = [ctx['v'].reshape(bsz, -1, MIX_W), sh(v)]
    else:
        keys, vals = [sh(k)], [sh(v)]
    oa = _diff_attention(sh(q), keys, vals, p['lamp'], p['subg'], lam_init)

    of, ob, new_gla = _gla(sh(gqk), sh(gv), sh(la), ctx['gla'] if latent else None)

    h0 = ctx['s5'].transpose(3, 0, 2, 1, 4).reshape(S5_G, bsz, 4 * S5_P) if latent else None
    y, hfin = _s5(cu, p['toep'], p['wst'], p['cst'], p['apow'], h0, bsz)

    x1 = _merge(l, x2, mod3, oa.reshape(bsz * n_tok, MIX_W), of.reshape(bsz * n_tok, MIX_W),
                ob.reshape(bsz * n_tok, MIX_W), gg, y, cu, gates, p['ong'], p['s5d'], sw['wglu'], p['bglu'],
                sw['wbr'], sw['wout'])
    x_out = _ffn(l, x1, mod3, p['n2g'], sw['wg'], sw['wu'], sw['wd']).reshape(bsz, n_tok, D_MODEL)
    if latent:
        return x_out, None
    new_k = sh(k).reshape(bsz, n_tok, A_HEADS, 2, A_HEAD_DIM)
    new_v = sh(v).reshape(bsz, n_tok, A_HEADS, 2 * A_HEAD_DIM)
    hf = hfin.reshape(S5_G, bsz, 2, 2, S5_P)
    new_s5 = hf.transpose(1, 3, 2, 0, 4)
    return x_out, (new_k, new_v, new_gla, new_s5)


def kernel(x_prompt, x_sample, cache_diff_k, cache_diff_v, state_gla, state_s5, c, c_ctx, w_mod, b_mod, norm1_g, norm2_g, w_in, diff_qn_g, diff_kn_g, diff_lam, diff_subln_g, gla_wa2, gla_ba, gla_on_g, s5_lam_re, s5_lam_im, s5_log_dt, s5_b_re, s5_b_im, s5_c_re, s5_c_im, s5_d, s5_w_glu, s5_b_glu, w_branch, w_out, w_ffn_gate, w_ffn_up, w_ffn_down):
    weights = dict(norm1_g=norm1_g, norm2_g=norm2_g, w_in=w_in, diff_qn_g=diff_qn_g, diff_kn_g=diff_kn_g,
                   diff_lam=diff_lam, diff_subln_g=diff_subln_g, gla_wa2=gla_wa2, gla_ba=gla_ba,
                   gla_on_g=gla_on_g, s5_lam_re=s5_lam_re, s5_lam_im=s5_lam_im, s5_log_dt=s5_log_dt,
                   s5_b_re=s5_b_re, s5_b_im=s5_b_im, s5_c_re=s5_c_re, s5_c_im=s5_c_im, s5_d=s5_d,
                   s5_w_glu=s5_w_glu, s5_b_glu=s5_b_glu, w_branch=w_branch, w_out=w_out,
                   w_ffn_gate=w_ffn_gate, w_ffn_up=w_ffn_up, w_ffn_down=w_ffn_down)
    depth = w_mod.shape[0]
    dec_b = c.shape[0]
    cond8 = jnp.concatenate([c_ctx[None, :], c, jnp.zeros((SUBLANES - 1 - dec_b, D_MODEL), F32)], axis=0)
    mod = _modulation(cond8, w_mod, b_mod)
    rope_tabs = _rope_tables(x_sample.shape[1])
    sw = _prepare_shared(weights)
    y_prompt, y_sample = x_prompt, x_sample
    k_list, v_list, gla_list, s5_list = [], [], [], []
    for l in range(depth):
        p = _prepare_layer(l, weights)
        lam_init = 0.8 - 0.6 * math.exp(-0.3 * l)
        y_prompt, (k_l, v_l, g_l, s_l) = _layer(l, y_prompt, mod[l, 0:1][:, None, :], p, sw, lam_init, None, None)
        k_list.append(k_l)
        v_list.append(v_l)
        gla_list.append(g_l)
        s5_list.append(s_l)
        ctx = dict(k=cache_diff_k[:, l], v=cache_diff_v[:, l], gla=state_gla[:, l], s5=state_s5[:, l])
        y_sample, _ = _layer(l, y_sample, mod[l, 1:1 + dec_b][:, None, :], p, sw, lam_init, ctx, rope_tabs)
    return (y_prompt, y_sample, jnp.stack(k_list, axis=1), jnp.stack(v_list, axis=1),
            jnp.stack(gla_list, axis=1), jnp.stack(s5_list, axis=1))
```

```python
import functools
import math

import numpy as np
import jax
import jax.numpy as jnp
from jax import lax
from jax.experimental import pallas as pl
from jax.experimental.pallas import tpu as pltpu

F32 = jnp.float32
BF16 = jnp.bfloat16

D_MODEL = 1024
MIX_W = 512
A_HEADS = 4
A_HEAD_DIM = 64
GRID_W = 64
ROPE_THETA = 10000.0
B_HEADS = 4
B_KDIM = 64
B_VDIM = 128
GLA_RANK = 16
GLA_TAU = 16.0
S5_G = 32
S5_GROUP = 16
S5_P = 64
FFN_DIM = 2816
EPS = 1e-6

VMEM_LIMIT_BYTES = 56 * 1024 * 1024
INPROJ_TILE = 512
INPROJ_SUB_TILE = 256
DENSE_TILE = 256
ATTN_Q_TILE = 1024
ATTN_SUB_TILE = 256
GLA_CHUNK = 128
GLA_STEP_CHUNKS = 4
GLA_FIRST_DIFF_LEVEL = 3
S5_CHUNK = 16
SUBLANES = 8
LANES = 128
S5_TILE_GROUPS = LANES // 16
S5_ROW_BLOCK = 64

OFF_AQ, OFF_AK, OFF_AV, OFF_BQK, OFF_BV, OFF_BG, HEAD_COLS = 0, 512, 1024, 1536, 2048, 2560, 3072
TAIL_START = HEAD_COLS + 2 * GLA_RANK
TAIL_CU, TAIL_GZ, TAIL_COLS = 0, MIX_W, MIX_W + 3 * D_MODEL
BR_PAD = 128


def _cparams(*sem):
    return pltpu.CompilerParams(dimension_semantics=sem, vmem_limit_bytes=VMEM_LIMIT_BYTES)


def _resident(shape):
    nd = len(shape)
    return pl.BlockSpec(shape, lambda *_: (0,) * nd, pipeline_mode=pl.Buffered(1))


def _layer_block(arr, l, last=None):
    tail = arr.shape[1:] if last is None else arr.shape[1:-1] + (last,)
    return pl.BlockSpec((None,) + tail, lambda *_: (l,) + (0,) * len(tail), pipeline_mode=pl.Buffered(1))


def _split_bf16(x):
    hi = x.astype(BF16)
    lo = (x - hi.astype(F32)).astype(BF16)
    return hi, lo


def _dot(a, b):
    return jnp.dot(a, b, preferred_element_type=F32)


def _dot_nt(a, b):
    return lax.dot_general(a, b, (((1,), (1,)), ((), ())), preferred_element_type=F32)


def _dot_tn(a, b):
    return lax.dot_general(a, b, (((0,), (0,)), ((), ())), preferred_element_type=F32)


def _dot_f32(a, b, nt=False):
    d = _dot_nt if nt else _dot
    ah, al = _split_bf16(a)
    bh, bl = _split_bf16(b)
    return d(ah, bh) + d(ah, bl) + d(al, bh)


def _rms(x):
    return x * lax.rsqrt(jnp.mean(x * x, axis=-1, keepdims=True) + EPS)


def _mod_kernel(cond_ref, w_ref, b_ref, o_ref):
    cnd = cond_ref[...]
    s = (cnd * jax.nn.sigmoid(cnd))
    o_ref[...] = _dot_f32(s, w_ref[...]) + b_ref[...]


def _modulation(cond8, w_mod, b_mod):
    depth, d, n = w_mod.shape
    tn = 1536
    return pl.pallas_call(
        _mod_kernel,
        grid=(depth, n // tn),
        in_specs=[pl.BlockSpec((SUBLANES, d), lambda l, j: (0, 0)),
                  pl.BlockSpec((None, d, tn), lambda l, j: (l, 0, j)),
                  pl.BlockSpec((None, 1, tn), lambda l, j: (l, 0, j))],
        out_specs=pl.BlockSpec((None, SUBLANES, tn), lambda l, j: (l, 0, j)),
        out_shape=jax.ShapeDtypeStruct((depth, SUBLANES, n), F32),
        compiler_params=_cparams("parallel", "parallel"),
        name="modulation",
    )(cond8, w_mod, b_mod.reshape(depth, 1, n))


def _group_rms64(z, gsum, gspread, gain):
    hi, lo = _split_bf16(_dot((z * z).astype(BF16), gsum))
    ms = _dot(hi, gspread) + _dot(lo, gspread)
    return z * lax.rsqrt(ms + EPS) * gain


def _rope(z, c, s):
    n = z.shape[-1]
    lane = lax.broadcasted_iota(jnp.int32, z.shape, 1)
    first = (lane & 31) < 16
    partner = jnp.where(first, pltpu.roll(z, n - 16, 1), pltpu.roll(z, 16, 1))
    return z * c + partner * s


def _inproj_kernel(rope, x_ref, mod_ref, n1_ref, w_head_ref, w_tail_ref, w_rank_ref, gsum_ref, gspread_ref,
                   qg_ref, kg_ref, wa_ref, ba_ref, *rest):
    if rope:
        cos_ref, sin_ref = rest[:2]
        rest = rest[2:]
    q_ref, k_ref, v_ref, gqk_ref, gv_ref, gg_ref, la_ref, cu_ref, gate_ref = rest
    mod = mod_ref[...]
    sh1 = mod[:, 0:D_MODEL]
    sc1 = mod[:, D_MODEL:2 * D_MODEL]
    gsum = gsum_ref[...]
    gspread = gspread_ref[...]
    wa_hi, wa_lo = _split_bf16(wa_ref[...])
    wcopy = lax.broadcasted_iota(jnp.int32, wa_hi.shape, 0) // (2 * GLA_RANK)
    wa3 = jnp.where(wcopy == 1, wa_lo, wa_hi)
    sub = math.gcd(INPROJ_SUB_TILE, x_ref.shape[0])

    for rows in [slice(j * sub, (j + 1) * sub) for j in range(x_ref.shape[0] // sub)]:
        h = _rms(x_ref[rows, :]) * n1_ref[...] * (1.0 + sc1) + sh1
        hb = h.astype(BF16)

        def seg(a, b, w_ref=w_head_ref):
            return _dot(hb, w_ref[:, a:b])

        r = _dot(hb, w_rank_ref[...])
        r_hi, r_lo = _split_bf16(r)
        copy = lax.broadcasted_iota(jnp.int32, r.shape, 1) // (2 * GLA_RANK)
        pre = _dot(jnp.where(copy < 2, r_hi, r_lo), wa3) + ba_ref[...]
        la_ref[rows, :] = (jnp.minimum(pre, 0.0) - jnp.log1p(jnp.exp(-jnp.abs(pre)))) * (1.0 / GLA_TAU)

        q = _group_rms64(seg(OFF_AQ, OFF_AK), gsum, gspread, qg_ref[...])
        k = _group_rms64(seg(OFF_AK, OFF_AV), gsum, gspread, kg_ref[...])
        if rope:
            c = cos_ref[rows, :]
            s = sin_ref[rows, :]
            q = _rope(q, c, s)
            k = _rope(k, c, s)
        q_ref[rows, :] = (q * (A_HEAD_DIM ** -0.5)).astype(q_ref.dtype)
        k_ref[rows, :] = k.astype(k_ref.dtype)

        gate_ref[rows, :] = jax.nn.sigmoid(seg(TAIL_GZ, TAIL_COLS, w_tail_ref)).astype(gate_ref.dtype)
        bg = seg(OFF_BG, HEAD_COLS)
        gg_ref[rows, :] = (bg * jax.nn.sigmoid(bg)).astype(gg_ref.dtype)
        bqk = seg(OFF_BQK, OFF_BV)
        lane = lax.broadcasted_iota(jnp.int32, bqk.shape, 1)
        gqk_ref[rows, :] = jnp.where(lane < B_HEADS * B_KDIM, bqk * (B_KDIM ** -0.5), bqk).astype(gqk_ref.dtype)
        v_ref[rows, :] = seg(OFF_AV, OFF_BQK).astype(v_ref.dtype)
        gv_ref[rows, :] = seg(OFF_BV, OFF_BG).astype(gv_ref.dtype)
        cu_ref[rows, :] = seg(TAIL_CU, TAIL_GZ, w_tail_ref).astype(cu_ref.dtype)


def _token_tiling(n_tok, mod3, want):
    tm = math.gcd(want, n_tok // mod3.shape[0])
    tiles_per_mod = (n_tok // tm) // mod3.shape[0]
    return tm, pl.BlockSpec((None, 1, mod3.shape[-1]), lambda i: (i // tiles_per_mod, 0, 0))


def _inproj(l, x2, mod3, n1g, w_in, w_tail, w_rank, gsum, gspread, qg, kg, wa, ba, rope_tabs, kv_dtype):
    n_tok = x2.shape[0]
    tm, mod_spec = _token_tiling(n_tok, mod3, INPROJ_TILE)
    rope = rope_tabs is not None
    tok = lambda w: pl.BlockSpec((tm, w), lambda i: (i, 0))
    in_specs = [tok(D_MODEL), mod_spec, _resident(n1g.shape),
                _layer_block(w_in, l, HEAD_COLS), _layer_block(w_tail, l), _layer_block(w_rank, l),
                _resident(gsum.shape), _resident(gspread.shape),
                _resident(qg.shape), _resident(kg.shape), _resident(wa.shape), _resident(ba.shape)]
    args = [x2, mod3, n1g, w_in, w_tail, w_rank, gsum, gspread, qg, kg, wa, ba]
    if rope:
        tiles_per_seq = rope_tabs[0].shape[0] // tm
        for t in rope_tabs:
            in_specs.append(pl.BlockSpec((tm, MIX_W), lambda i: (i % tiles_per_seq, 0)))
            args.append(t)
    widths = [(MIX_W, BF16), (MIX_W, kv_dtype), (MIX_W, kv_dtype), (MIX_W, BF16), (MIX_W, BF16),
              (MIX_W, BF16), (MIX_W, F32), (MIX_W, F32), (3 * D_MODEL, BF16)]
    return pl.pallas_call(
        functools.partial(_inproj_kernel, rope),
        grid=(n_tok // tm,),
        in_specs=in_specs,
        out_specs=[tok(w) for w, _ in widths],
        out_shape=[jax.ShapeDtypeStruct((n_tok, w), dt) for w, dt in widths],
        compiler_params=_cparams("parallel"),
        name="inproj",
    )(*args)


def _attn_kernel(lam_init, n_seg, q_ref, *refs):
    k_refs = refs[:n_seg]
    v_refs = refs[n_seg:2 * n_seg]
    lamp_ref, subg_ref, o_ref = refs[2 * n_seg:]
    lv = lamp_ref[...]
    lam = (jnp.exp(jnp.sum(lv[0:1] * lv[1:2], axis=-1, keepdims=True))
           - jnp.exp(jnp.sum(lv[2:3] * lv[3:4], axis=-1, keepdims=True)) + lam_init)
    hd = 2 * A_HEAD_DIM
    n_batch, n_rows = q_ref.shape[0], q_ref.shape[1]
    tq = math.gcd(ATTN_SUB_TILE, n_rows)
    for bi, sub, h in [(a, b, c) for a in range(n_batch) for b in range(n_rows // tq) for c in range(A_HEADS)]:
        rows = slice(sub * tq, (sub + 1) * tq)
        sl = slice(h * hd, (h + 1) * hd)
        qh = q_ref[bi, rows, sl]
        first = lax.broadcasted_iota(jnp.int32, qh.shape, 1) < A_HEAD_DIM
        zero = jnp.zeros_like(qh)
        q2 = jnp.concatenate([jnp.where(first, qh, zero), jnp.where(first, zero, qh)], axis=0)
        scores = [_dot_nt(q2, k_ref[bi, :, sl].astype(BF16)) for k_ref in k_refs]
        m = functools.reduce(jnp.maximum, [jnp.max(s, axis=-1, keepdims=True) for s in scores])
        acc = None
        for s, v_ref in zip(scores, v_refs):
            e = jnp.exp((s - m).astype(BF16))
            v_ext = jnp.concatenate([v_ref[bi, :, sl].astype(BF16), jnp.ones((v_ref.shape[1], hd), BF16)], axis=1)
            pv = _dot(e, v_ext)
            acc = pv if acc is None else acc + pv
        o2 = acc[:, :hd] / acc[:, hd:]
        o = o2[:tq] - lam * o2[tq:]
        o_ref[bi, rows, sl] = (_rms(o) * subg_ref[:, sl] * (1.0 - lam_init)).astype(o_ref.dtype)


def _diff_attention(q, ks, vs, lamp, subg, lam_init):
    bsz, lq, w = q.shape
    tq = math.gcd(ATTN_Q_TILE, lq)
    bb = math.gcd(max(1, ATTN_Q_TILE // lq), bsz)
    kv_spec = lambda a: pl.BlockSpec((bb, a.shape[1], w), lambda b, i: (b, 0, 0))
    return pl.pallas_call(
        functools.partial(_attn_kernel, lam_init, len(ks)),
        grid=(bsz // bb, lq // tq),
        in_specs=[pl.BlockSpec((bb, tq, w), lambda b, i: (b, i, 0))]
                 + [kv_spec(a) for a in ks] + [kv_spec(a) for a in vs]
                 + [pl.BlockSpec(lamp.shape, lambda b, i: (0, 0)), pl.BlockSpec(subg.shape, lambda b, i: (0, 0))],
        out_specs=pl.BlockSpec((bb, tq, w), lambda b, i: (b, i, 0)),
        out_shape=jax.ShapeDtypeStruct((bsz, lq, w), BF16),
        compiler_params=_cparams("parallel", "parallel"),
        name="diff_attention",
    )(q, *ks, *vs, lamp, subg)


def _gla_masks(chunk):
    nlev = int(math.log2(chunk))
    assert 1 << nlev == chunk
    t = np.arange(chunk)[:, None]
    r = np.arange(chunk)[None, :]
    cum, pair = [], []
    for j in range(nlev + 1):
        start = (t >> j) << j
        end = start + (1 << j) - 1
        if 0 < j < GLA_FIRST_DIFF_LEVEL:
            cum.append((r >= start) & (r <= t))
            cum.append((r > t) & (r <= end))
        if j < nlev:
            pair.append(((t >> (j + 1)) == (r >> (j + 1))) & (((t >> j) & 1) == 1) & (((r >> j) & 1) == 0))
    cum.append(r <= t)
    pair.append(t == r)
    dup = lambda m: np.concatenate([m, m], axis=1)
    cum_f = dup(np.concatenate(cum, 0).astype(np.float32))
    pair_f = np.stack(pair, 0).astype(np.float32)
    cum_b = dup(np.concatenate([m[::-1, ::-1] for m in cum], 0).astype(np.float32))
    pair_b = pair_f[:, ::-1, ::-1]
    return (jnp.asarray(np.stack([cum_f, cum_b]), BF16), jnp.asarray(np.stack([pair_f, pair_b]), F32), nlev)


def _gla_chunk(qk, v, la, cum, pair_ref, d, st_ref, run_ref, bd, nlev, last_row):
    c = qk.shape[0]
    kw = B_HEADS * B_KDIM
    q = qk[:, :kw].astype(F32)
    k = qk[:, kw:].astype(F32)
    hi, lo = _split_bf16(la)
    sums = _dot(cum, jnp.concatenate([hi, lo], axis=0))
    n_mxu = GLA_FIRST_DIFF_LEVEL - 1
    run = sums[2 * n_mxu * c:]
    run_ref[...] = run

    def run_at(size, row_of_block):
        pieces = []
        for kb in range(c // size):
            r = row_of_block(kb)
            pieces.append(jnp.broadcast_to(run_ref[r:r + 1, :], (size, kw)) if 0 <= r < c
                          else jnp.zeros((size, kw), F32))
        return pieces[0] if len(pieces) == 1 else jnp.concatenate(pieces, axis=0)

    def factors(j):
        if j < GLA_FIRST_DIFF_LEVEL:
            return jnp.exp(sums[(2 * j - 2) * c:(2 * j - 1) * c]), jnp.exp(sums[(2 * j - 1) * c:(2 * j) * c])
        size = 1 << j
        if d == 0:
            before = run_at(size, lambda kb: kb * size - 1)
            last = run_at(size, lambda kb: (kb + 1) * size - 1)
            return jnp.exp(run - before), jnp.exp(last - run)
        after = run_at(size, lambda kb: (kb + 1) * size)
        first = run_at(size, lambda kb: kb * size)
        return jnp.exp(run - after), jnp.exp(first - run)

    lane = lax.broadcasted_iota(jnp.int32, (c, kw), 1)
    head_masks = [(lane >= h * B_KDIM) & (lane < (h + 1) * B_KDIM) for h in range(B_HEADS)]
    zero = jnp.zeros((c, kw), BF16)

    def scores(qf, kf, pm):
        qb = qf.astype(BF16)
        kb = kf.astype(BF16)
        stacked = _dot_nt(jnp.concatenate([jnp.where(m, qb, zero) for m in head_masks], axis=0), kb)
        return [pm * stacked[h * c:(h + 1) * c] for h in range(B_HEADS)]

    att = scores(q, k, pair_ref[d, nlev])
    for j in range(nlev):
        if j == 0:
            lev = scores(q * jnp.exp(la), k, pair_ref[d, 0])
        else:
            eq, ek = factors(j)
            lev = scores(q * eq, k * ek, pair_ref[d, j])
        att = [a + b for a, b in zip(att, lev)]
    eq, ek = factors(nlev)
    st = st_ref[...]
    o = _dot_nt((q * eq).astype(BF16), st.astype(BF16))
    outs = []
    for h in range(B_HEADS):
        sl = slice(h * B_VDIM, (h + 1) * B_VDIM)
        outs.append(o[:, sl] + _dot(att[h].astype(BF16), v[:, sl]))
    dec = eq[last_row:last_row + 1, :]
    st_ref[...] = st * dec + bd * _dot_tn(v, (k * ek).astype(BF16))
    return outs


def _gla_kernel(has_s0, nlev, *refs):
    if has_s0:
        s0_ref, refs = refs[0], refs[1:]
    (qkf_ref, vf_ref, laf_ref, qkb_ref, vb_ref, lab_ref, cum_ref, pair_ref, bd_ref,
     of_ref, ob_ref, sfin_ref, st_f, st_b, run_f, run_b) = refs
    i = pl.program_id(1)
    c = GLA_CHUNK
    n_sub = qkf_ref.shape[0] // c

    @pl.when(i == 0)
    def _():
        for d, st in enumerate((st_f, st_b)):
            if has_s0:
                zero = jnp.zeros((B_KDIM, B_VDIM), F32)
                full = jnp.concatenate(
                    [jnp.concatenate([s0_ref[d, h] if hh == h else zero for hh in range(B_HEADS)], axis=1)
                     for h in range(B_HEADS)], axis=0)
                st[...] = full.T
            else:
                st[...] = jnp.zeros_like(st)

    bd = bd_ref[...]
    for sub in range(n_sub):
        rf = slice(sub * c, (sub + 1) * c)
        outs = _gla_chunk(qkf_ref[rf, :], vf_ref[rf, :], laf_ref[rf, :], cum_ref[0], pair_ref, 0, st_f, run_f, bd,
                          nlev, c - 1)
        for h, o in enumerate(outs):
            of_ref[rf, h * B_VDIM:(h + 1) * B_VDIM] = o.astype(of_ref.dtype)
        rb = slice((n_sub - 1 - sub) * c, (n_sub - sub) * c)
        outs = _gla_chunk(qkb_ref[rb, :], vb_ref[rb, :], lab_ref[rb, :], cum_ref[1], pair_ref, 1, st_b, run_b, bd,
                          nlev, 0)
        for h, o in enumerate(outs):
            ob_ref[rb, h * B_VDIM:(h + 1) * B_VDIM] = o.astype(ob_ref.dtype)

    @pl.when(i == pl.num_programs(1) - 1)
    def _():
        for d, st in enumerate((st_f, st_b)):
            full = st[...].T
            for h in range(B_HEADS):
                sfin_ref[d, h] = full[h * B_KDIM:(h + 1) * B_KDIM, h * B_VDIM:(h + 1) * B_VDIM]


def _gla(gqk, gv, la, s0):
    bsz, n_tok, _ = gqk.shape
    c = math.gcd(GLA_STEP_CHUNKS * GLA_CHUNK, n_tok)
    n = n_tok // c
    cum, pair, nlev = _gla_masks(GLA_CHUNK)
    kw = B_HEADS * B_KDIM
    vw = B_HEADS * B_VDIM
    rows = np.arange(vw)[:, None] // B_VDIM
    cols = np.arange(kw)[None, :] // B_KDIM
    bd = jnp.asarray((rows == cols).astype(np.float32))
    fwd = lambda w, off=0: pl.BlockSpec((None, c, w), lambda b, i: (b, i, off))
    bwd = lambda w, off=0: pl.BlockSpec((None, c, w), lambda b, i: (b, n - 1 - i, off))
    in_specs = [fwd(2 * kw), fwd(vw), fwd(kw, 0), bwd(2 * kw), bwd(vw), bwd(kw, 1),
                _resident(cum.shape), _resident(pair.shape), _resident(bd.shape)]
    args = [gqk, gv, la, gqk, gv, la, cum, pair, bd]
    state_spec = pl.BlockSpec((None, 2, B_HEADS, B_KDIM, B_VDIM), lambda b, i: (b, 0, 0, 0, 0))
    if s0 is not None:
        in_specs.insert(0, state_spec)
        args.insert(0, s0)
    return pl.pallas_call(
        functools.partial(_gla_kernel, s0 is not None, nlev),
        grid=(bsz, n),
        in_specs=in_specs,
        out_specs=[fwd(vw), bwd(vw), state_spec],
        out_shape=[jax.ShapeDtypeStruct((bsz, n_tok, vw), BF16), jax.ShapeDtypeStruct((bsz, n_tok, vw), BF16),
                   jax.ShapeDtypeStruct((bsz, 2, B_HEADS, B_KDIM, B_VDIM), F32)],
        scratch_shapes=[pltpu.VMEM((vw, kw), F32), pltpu.VMEM((vw, kw), F32),
                        pltpu.VMEM((GLA_CHUNK, kw), F32), pltpu.VMEM((GLA_CHUNK, kw), F32)],
        compiler_params=_cparams("parallel", "arbitrary"),
        name="gla",
    )(*args)


def _s5_prep_kernel(lr_ref, li_ref, dt_ref, bre_ref, bim_ref, cre_ref, cim_ref,
                    toep_ref, wst_ref, cst_ref, apow_ref):
    t_len = S5_CHUNK
    grp = S5_GROUP
    rows = t_len * grp
    lr = lr_ref[...]
    li = li_ref[...]
    dt = jnp.exp(dt_ref[...])
    a = lr * dt
    th = li * dt
    mag = jnp.exp(a)
    ar = mag * jnp.cos(th)
    ai = mag * jnp.sin(th)
    den = lr * lr + li * li
    fr = ((ar - 1.0) * lr + ai * li) / den
    fi = (ai * lr - (ar - 1.0) * li) / den
    b_re = bre_ref[...]
    b_im = bim_ref[...]
    bbr = jnp.concatenate([fr * b_re - fi * b_im] * t_len, axis=0)
    bbi = jnp.concatenate([fr * b_im + fi * b_re] * t_len, axis=0)
    c_re = cre_ref[...]
    c_im = cim_ref[...]
    c_re_t = jnp.concatenate([c_re] * t_len, axis=0)
    c_im_t = jnp.concatenate([c_im] * t_len, axis=0)

    n_pow = -(-(t_len + 1) // SUBLANES) * SUBLANES
    ex = lax.broadcasted_iota(jnp.int32, (n_pow, LANES), 0).astype(F32)
    pmag = jnp.exp(ex * a)
    pow_re = pmag * jnp.cos(ex * th)
    pow_im = pmag * jnp.sin(ex * th)
    is_fwd = lax.broadcasted_iota(jnp.int32, (grp, LANES), 1) < S5_P

    def expand(tab, exp_fwd, exp_bwd):
        blocks = []
        for step in range(t_len):
            f = jnp.broadcast_to(tab[exp_fwd(step):exp_fwd(step) + 1, :], (grp, LANES))
            b = jnp.broadcast_to(tab[exp_bwd(step):exp_bwd(step) + 1, :], (grp, LANES))
            blocks.append(jnp.where(is_fwd, f, b))
        return jnp.concatenate(blocks, axis=0)

    pr = expand(pow_re, lambda s: t_len - 1 - s, lambda s: s)
    pi = expand(pow_im, lambda s: t_len - 1 - s, lambda s: s)
    xr = bbr * pr - bbi * pi
    xi = bbr * pi + bbi * pr
    wst_ref[...] = jnp.concatenate([xr, xi], axis=1).astype(wst_ref.dtype)

    fwd_rows = lax.broadcasted_iota(jnp.int32, (rows, LANES), 1) < S5_P
    c_re_rep = jnp.concatenate([c_re] * (LANES // grp), axis=0)
    c_im_rep = jnp.concatenate([c_im] * (LANES // grp), axis=0)
    zero = jnp.zeros_like(xr)

    def lag_kernel(keep):
        return (_dot_f32(jnp.where(keep, xr, zero), c_re_rep, nt=True)
                - _dot_f32(jnp.where(keep, xi, zero), c_im_rep, nt=True))

    ker_f = lag_kernel(fwd_rows)
    ker_b = lag_kernel(jnp.logical_not(fwd_rows))
    blk = lax.broadcasted_iota(jnp.int32, (rows, LANES), 1) // grp

    def column_block(t):
        up = (t_len - 1 - t) * grp
        down = t * grp
        f = ker_f[up:] if up == 0 else jnp.concatenate([ker_f[up:], jnp.zeros((up, LANES), F32)], axis=0)
        b = ker_b if down == 0 else jnp.concatenate([jnp.zeros((down, LANES), F32), ker_b[:rows - down]], axis=0)
        return f + b

    per_tile = LANES // grp
    for h in range(t_len // per_tile):
        tile = column_block(h * per_tile)
        for j in range(1, per_tile):
            tile = jnp.where(blk == j, column_block(h * per_tile + j), tile)
        toep_ref[:, h * LANES:(h + 1) * LANES] = tile.astype(toep_ref.dtype)

    pr = expand(pow_re, lambda t: t + 1, lambda t: t_len - t)
    pi = expand(pow_im, lambda t: t + 1, lambda t: t_len - t)
    cr = c_re_t * pr - c_im_t * pi
    ci = -(c_re_t * pi + c_im_t * pr)
    cst_ref[...] = jnp.concatenate([jnp.where(fwd_rows, cr, zero), jnp.where(fwd_rows, ci, zero),
                                    jnp.where(fwd_rows, zero, cr), jnp.where(fwd_rows, zero, ci)],
                                   axis=1).astype(cst_ref.dtype)
    apow_ref[...] = jnp.concatenate([pow_re[t_len:t_len + 1], pow_im[t_len:t_len + 1],
                                     jnp.zeros((SUBLANES - 2, LANES), F32)], axis=0)


def _s5_prep(lr, li, ldt, bre_t, bim_t, cre, cim):
    g = lr.shape[0]
    rows = S5_CHUNK * S5_GROUP
    grp = lambda *s: pl.BlockSpec((None,) + s, lambda i: (i,) + (0,) * len(s))
    return pl.pallas_call(
        _s5_prep_kernel,
        grid=(g,),
        in_specs=[grp(1, LANES)] * 3 + [grp(S5_GROUP, LANES)] * 4,
        out_specs=[grp(rows, rows), grp(rows, 4 * S5_P), grp(rows, 8 * S5_P), grp(SUBLANES, LANES)],
        out_shape=[jax.ShapeDtypeStruct((g, rows, rows), BF16),
                   jax.ShapeDtypeStruct((g, rows, 4 * S5_P), BF16),
                   jax.ShapeDtypeStruct((g, rows, 8 * S5_P), BF16),
                   jax.ShapeDtypeStruct((g, SUBLANES, LANES), F32)],
        compiler_params=_cparams("parallel"),
        name="s5_prep",
    )(lr, li, ldt, bre_t, bim_t, cre, cim)


def _block_transpose(xs, blk):
    n = len(xs)
    d = n // 2
    while d >= 1:
        low = (blk & d) == 0
        new = list(xs)
        for i in range(n):
            if i & d == 0:
                a, b = xs[i], xs[i | d]
                new[i] = jnp.where(low, a, pltpu.roll(b, d * S5_GROUP, 1))
                new[i | d] = jnp.where(low, pltpu.roll(a, LANES - d * S5_GROUP, 1), b)
        xs = new
        d //= 2
    return xs


def _s5_kernel(n_chunks, bp, splits, cu_ref, toep_ref, wst_ref, cst_ref, apow_ref, h0_ref, y_ref, hfin_ref,
               u_scr, s_scr, h_scr, yg_scr):
    t_len = S5_CHUNK
    ng = S5_TILE_GROUPS
    seq = n_chunks * t_len
    rows = bp * n_chunks
    rb = max(bp, math.gcd(S5_ROW_BLOCK, rows))
    cpb = rb // bp
    blk = lax.broadcasted_iota(jnp.int32, (rb, LANES), 1) // S5_GROUP

    def relayout_in(r, carry):
        r0 = pl.multiple_of(r * rb, rb)
        for h in range(t_len // ng):
            xs = []
            for b in range(ng):
                pieces = [cu_ref[pl.ds((r * cpb + j) * t_len + ng * h + b, bp, stride=seq), :] for j in range(cpb)]
                xs.append(pieces[0] if cpb == 1 else jnp.concatenate(pieces, axis=0))
            for g, tile in enumerate(_block_transpose(xs, blk)):
                u_scr[g, pl.ds(r0, rb), h * LANES:(h + 1) * LANES] = tile.astype(BF16)
        return carry

    lax.fori_loop(0, rows // rb, relayout_in, 0)

    half = 2 * S5_P
    for g in range(ng):
        s_scr[g] = _dot(u_scr[g], wst_ref[g])

    is_fwd = lax.broadcasted_iota(jnp.int32, (bp, half), 1) < S5_P

    def scan(init, store):
        def step(kk, hs):
            f0 = pl.multiple_of(kk * bp, bp)
            b0 = pl.multiple_of((n_chunks - 1 - kk) * bp, bp)
            new = []
            for g in range(ng):
                re, im = hs[g]
                if store:
                    h_scr[g, pl.ds(f0, bp), 0:half] = re
                    h_scr[g, pl.ds(f0, bp), half:2 * half] = im
                    h_scr[g, pl.ds(b0, bp), 2 * half:3 * half] = re
                    h_scr[g, pl.ds(b0, bp), 3 * half:4 * half] = im
                s_f = s_scr[g, pl.ds(f0, bp), :]
                s_b = s_scr[g, pl.ds(b0, bp), :]
                a_re = apow_ref[g, 0:1, :]
                a_im = apow_ref[g, 1:2, :]
                new.append((a_re * re - a_im * im + jnp.where(is_fwd, s_f[:, :half], s_b[:, :half]),
                            a_re * im + a_im * re + jnp.where(is_fwd, s_f[:, half:], s_b[:, half:])))
            return tuple(new)

        return lax.fori_loop(0, n_chunks, step, init)

    h0 = tuple((h0_ref[g, :, :half], h0_ref[g, :, half:]) for g in range(ng))
    init = h0
    if splits > 1:
        seg = lax.broadcasted_iota(jnp.int32, (bp, half), 0) & (splits - 1)
        take_prev = is_fwd & (seg >= 1)
        take_next = jnp.logical_not(is_fwd) & (seg <= splits - 2)
        for _ in range(splits - 1):
            fin = scan(init, store=False)
            init = tuple(tuple(part0 + jnp.where(take_prev, pltpu.roll(part, 1, 0),
                                                 jnp.where(take_next, pltpu.roll(part, bp - 1, 0), 0.0))
                               for part0, part in zip(h0[g], fin[g])) for g in range(ng))
    hs = scan(init, store=True)
    for g in range(ng):
        hfin_ref[g, :, :half] = hs[g][0]
        hfin_ref[g, :, half:] = hs[g][1]
        yg_scr[g] = _dot(u_scr[g], toep_ref[g]) + _dot_nt(h_scr[g].astype(BF16), cst_ref[g])

    def relayout_out(r, carry):
        r0 = pl.multiple_of(r * rb, rb)
        for h in range(t_len // ng):
            ys = [yg_scr[g, pl.ds(r0, rb), h * LANES:(h + 1) * LANES] for g in range(ng)]
            for b, tile in enumerate(_block_transpose(ys, blk)):
                for j in range(cpb):
                    y_ref[pl.ds((r * cpb + j) * t_len + ng * h + b, bp, stride=seq), :] = tile[j * bp:(j + 1) * bp]
        return carry

    lax.fori_loop(0, rows // rb, relayout_out, 0)


def _s5(cu, toep, wst, cst, apow, h0, bsz):
    n_rows, width = cu.shape
    splits = SUBLANES // bsz if SUBLANES % bsz == 0 else 1
    bp = bsz * splits
    n_chunks = n_rows // bp // S5_CHUNK
    rows = bp * n_chunks
    ng = S5_TILE_GROUPS
    w = S5_CHUNK * S5_GROUP
    p2 = 2 * S5_P
    if h0 is None:
        h0p = jnp.zeros((S5_G, bp, 2 * p2), F32)
    elif splits == 1:
        h0p = h0
    else:
        h04 = h0.reshape(S5_G, bsz, 1, 2, 2, S5_P)
        zero = jnp.zeros((S5_G, bsz, splits - 1, 2, 1, S5_P), F32)
        h0p = jnp.concatenate([jnp.concatenate([h04[..., 0:1, :], zero], axis=2),
                               jnp.concatenate([zero, h04[..., 1:2, :]], axis=2)], axis=4)
        h0p = h0p.reshape(S5_G, bp, 2 * p2)
    tile = lambda *s: pl.BlockSpec((ng,) + s, lambda j: (j,) + (0,) * len(s))
    lanes = pl.BlockSpec((n_rows, LANES), lambda j: (0, j))
    y, hfin = pl.pallas_call(
        functools.partial(_s5_kernel, n_chunks, bp, splits),
        grid=(width // LANES,),
        in_specs=[lanes, tile(w, w), tile(w, 2 * p2), tile(w, 4 * p2), tile(SUBLANES, p2), tile(bp, 2 * p2)],
        out_specs=[lanes, tile(bp, 2 * p2)],
        out_shape=[jax.ShapeDtypeStruct((n_rows, width), F32), jax.ShapeDtypeStruct((S5_G, bp, 2 * p2), F32)],
        scratch_shapes=[pltpu.VMEM((ng, rows, w), BF16), pltpu.VMEM((ng, rows, 2 * p2), F32),
                        pltpu.VMEM((ng, rows, 4 * p2), F32), pltpu.VMEM((ng, rows, w), F32)],
        compiler_params=_cparams("parallel"),
        name="s5_scan",
    )(cu, toep, wst, cst, apow, h0p)
    hf = hfin.reshape(S5_G, bsz, splits, 2, 2, S5_P)
    hfin = jnp.stack([hf[:, :, splits - 1, :, 0], hf[:, :, 0, :, 1]], axis=3)
    return y, hfin.reshape(S5_G, bsz, 2 * p2)


def _mix_ffn_kernel(x_ref, mod_ref, oa_ref, of_ref, ob_ref, gg_ref, y_ref, cu_ref, gate_ref,
                    ong_ref, s5d_ref, wglu_ref, bglu_ref, wbr_ref, wout_ref, n2_ref, wg_ref, wu_ref, wd_ref, o_ref):
    mod = mod_ref[...]
    g1 = mod[:, 2 * D_MODEL:3 * D_MODEL]
    o_gla = of_ref[...].astype(F32) + ob_ref[...].astype(F32)
    parts = []
    for h in range(B_HEADS):
        sl = slice(h * B_VDIM, (h + 1) * B_VDIM)
        parts.append((_rms(o_gla[:, sl]) * ong_ref[...] * gg_ref[:, sl].astype(F32)).astype(BF16))
    ob = jnp.concatenate(parts, axis=1)

    cu = cu_ref[...].astype(F32)
    z = y_ref[...].astype(F32) + s5d_ref[...] * cu
    yc = 0.5 * z * (1.0 + jnp.tanh(math.sqrt(2.0 / math.pi) * (z + 0.044715 * (z * z * z))))
    glu = _dot(yc.astype(BF16), wglu_ref[...]) + bglu_ref[...]
    oc = (glu[:, :MIX_W] * jax.nn.sigmoid(glu[:, MIX_W:])).astype(BF16)

    merged = None
    for r, br in enumerate((oa_ref[...], ob, oc)):
        term = gate_ref[:, r * D_MODEL:(r + 1) * D_MODEL].astype(F32) * _dot(br, wbr_ref[r])
        merged = term if merged is None else merged + term
    x = x_ref[...] + g1 * _dot(merged.astype(BF16), wout_ref[...])

    sh2 = mod[:, 3 * D_MODEL:4 * D_MODEL]
    sc2 = mod[:, 4 * D_MODEL:5 * D_MODEL]
    g2 = mod[:, 5 * D_MODEL:6 * D_MODEL]
    hb = (_rms(x) * n2_ref[...] * (1.0 + sc2) + sh2).astype(BF16)
    gate = _dot(hb, wg_ref[...])
    act = (gate * jax.nn.sigmoid(gate) * _dot(hb, wu_ref[...])).astype(BF16)
    o_ref[...] = x + g2 * _dot(act, wd_ref[...])


def _mix_ffn(l, x2, mod3, oa, of, ob, gg, y, cu, gates, ong, s5d, wglu, bglu, wbr, wout, n2g, wg, wu, wd):
    n_tok = x2.shape[0]
    tm, mod_spec = _token_tiling(n_tok, mod3, DENSE_TILE)
    tok = lambda w: pl.BlockSpec((tm, w), lambda i: (i, 0))
    return pl.pallas_call(
        _mix_ffn_kernel,
        grid=(n_tok // tm,),
        in_specs=[tok(D_MODEL), mod_spec,
                  tok(MIX_W), tok(MIX_W), tok(MIX_W), tok(MIX_W), tok(MIX_W), tok(MIX_W), tok(3 * D_MODEL),
                  _resident(ong.shape), _resident(s5d.shape), _layer_block(wglu, l), _resident(bglu.shape),
                  _layer_block(wbr, l), _layer_block(wout, l),
                  _resident(n2g.shape), _layer_block(wg, l), _layer_block(wu, l), _layer_block(wd, l)],
        out_specs=tok(D_MODEL),
        out_shape=jax.ShapeDtypeStruct((n_tok, D_MODEL), F32),
        compiler_params=_cparams("parallel"),
        name="mix_ffn",
    )(x2, mod3, oa, of, ob, gg, y, cu, gates, ong, s5d, wglu, bglu, wbr, wout, n2g, wg, wu, wd)


def _rope_tables(n_tok):
    t = np.arange(n_tok)
    row = (t // GRID_W).astype(np.float32)
    col = (t % GRID_W).astype(np.float32)
    half = A_HEAD_DIM // 2
    inv = jnp.asarray(ROPE_THETA, F32) ** (-jnp.arange(0, half, 2, dtype=F32) / half)
    ang_r = jnp.asarray(row)[:, None] * inv
    ang_c = jnp.asarray(col)[:, None] * inv
    cos = jnp.concatenate([jnp.cos(ang_r)] * 2 + [jnp.cos(ang_c)] * 2, axis=-1)
    sin = jnp.concatenate([-jnp.sin(ang_r), jnp.sin(ang_r), -jnp.sin(ang_c), jnp.sin(ang_c)], axis=-1)
    reps = MIX_W // A_HEAD_DIM
    return jnp.tile(cos, (1, reps)), jnp.tile(sin, (1, reps))


def _prepare_shared(w):
    w_in = w['w_in'].astype(BF16)
    w_rank = w_in[:, :, HEAD_COLS:TAIL_START]
    depth = w_in.shape[0]
    return dict(
        w_in=w_in, w_tail=w_in[:, :, TAIL_START:],
        w_rank=jnp.concatenate([w_rank, w_rank, w_rank,
                                jnp.zeros((depth, D_MODEL, BR_PAD - 6 * GLA_RANK), BF16)], axis=2),
        wglu=w['s5_w_glu'].astype(BF16), wbr=w['w_branch'].astype(BF16), wout=w['w_out'].astype(BF16),
        wg=w['w_ffn_gate'].astype(BF16), wu=w['w_ffn_up'].astype(BF16), wd=w['w_ffn_down'].astype(BF16))


def _prepare_layer(l, w):
    kw = B_HEADS * B_KDIM
    zk = jnp.zeros((GLA_RANK, kw), F32)
    wa1 = jnp.concatenate([jnp.concatenate([w['gla_wa2'][l, 0], zk], axis=1),
                           jnp.concatenate([zk, w['gla_wa2'][l, 1]], axis=1)], axis=0)
    wa = jnp.concatenate([wa1, wa1, wa1, jnp.zeros((BR_PAD - 6 * GLA_RANK, 2 * kw), F32)], axis=0)
    ba = w['gla_ba'][l].reshape(1, 2 * kw)
    gidx = np.arange(MIX_W) // A_HEAD_DIM
    member = (gidx[:, None] == np.arange(LANES)[None, :]).astype(np.float32)
    gsum = jnp.asarray(member / A_HEAD_DIM, BF16)
    gspread = jnp.asarray(member.T, BF16)
    vec = lambda a: a.transpose(1, 0, 2).reshape(S5_G, 1, 2 * S5_P)
    mat = lambda a: a.transpose(1, 2, 0, 3).reshape(S5_G, S5_GROUP, 2 * S5_P)
    ldt = jnp.broadcast_to(w['s5_log_dt'][l][:, :, None], (2, S5_G, S5_P))
    toep, wst, cst, ap = _s5_prep(
        vec(w['s5_lam_re'][l]), vec(w['s5_lam_im'][l]), vec(ldt),
        mat(jnp.swapaxes(w['s5_b_re'][l], -1, -2)), mat(jnp.swapaxes(w['s5_b_im'][l], -1, -2)),
        mat(w['s5_c_re'][l]), mat(w['s5_c_im'][l]))
    return dict(
        wa=wa, ba=ba, gsum=gsum, gspread=gspread,
        n1g=w['norm1_g'][l].reshape(1, -1), n2g=w['norm2_g'][l].reshape(1, -1),
        qg=jnp.tile(w['diff_qn_g'][l], MIX_W // A_HEAD_DIM).reshape(1, -1),
        kg=jnp.tile(w['diff_kn_g'][l], MIX_W // A_HEAD_DIM).reshape(1, -1),
        lamp=w['diff_lam'][l], subg=jnp.tile(w['diff_subln_g'][l], A_HEADS).reshape(1, -1),
        ong=w['gla_on_g'][l].reshape(1, -1), s5d=w['s5_d'][l].reshape(1, -1),
        toep=toep, wst=wst, cst=cst, apow=ap, bglu=w['s5_b_glu'][l].reshape(1, -1),
    )


def _layer(l, x, mod3, p, sw, lam_init, ctx, rope_tabs):
    bsz, n_tok, _ = x.shape
    x2 = x.reshape(bsz * n_tok, D_MODEL)
    latent = ctx is not None
    q, k, v, gqk, gv, gg, la, cu, gates = _inproj(
        l, x2, mod3, p['n1g'], sw['w_in'], sw['w_tail'], sw['w_rank'], p['gsum'], p['gspread'], p['qg'], p['kg'],
        p['wa'], p['ba'], rope_tabs if latent else None, BF16 if latent else F32)
    sh = lambda a: a.reshape(bsz, n_tok, a.shape[-1])

    if latent:
        keys = [ctx['k'].reshape(bsz, -1, MIX_W), sh(k)]
        vals = [ctx['v'].reshape(bsz, -1, MIX_W), sh(v)]
    else:
        keys, vals = [sh(k)], [sh(v)]
    oa = _diff_attention(sh(q), keys, vals, p['lamp'], p['subg'], lam_init)

    of, ob, new_gla = _gla(sh(gqk), sh(gv), sh(la), ctx['gla'] if latent else None)

    h0 = ctx['s5'].transpose(3, 0, 2, 1, 4).reshape(S5_G, bsz, 4 * S5_P) if latent else None
    y, hfin = _s5(cu, p['toep'], p['wst'], p['cst'], p['apow'], h0, bsz)

    x_out = _mix_ffn(l, x2, mod3, oa.reshape(bsz * n_tok, MIX_W), of.reshape(bsz * n_tok, MIX_W),
                     ob.reshape(bsz * n_tok, MIX_W), gg, y, cu, gates, p['ong'], p['s5d'], sw['wglu'], p['bglu'],
                     sw['wbr'], sw['wout'], p['n2g'], sw['wg'], sw['wu'], sw['wd']).reshape(bsz, n_tok, D_MODEL)
    if latent:
        return x_out, None
    new_k = sh(k).reshape(bsz, n_tok, A_HEADS, 2, A_HEAD_DIM)
    new_v = sh(v).reshape(bsz, n_tok, A_HEADS, 2 * A_HEAD_DIM)
    hf = hfin.reshape(S5_G, bsz, 2, 2, S5_P)
    new_s5 = hf.transpose(1, 3, 2, 0, 4)
    return x_out, (new_k, new_v, new_gla, new_s5)


def kernel(x_prompt, x_sample, cache_diff_k, cache_diff_v, state_gla, state_s5, c, c_ctx, w_mod, b_mod, norm1_g, norm2_g, w_in, diff_qn_g, diff_kn_g, diff_lam, diff_subln_g, gla_wa2, gla_ba, gla_on_g, s5_lam_re, s5_lam_im, s5_log_dt, s5_b_re, s5_b_im, s5_c_re, s5_c_im, s5_d, s5_w_glu, s5_b_glu, w_branch, w_out, w_ffn_gate, w_ffn_up, w_ffn_down):
    weights = dict(norm1_g=norm1_g, norm2_g=norm2_g, w_in=w_in, diff_qn_g=diff_qn_g, diff_kn_g=diff_kn_g,
                   diff_lam=diff_lam, diff_subln_g=diff_subln_g, gla_wa2=gla_wa2, gla_ba=gla_ba,
                   gla_on_g=gla_on_g, s5_lam_re=s5_lam_re, s5_lam_im=s5_lam_im, s5_log_dt=s5_log_dt,
                   s5_b_re=s5_b_re, s5_b_im=s5_b_im, s5_c_re=s5_c_re, s5_c_im=s5_c_im, s5_d=s5_d,
                   s5_w_glu=s5_w_glu, s5_b_glu=s5_b_glu, w_branch=w_branch, w_out=w_out,
                   w_ffn_gate=w_ffn_gate, w_ffn_up=w_ffn_up, w_ffn_down=w_ffn_down)
    depth = w_mod.shape[0]
    dec_b = c.shape[0]
    cond8 = jnp.concatenate([c_ctx[None, :], c, jnp.zeros((SUBLANES - 1 - dec_b, D_MODEL), F32)], axis=0)
    mod = _modulation(cond8, w_mod, b_mod)
    rope_tabs = _rope_tables(x_sample.shape[1])
    sw = _prepare_shared(weights)
    y_prompt, y_sample = x_prompt, x_sample
    k_list, v_list, gla_list, s5_list = [], [], [], []
    for l in range(depth):
        p = _prepare_layer(l, weights)
        lam_init = 0.8 - 0.6 * math.exp(-0.3 * l)
        y_prompt, (k_l, v_l, g_l, s_l) = _layer(l, y_prompt, mod[l, 0:1][:, None, :], p, sw, lam_init, None, None)
        k_list.append(k_l)
        v_list.append(v_l)
        gla_list.append(g_l)
        s5_list.append(s_l)
        ctx = dict(k=cache_diff_k[:, l], v=cache_diff_v[:, l], gla=state_gla[:, l], s5=state_s5[:, l])
        y_sample, _ = _layer(l, y_sample, mod[l, 1:1 + dec_b][:, None, :], p, sw, lam_init, ctx, rope_tabs)
    return (y_prompt, y_sample, jnp.stack(k_list, axis=1), jnp.stack(v_list, axis=1),
            jnp.stack(gla_list, axis=1), jnp.stack(s5_list, axis=1))
```

```python
import functools
import math

import numpy as np
import jax
import jax.numpy as jnp
from jax import lax
from jax.experimental import pallas as pl
from jax.experimental.pallas import tpu as pltpu

F32 = jnp.float32
BF16 = jnp.bfloat16

D_MODEL = 1024
MIX_W = 512
A_HEADS = 4
A_HEAD_DIM = 64
GRID_W = 64
ROPE_THETA = 10000.0
B_HEADS = 4
B_KDIM = 64
B_VDIM = 128
GLA_RANK = 16
GLA_TAU = 16.0
S5_G = 32
S5_GROUP = 16
S5_P = 64
FFN_DIM = 2816
EPS = 1e-6

VMEM_LIMIT_BYTES = 56 * 1024 * 1024
INPROJ_TILE = 512
INPROJ_SUB_TILE = 256
DENSE_TILE = 256
ATTN_Q_TILE = 1024
ATTN_SUB_TILE = 256
GLA_CHUNK = 128
GLA_STEP_CHUNKS = 4
GLA_FIRST_DIFF_LEVEL = 3
S5_CHUNK = 16
SUBLANES = 8
LANES = 128
S5_TILE_GROUPS = LANES // 16
S5_ROW_BLOCK = 64
S5_ROW_PAD = 8

OFF_AQ, OFF_AK, OFF_AV, OFF_BQK, OFF_BV, OFF_BG, HEAD_COLS = 0, 512, 1024, 1536, 2048, 2560, 3072
TAIL_START = HEAD_COLS + 2 * GLA_RANK
TAIL_CU, TAIL_GZ, TAIL_COLS = 0, MIX_W, MIX_W + 3 * D_MODEL
BR_PAD = 128


def _cparams(*sem):
    return pltpu.CompilerParams(dimension_semantics=sem, vmem_limit_bytes=VMEM_LIMIT_BYTES)


def _resident(shape):
    nd = len(shape)
    return pl.BlockSpec(shape, lambda *_: (0,) * nd, pipeline_mode=pl.Buffered(1))


def _layer_block(arr, l, last=None):
    tail = arr.shape[1:] if last is None else arr.shape[1:-1] + (last,)
    return pl.BlockSpec((None,) + tail, lambda *_: (l,) + (0,) * len(tail), pipeline_mode=pl.Buffered(1))


def _split_bf16(x):
    hi = x.astype(BF16)
    lo = (x - hi.astype(F32)).astype(BF16)
    return hi, lo


def _dot(a, b):
    return jnp.dot(a, b, preferred_element_type=F32)


def _dot_nt(a, b):
    return lax.dot_general(a, b, (((1,), (1,)), ((), ())), preferred_element_type=F32)


def _dot_tn(a, b):
    return lax.dot_general(a, b, (((0,), (0,)), ((), ())), preferred_element_type=F32)


def _dot_f32(a, b, nt=False):
    d = _dot_nt if nt else _dot
    ah, al = _split_bf16(a)
    bh, bl = _split_bf16(b)
    return d(ah, bh) + d(ah, bl) + d(al, bh)


def _rms(x):
    return x * lax.rsqrt(jnp.mean(x * x, axis=-1, keepdims=True) + EPS)


def _mod_kernel(cond_ref, w_ref, b_ref, o_ref):
    cnd = cond_ref[...]
    s = (cnd * jax.nn.sigmoid(cnd))
    o_ref[...] = _dot_f32(s, w_ref[...]) + b_ref[...]


def _modulation(cond8, w_mod, b_mod):
    depth, d, n = w_mod.shape
    tn = 1536
    return pl.pallas_call(
        _mod_kernel,
        grid=(depth, n // tn),
        in_specs=[pl.BlockSpec((SUBLANES, d), lambda l, j: (0, 0)),
                  pl.BlockSpec((None, d, tn), lambda l, j: (l, 0, j)),
                  pl.BlockSpec((None, 1, tn), lambda l, j: (l, 0, j))],
        out_specs=pl.BlockSpec((None, SUBLANES, tn), lambda l, j: (l, 0, j)),
        out_shape=jax.ShapeDtypeStruct((depth, SUBLANES, n), F32),
        compiler_params=_cparams("parallel", "parallel"),
        name="modulation",
    )(cond8, w_mod, b_mod.reshape(depth, 1, n))


def _group_rms64(z, gsum, gspread, gain):
    hi, lo = _split_bf16(_dot((z * z).astype(BF16), gsum))
    ms = _dot(hi, gspread) + _dot(lo, gspread)
    return z * lax.rsqrt(ms + EPS) * gain


def _rope(z, c, s):
    n = z.shape[-1]
    lane = lax.broadcasted_iota(jnp.int32, z.shape, 1)
    first = (lane & 31) < 16
    partner = jnp.where(first, pltpu.roll(z, n - 16, 1), pltpu.roll(z, 16, 1))
    return z * c + partner * s


def _inproj_kernel(rope, x_ref, mod_ref, n1_ref, w_head_ref, w_tail_ref, w_rank_ref, gsum_ref, gspread_ref,
                   qg_ref, kg_ref, wa_ref, ba_ref, *rest):
    if rope:
        cos_ref, sin_ref = rest[:2]
        rest = rest[2:]
    q_ref, k_ref, v_ref, gqk_ref, gv_ref, gg_ref, la_ref, cu_ref, gate_ref = rest[1:]
    mod = mod_ref[...]
    sh1 = mod[:, 0:D_MODEL]
    sc1 = mod[:, D_MODEL:2 * D_MODEL]
    gsum = gsum_ref[...]
    gspread = gspread_ref[...]
    wa_hi, wa_lo = _split_bf16(wa_ref[...])
    wcopy = lax.broadcasted_iota(jnp.int32, wa_hi.shape, 0) // (2 * GLA_RANK)
    wa3 = jnp.where(wcopy == 1, wa_lo, wa_hi)
    sub = math.gcd(INPROJ_SUB_TILE, x_ref.shape[0])

    for rows in [slice(j * sub, (j + 1) * sub) for j in range(x_ref.shape[0] // sub)]:
        h = _rms(x_ref[rows, :]) * n1_ref[...] * (1.0 + sc1) + sh1
        hb = h.astype(BF16)

        def seg(a, b, w_ref=w_head_ref):
            return _dot(hb, w_ref[:, a:b])

        r = _dot(hb, w_rank_ref[...])
        r_hi, r_lo = _split_bf16(r)
        copy = lax.broadcasted_iota(jnp.int32, r.shape, 1) // (2 * GLA_RANK)
        pre = _dot(jnp.where(copy < 2, r_hi, r_lo), wa3) + ba_ref[...]
        la_ref[rows, :] = (jnp.minimum(pre, 0.0) - jnp.log1p(jnp.exp(-jnp.abs(pre)))) * (1.0 / GLA_TAU)

        q = _group_rms64(seg(OFF_AQ, OFF_AK), gsum, gspread, qg_ref[...])
        k = _group_rms64(seg(OFF_AK, OFF_AV), gsum, gspread, kg_ref[...])
        if rope:
            c = cos_ref[rows, :]
            s = sin_ref[rows, :]
            q = _rope(q, c, s)
            k = _rope(k, c, s)
        q_ref[rows, :] = (q * (A_HEAD_DIM ** -0.5)).astype(q_ref.dtype)
        k_ref[rows, :] = k.astype(k_ref.dtype)

        gate_ref[rows, :] = jax.nn.sigmoid(seg(TAIL_GZ, TAIL_COLS, w_tail_ref)).astype(gate_ref.dtype)
        bg = seg(OFF_BG, HEAD_COLS)
        gg_ref[rows, :] = (bg * jax.nn.sigmoid(bg)).astype(gg_ref.dtype)
        bqk = seg(OFF_BQK, OFF_BV)
        lane = lax.broadcasted_iota(jnp.int32, bqk.shape, 1)
        gqk_ref[rows, :] = jnp.where(lane < B_HEADS * B_KDIM, bqk * (B_KDIM ** -0.5), bqk).astype(gqk_ref.dtype)
        v_ref[rows, :] = seg(OFF_AV, OFF_BQK).astype(v_ref.dtype)
        gv_ref[rows, :] = seg(OFF_BV, OFF_BG).astype(gv_ref.dtype)
        cu = seg(TAIL_CU, TAIL_GZ, w_tail_ref).astype(cu_ref.dtype)
        if len(cu_ref.shape) == 2:
            cu_ref[rows, :] = cu
        else:
            seq = cu_ref.shape[1]
            for s in range(sub // seq):
                cu_ref[(rows.start + s * seq) // seq] = cu[s * seq:(s + 1) * seq]


def _token_tiling(n_tok, mod3, want):
    tm = math.gcd(want, n_tok // mod3.shape[0])
    tiles_per_mod = (n_tok // tm) // mod3.shape[0]
    return tm, pl.BlockSpec((None, 1, mod3.shape[-1]), lambda i: (i // tiles_per_mod, 0, 0))


def _inproj(l, x2, mod3, n1g, w_in, w_tail, w_rank, gsum, gspread, qg, kg, wa, ba, rope_tabs, kv_dtype, s5_seq):
    n_tok = x2.shape[0]
    tm, mod_spec = _token_tiling(n_tok, mod3, INPROJ_TILE)
    rope = rope_tabs is not None
    tok = lambda w: pl.BlockSpec((tm, w), lambda i: (i, 0))
    in_specs = [tok(D_MODEL), mod_spec, _resident(n1g.shape),
                _layer_block(w_in, l, HEAD_COLS), _layer_block(w_tail, l), _layer_block(w_rank, l),
                _resident(gsum.shape), _resident(gspread.shape),
                _resident(qg.shape), _resident(kg.shape), _resident(wa.shape), _resident(ba.shape)]
    args = [x2, mod3, n1g, w_in, w_tail, w_rank, gsum, gspread, qg, kg, wa, ba]
    if rope:
        tiles_per_seq = rope_tabs[0].shape[0] // tm
        for t in rope_tabs:
            in_specs.append(pl.BlockSpec((tm, MIX_W), lambda i: (i % tiles_per_seq, 0)))
            args.append(t)
    widths = [(MIX_W, BF16), (MIX_W, kv_dtype), (MIX_W, kv_dtype), (MIX_W, BF16), (MIX_W, BF16),
              (MIX_W, BF16), (MIX_W, F32), (MIX_W, F32), (3 * D_MODEL, BF16)]
    out_specs = [tok(w) for w, _ in widths]
    out_shape = [jax.ShapeDtypeStruct((n_tok, w), dt) for w, dt in widths]
    cu_out = 7
    cu_shape = (n_tok // s5_seq, s5_seq + S5_ROW_PAD, MIX_W)
    if tm <= s5_seq:
        per_seq = s5_seq // tm
        out_specs[cu_out] = pl.BlockSpec((None, tm, MIX_W), lambda i: (i // per_seq, i % per_seq, 0))
    else:
        out_specs[cu_out] = pl.BlockSpec((tm // s5_seq, s5_seq, MIX_W), lambda i: (i, 0, 0))
    out_shape[cu_out] = jax.ShapeDtypeStruct(cu_shape, F32)
    in_specs.append(pl.BlockSpec(memory_space=pl.ANY))
    args.append(jnp.zeros(cu_shape, F32))
    return pl.pallas_call(
        functools.partial(_inproj_kernel, rope),
        grid=(n_tok // tm,),
        in_specs=in_specs,
        out_specs=out_specs,
        out_shape=out_shape,
        input_output_aliases={len(args) - 1: cu_out},
        compiler_params=_cparams("parallel"),
        name="inproj",
    )(*args)


def _attn_kernel(lam_init, n_seg, q_ref, *refs):
    k_refs = refs[:n_seg]
    v_refs = refs[n_seg:2 * n_seg]
    lamp_ref, subg_ref, o_ref = refs[2 * n_seg:]
    lv = lamp_ref[...]
    lam = (jnp.exp(jnp.sum(lv[0:1] * lv[1:2], axis=-1, keepdims=True))
           - jnp.exp(jnp.sum(lv[2:3] * lv[3:4], axis=-1, keepdims=True)) + lam_init)
    hd = 2 * A_HEAD_DIM
    n_batch, n_rows = q_ref.shape[0], q_ref.shape[1]
    tq = math.gcd(ATTN_SUB_TILE, n_rows)
    for bi, sub, h in [(a, b, c) for a in range(n_batch) for b in range(n_rows // tq) for c in range(A_HEADS)]:
        rows = slice(sub * tq, (sub + 1) * tq)
        sl = slice(h * hd, (h + 1) * hd)
        qh = q_ref[bi, rows, sl]
        first = lax.broadcasted_iota(jnp.int32, qh.shape, 1) < A_HEAD_DIM
        zero = jnp.zeros_like(qh)
        q2 = jnp.concatenate([jnp.where(first, qh, zero), jnp.where(first, zero, qh)], axis=0)
        scores = [_dot_nt(q2, k_ref[bi, :, sl].astype(BF16)) for k_ref in k_refs]
        m = functools.reduce(jnp.maximum, [jnp.max(s, axis=-1, keepdims=True) for s in scores])
        acc = None
        for s, v_ref in zip(scores, v_refs):
            e = jnp.exp((s - m).astype(BF16))
            v_ext = jnp.concatenate([v_ref[bi, :, sl].astype(BF16), jnp.ones((v_ref.shape[1], hd), BF16)], axis=1)
            pv = _dot(e, v_ext)
            acc = pv if acc is None else acc + pv
        o2 = acc[:, :hd] / acc[:, hd:]
        o = o2[:tq] - lam * o2[tq:]
        o_ref[bi, rows, sl] = (_rms(o) * subg_ref[:, sl] * (1.0 - lam_init)).astype(o_ref.dtype)


def _diff_attention(q, ks, vs, lamp, subg, lam_init):
    bsz, lq, w = q.shape
    tq = math.gcd(ATTN_Q_TILE, lq)
    bb = math.gcd(max(1, ATTN_Q_TILE // lq), bsz)
    kv_spec = lambda a: pl.BlockSpec((bb, a.shape[1], w), lambda b, i: (b, 0, 0))
    return pl.pallas_call(
        functools.partial(_attn_kernel, lam_init, len(ks)),
        grid=(bsz // bb, lq // tq),
        in_specs=[pl.BlockSpec((bb, tq, w), lambda b, i: (b, i, 0))]
                 + [kv_spec(a) for a in ks] + [kv_spec(a) for a in vs]
                 + [pl.BlockSpec(lamp.shape, lambda b, i: (0, 0)), pl.BlockSpec(subg.shape, lambda b, i: (0, 0))],
        out_specs=pl.BlockSpec((bb, tq, w), lambda b, i: (b, i, 0)),
        out_shape=jax.ShapeDtypeStruct((bsz, lq, w), BF16),
        compiler_params=_cparams("parallel", "parallel"),
        name="diff_attention",
    )(q, *ks, *vs, lamp, subg)


def _gla_masks(chunk):
    nlev = int(math.log2(chunk))
    assert 1 << nlev == chunk
    t = np.arange(chunk)[:, None]
    r = np.arange(chunk)[None, :]
    cum, pair = [], []
    for j in range(nlev + 1):
        start = (t >> j) << j
        end = start + (1 << j) - 1
        if 0 < j < GLA_FIRST_DIFF_LEVEL:
            cum.append((r >= start) & (r <= t))
            cum.append((r > t) & (r <= end))
        if j < nlev:
            pair.append(((t >> (j + 1)) == (r >> (j + 1))) & (((t >> j) & 1) == 1) & (((r >> j) & 1) == 0))
    cum.append(r <= t)
    pair.append(t == r)
    dup = lambda m: np.concatenate([m, m], axis=1)
    cum_f = dup(np.concatenate(cum, 0).astype(np.float32))
    pair_f = np.stack(pair, 0).astype(np.float32)
    cum_b = dup(np.concatenate([m[::-1, ::-1] for m in cum], 0).astype(np.float32))
    pair_b = pair_f[:, ::-1, ::-1]
    return (jnp.asarray(np.stack([cum_f, cum_b]), BF16), jnp.asarray(np.stack([pair_f, pair_b]), F32), nlev)


def _gla_chunk(qk, v, la, cum, pair_ref, d, st_ref, run_ref, bd, nlev, last_row):
    c = qk.shape[0]
    kw = B_HEADS * B_KDIM
    q = qk[:, :kw].astype(F32)
    k = qk[:, kw:].astype(F32)
    hi, lo = _split_bf16(la)
    sums = _dot(cum, jnp.concatenate([hi, lo], axis=0))
    n_mxu = GLA_FIRST_DIFF_LEVEL - 1
    run = sums[2 * n_mxu * c:]
    run_ref[...] = run

    def run_at(size, row_of_block):
        pieces = []
        for kb in range(c // size):
            r = row_of_block(kb)
            pieces.append(jnp.broadcast_to(run_ref[r:r + 1, :], (size, kw)) if 0 <= r < c
                          else jnp.zeros((size, kw), F32))
        return pieces[0] if len(pieces) == 1 else jnp.concatenate(pieces, axis=0)

    def factors(j):
        if j < GLA_FIRST_DIFF_LEVEL:
            return jnp.exp(sums[(2 * j - 2) * c:(2 * j - 1) * c]), jnp.exp(sums[(2 * j - 1) * c:(2 * j) * c])
        size = 1 << j
        if d == 0:
            before = run_at(size, lambda kb: kb * size - 1)
            last = run_at(size, lambda kb: (kb + 1) * size - 1)
            return jnp.exp(run - before), jnp.exp(last - run)
        after = run_at(size, lambda kb: (kb + 1) * size)
        first = run_at(size, lambda kb: kb * size)
        return jnp.exp(run - after), jnp.exp(first - run)

    lane = lax.broadcasted_iota(jnp.int32, (c, kw), 1)
    head_masks = [(lane >= h * B_KDIM) & (lane < (h + 1) * B_KDIM) for h in range(B_HEADS)]
    zero = jnp.zeros((c, kw), BF16)

    def scores(qf, kf, pm):
        qb = qf.astype(BF16)
        kb = kf.astype(BF16)
        stacked = _dot_nt(jnp.concatenate([jnp.where(m, qb, zero) for m in head_masks], axis=0), kb)
        return [pm * stacked[h * c:(h + 1) * c] for h in range(B_HEADS)]

    att = scores(q, k, pair_ref[d, nlev])
    for j in range(nlev):
        if j == 0:
            lev = scores(q * jnp.exp(la), k, pair_ref[d, 0])
        else:
            eq, ek = factors(j)
            lev = scores(q * eq, k * ek, pair_ref[d, j])
        att = [a + b for a, b in zip(att, lev)]
    eq, ek = factors(nlev)
    st = st_ref[...]
    o = _dot_nt((q * eq).astype(BF16), st.astype(BF16))
    outs = []
    for h in range(B_HEADS):
        sl = slice(h * B_VDIM, (h + 1) * B_VDIM)
        outs.append(o[:, sl] + _dot(att[h].astype(BF16), v[:, sl]))
    dec = eq[last_row:last_row + 1, :]
    st_ref[...] = st * dec + bd * _dot_tn(v, (k * ek).astype(BF16))
    return outs


def _gla_kernel(has_s0, nlev, *refs):
    if has_s0:
        s0_ref, refs = refs[0], refs[1:]
    (qkf_ref, vf_ref, laf_ref, qkb_ref, vb_ref, lab_ref, cum_ref, pair_ref, bd_ref,
     of_ref, ob_ref, sfin_ref, st_f, st_b, run_f, run_b) = refs
    i = pl.program_id(1)
    c = GLA_CHUNK
    n_sub = qkf_ref.shape[0] // c

    @pl.when(i == 0)
    def _():
        for d, st in enumerate((st_f, st_b)):
            if has_s0:
                zero = jnp.zeros((B_KDIM, B_VDIM), F32)
                full = jnp.concatenate(
                    [jnp.concatenate([s0_ref[d, h] if hh == h else zero for hh in range(B_HEADS)], axis=1)
                     for h in range(B_HEADS)], axis=0)
                st[...] = full.T
            else:
                st[...] = jnp.zeros_like(st)

    bd = bd_ref[...]
    for sub in range(n_sub):
        rf = slice(sub * c, (sub + 1) * c)
        outs = _gla_chunk(qkf_ref[rf, :], vf_ref[rf, :], laf_ref[rf, :], cum_ref[0], pair_ref, 0, st_f, run_f, bd,
                          nlev, c - 1)
        for h, o in enumerate(outs):
            of_ref[rf, h * B_VDIM:(h + 1) * B_VDIM] = o.astype(of_ref.dtype)
        rb = slice((n_sub - 1 - sub) * c, (n_sub - sub) * c)
        outs = _gla_chunk(qkb_ref[rb, :], vb_ref[rb, :], lab_ref[rb, :], cum_ref[1], pair_ref, 1, st_b, run_b, bd,
                          nlev, 0)
        for h, o in enumerate(outs):
            ob_ref[rb, h * B_VDIM:(h + 1) * B_VDIM] = o.astype(ob_ref.dtype)

    @pl.when(i == pl.num_programs(1) - 1)
    def _():
        for d, st in enumerate((st_f, st_b)):
            full = st[...].T
            for h in range(B_HEADS):
                sfin_ref[d, h] = full[h * B_KDIM:(h + 1) * B_KDIM, h * B_VDIM:(h + 1) * B_VDIM]


def _gla(gqk, gv, la, s0):
    bsz, n_tok, _ = gqk.shape
    c = math.gcd(GLA_STEP_CHUNKS * GLA_CHUNK, n_tok)
    n = n_tok // c
    cum, pair, nlev = _gla_masks(GLA_CHUNK)
    kw = B_HEADS * B_KDIM
    vw = B_HEADS * B_VDIM
    rows = np.arange(vw)[:, None] // B_VDIM
    cols = np.arange(kw)[None, :] // B_KDIM
    bd = jnp.asarray((rows == cols).astype(np.float32))
    fwd = lambda w, off=0: pl.BlockSpec((None, c, w), lambda b, i: (b, i, off))
    bwd = lambda w, off=0: pl.BlockSpec((None, c, w), lambda b, i: (b, n - 1 - i, off))
    in_specs = [fwd(2 * kw), fwd(vw), fwd(kw, 0), bwd(2 * kw), bwd(vw), bwd(kw, 1),
                _resident(cum.shape), _resident(pair.shape), _resident(bd.shape)]
    args = [gqk, gv, la, gqk, gv, la, cum, pair, bd]
    state_spec = pl.BlockSpec((None, 2, B_HEADS, B_KDIM, B_VDIM), lambda b, i: (b, 0, 0, 0, 0))
    if s0 is not None:
        in_specs.insert(0, state_spec)
        args.insert(0, s0)
    return pl.pallas_call(
        functools.partial(_gla_kernel, s0 is not None, nlev),
        grid=(bsz, n),
        in_specs=in_specs,
        out_specs=[fwd(vw), bwd(vw), state_spec],
        out_shape=[jax.ShapeDtypeStruct((bsz, n_tok, vw), BF16), jax.ShapeDtypeStruct((bsz, n_tok, vw), BF16),
                   jax.ShapeDtypeStruct((bsz, 2, B_HEADS, B_KDIM, B_VDIM), F32)],
        scratch_shapes=[pltpu.VMEM((vw, kw), F32), pltpu.VMEM((vw, kw), F32),
                        pltpu.VMEM((GLA_CHUNK, kw), F32), pltpu.VMEM((GLA_CHUNK, kw), F32)],
        compiler_params=_cparams("parallel", "arbitrary"),
        name="gla",
    )(*args)


def _s5_prep_kernel(lr_ref, li_ref, dt_ref, bre_ref, bim_ref, cre_ref, cim_ref,
                    toep_ref, wst_ref, cst_ref, apow_ref):
    t_len = S5_CHUNK
    grp = S5_GROUP
    rows = t_len * grp
    lr = lr_ref[...]
    li = li_ref[...]
    dt = jnp.exp(dt_ref[...])
    a = lr * dt
    th = li * dt
    mag = jnp.exp(a)
    ar = mag * jnp.cos(th)
    ai = mag * jnp.sin(th)
    den = lr * lr + li * li
    fr = ((ar - 1.0) * lr + ai * li) / den
    fi = (ai * lr - (ar - 1.0) * li) / den
    b_re = bre_ref[...]
    b_im = bim_ref[...]
    bbr = jnp.concatenate([fr * b_re - fi * b_im] * t_len, axis=0)
    bbi = jnp.concatenate([fr * b_im + fi * b_re] * t_len, axis=0)
    c_re = cre_ref[...]
    c_im = cim_ref[...]
    c_re_t = jnp.concatenate([c_re] * t_len, axis=0)
    c_im_t = jnp.concatenate([c_im] * t_len, axis=0)

    n_pow = -(-(t_len + 1) // SUBLANES) * SUBLANES
    ex = lax.broadcasted_iota(jnp.int32, (n_pow, LANES), 0).astype(F32)
    pmag = jnp.exp(ex * a)
    pow_re = pmag * jnp.cos(ex * th)
    pow_im = pmag * jnp.sin(ex * th)
    is_fwd = lax.broadcasted_iota(jnp.int32, (grp, LANES), 1) < S5_P

    def expand(tab, exp_fwd, exp_bwd):
        blocks = []
        for step in range(t_len):
            f = jnp.broadcast_to(tab[exp_fwd(step):exp_fwd(step) + 1, :], (grp, LANES))
            b = jnp.broadcast_to(tab[exp_bwd(step):exp_bwd(step) + 1, :], (grp, LANES))
            blocks.append(jnp.where(is_fwd, f, b))
        return jnp.concatenate(blocks, axis=0)

    pr = expand(pow_re, lambda s: t_len - 1 - s, lambda s: s)
    pi = expand(pow_im, lambda s: t_len - 1 - s, lambda s: s)
    xr = bbr * pr - bbi * pi
    xi = bbr * pi + bbi * pr
    wst_ref[...] = jnp.concatenate([xr, xi], axis=1).astype(wst_ref.dtype)

    fwd_rows = lax.broadcasted_iota(jnp.int32, (rows, LANES), 1) < S5_P
    c_re_rep = jnp.concatenate([c_re] * (LANES // grp), axis=0)
    c_im_rep = jnp.concatenate([c_im] * (LANES // grp), axis=0)
    zero = jnp.zeros_like(xr)

    def lag_kernel(keep):
        return (_dot_f32(jnp.where(keep, xr, zero), c_re_rep, nt=True)
                - _dot_f32(jnp.where(keep, xi, zero), c_im_rep, nt=True))

    ker_f = lag_kernel(fwd_rows)
    ker_b = lag_kernel(jnp.logical_not(fwd_rows))
    blk = lax.broadcasted_iota(jnp.int32, (rows, LANES), 1) // grp

    def column_block(t):
        up = (t_len - 1 - t) * grp
        down = t * grp
        f = ker_f[up:] if up == 0 else jnp.concatenate([ker_f[up:], jnp.zeros((up, LANES), F32)], axis=0)
        b = ker_b if down == 0 else jnp.concatenate([jnp.zeros((down, LANES), F32), ker_b[:rows - down]], axis=0)
        return f + b

    per_tile = LANES // grp
    for h in range(t_len // per_tile):
        tile = column_block(h * per_tile)
        for j in range(1, per_tile):
            tile = jnp.where(blk == j, column_block(h * per_tile + j), tile)
        toep_ref[:, h * LANES:(h + 1) * LANES] = tile.astype(toep_ref.dtype)

    pr = expand(pow_re, lambda t: t + 1, lambda t: t_len - t)
    pi = expand(pow_im, lambda t: t + 1, lambda t: t_len - t)
    cr = c_re_t * pr - c_im_t * pi
    ci = -(c_re_t * pi + c_im_t * pr)
    cst_ref[...] = jnp.concatenate([jnp.where(fwd_rows, cr, zero), jnp.where(fwd_rows, ci, zero),
                                    jnp.where(fwd_rows, zero, cr), jnp.where(fwd_rows, zero, ci)],
                                   axis=1).astype(cst_ref.dtype)
    apow_ref[...] = jnp.concatenate([pow_re[t_len:t_len + 1], pow_im[t_len:t_len + 1],
                                     jnp.zeros((SUBLANES - 2, LANES), F32)], axis=0)


def _s5_prep(lr, li, ldt, bre_t, bim_t, cre, cim):
    g = lr.shape[0]
    rows = S5_CHUNK * S5_GROUP
    grp = lambda *s: pl.BlockSpec((None,) + s, lambda i: (i,) + (0,) * len(s))
    return pl.pallas_call(
        _s5_prep_kernel,
        grid=(g,),
        in_specs=[grp(1, LANES)] * 3 + [grp(S5_GROUP, LANES)] * 4,
        out_specs=[grp(rows, rows), grp(rows, 4 * S5_P), grp(rows, 8 * S5_P), grp(SUBLANES, LANES)],
        out_shape=[jax.ShapeDtypeStruct((g, rows, rows), BF16),
                   jax.ShapeDtypeStruct((g, rows, 4 * S5_P), BF16),
                   jax.ShapeDtypeStruct((g, rows, 8 * S5_P), BF16),
                   jax.ShapeDtypeStruct((g, SUBLANES, LANES), F32)],
        compiler_params=_cparams("parallel"),
        name="s5_prep",
    )(lr, li, ldt, bre_t, bim_t, cre, cim)


def _block_transpose(xs, blk):
    n = len(xs)
    d = n // 2
    while d >= 1:
        low = (blk & d) == 0
        new = list(xs)
        for i in range(n):
            if i & d == 0:
                a, b = xs[i], xs[i | d]
                new[i] = jnp.where(low, a, pltpu.roll(b, d * S5_GROUP, 1))
                new[i | d] = jnp.where(low, pltpu.roll(a, LANES - d * S5_GROUP, 1), b)
        xs = new
        d //= 2
    return xs


def _s5_kernel(n_chunks, bp, splits, cu_ref, toep_ref, wst_ref, cst_ref, apow_ref, h0_ref, y_ref, hfin_ref,
               u_scr, s_scr, h_scr, yg_scr):
    t_len = S5_CHUNK
    ng = S5_TILE_GROUPS
    seq = n_chunks * t_len
    pitch = seq + S5_ROW_PAD
    rows = bp * n_chunks
    rb = max(bp, math.gcd(S5_ROW_BLOCK, rows))
    cpb = rb // bp
    blk = lax.broadcasted_iota(jnp.int32, (rb, LANES), 1) // S5_GROUP

    def relayout_in(r, carry):
        r0 = pl.multiple_of(r * rb, rb)
        for h in range(t_len // ng):
            xs = []
            for b in range(ng):
                pieces = [cu_ref[pl.ds((r * cpb + j) * t_len + ng * h + b, bp, stride=pitch), :] for j in range(cpb)]
                xs.append(pieces[0] if cpb == 1 else jnp.concatenate(pieces, axis=0))
            for g, tile in enumerate(_block_transpose(xs, blk)):
                u_scr[g, pl.ds(r0, rb), h * LANES:(h + 1) * LANES] = tile.astype(BF16)
        return carry

    lax.fori_loop(0, rows // rb, relayout_in, 0)

    half = 2 * S5_P
    for g in range(ng):
        s_scr[g] = _dot(u_scr[g], wst_ref[g])

    is_fwd = lax.broadcasted_iota(jnp.int32, (bp, half), 1) < S5_P

    def scan(init, store):
        def step(kk, hs):
            f0 = pl.multiple_of(kk * bp, bp)
            b0 = pl.multiple_of((n_chunks - 1 - kk) * bp, bp)
            new = []
            for g in range(ng):
                re, im = hs[g]
                if store:
                    h_scr[g, pl.ds(f0, bp), 0:half] = re
                    h_scr[g, pl.ds(f0, bp), half:2 * half] = im
                    h_scr[g, pl.ds(b0, bp), 2 * half:3 * half] = re
                    h_scr[g, pl.ds(b0, bp), 3 * half:4 * half] = im
                s_f = s_scr[g, pl.ds(f0, bp), :]
                s_b = s_scr[g, pl.ds(b0, bp), :]
                a_re = apow_ref[g, 0:1, :]
                a_im = apow_ref[g, 1:2, :]
                new.append((a_re * re - a_im * im + jnp.where(is_fwd, s_f[:, :half], s_b[:, :half]),
                            a_re * im + a_im * re + jnp.where(is_fwd, s_f[:, half:], s_b[:, half:])))
            return tuple(new)

        return lax.fori_loop(0, n_chunks, step, init)

    h0 = tuple((h0_ref[g, :, :half], h0_ref[g, :, half:]) for g in range(ng))
    init = h0
    if splits > 1:
        seg = lax.broadcasted_iota(jnp.int32, (bp, half), 0) & (splits - 1)
        take_prev = is_fwd & (seg >= 1)
        take_next = jnp.logical_not(is_fwd) & (seg <= splits - 2)
        for _ in range(splits - 1):
            fin = scan(init, store=False)
            init = tuple(tuple(part0 + jnp.where(take_prev, pltpu.roll(part, 1, 0),
                                                 jnp.where(take_next, pltpu.roll(part, bp - 1, 0), 0.0))
                               for part0, part in zip(h0[g], fin[g])) for g in range(ng))
    hs = scan(init, store=True)
    for g in range(ng):
        hfin_ref[g, :, :half] = hs[g][0]
        hfin_ref[g, :, half:] = hs[g][1]
        yg_scr[g] = _dot(u_scr[g], toep_ref[g]) + _dot_nt(h_scr[g].astype(BF16), cst_ref[g])

    def relayout_out(r, carry):
        r0 = pl.multiple_of(r * rb, rb)
        for h in range(t_len // ng):
            ys = [yg_scr[g, pl.ds(r0, rb), h * LANES:(h + 1) * LANES] for g in range(ng)]
            for b, tile in enumerate(_block_transpose(ys, blk)):
                for j in range(cpb):
                    y_ref[pl.ds((r * cpb + j) * t_len + ng * h + b, bp, stride=pitch), :] = tile[j * bp:(j + 1) * bp]
        return carry

    lax.fori_loop(0, rows // rb, relayout_out, 0)
    for pb in range(bp):
        y_ref[pb * pitch + seq:(pb + 1) * pitch, :] = jnp.zeros((S5_ROW_PAD, LANES), F32)


def _s5_splits(bsz):
    return SUBLANES // bsz if SUBLANES % bsz == 0 else 1


def _s5(cu, toep, wst, cst, apow, h0, bsz):
    bp, pitch, width = cu.shape
    splits = _s5_splits(bsz)
    assert bp == bsz * splits
    n_chunks = (pitch - S5_ROW_PAD) // S5_CHUNK
    n_rows = bp * pitch
    cu = cu.reshape(n_rows, width)
    rows = bp * n_chunks
    ng = S5_TILE_GROUPS
    w = S5_CHUNK * S5_GROUP
    p2 = 2 * S5_P
    if h0 is None:
        h0p = jnp.zeros((S5_G, bp, 2 * p2), F32)
    elif splits == 1:
        h0p = h0
    else:
        h04 = h0.reshape(S5_G, bsz, 1, 2, 2, S5_P)
        zero = jnp.zeros((S5_G, bsz, splits - 1, 2, 1, S5_P), F32)
        h0p = jnp.concatenate([jnp.concatenate([h04[..., 0:1, :], zero], axis=2),
                               jnp.concatenate([zero, h04[..., 1:2, :]], axis=2)], axis=4)
        h0p = h0p.reshape(S5_G, bp, 2 * p2)
    tile = lambda *s: pl.BlockSpec((ng,) + s, lambda j: (j,) + (0,) * len(s))
    lanes = lambda r: pl.BlockSpec((r, LANES), lambda j: (0, j))
    out_rows = bp * (n_chunks * S5_CHUNK + S5_ROW_PAD)
    y, hfin = pl.pallas_call(
        functools.partial(_s5_kernel, n_chunks, bp, splits),
        grid=(width // LANES,),
        in_specs=[lanes(n_rows), tile(w, w), tile(w, 2 * p2), tile(w, 4 * p2), tile(SUBLANES, p2),
                  tile(bp, 2 * p2)],
        out_specs=[lanes(out_rows), tile(bp, 2 * p2)],
        out_shape=[jax.ShapeDtypeStruct((out_rows, width), F32), jax.ShapeDtypeStruct((S5_G, bp, 2 * p2), F32)],
        scratch_shapes=[pltpu.VMEM((ng, rows, w), BF16), pltpu.VMEM((ng, rows, 2 * p2), F32),
                        pltpu.VMEM((ng, rows, 4 * p2), F32), pltpu.VMEM((ng, rows, w), F32)],
        compiler_params=_cparams("parallel"),
        name="s5_scan",
    )(cu, toep, wst, cst, apow, h0p)
    hf = hfin.reshape(S5_G, bsz, splits, 2, 2, S5_P)
    hfin = jnp.stack([hf[:, :, splits - 1, :, 0], hf[:, :, 0, :, 1]], axis=3)
    return y.reshape(bp, out_rows // bp, width), hfin.reshape(S5_G, bsz, 2 * p2)


def _mix_ffn_kernel(x_ref, mod_ref, oa_ref, of_ref, ob_ref, gg_ref, y_ref, cu_ref, gate_ref,
                    ong_ref, s5d_ref, wglu_ref, bglu_ref, wbr_ref, wout_ref, n2_ref, wg_ref, wu_ref, wd_ref, o_ref):
    mod = mod_ref[...]
    g1 = mod[:, 2 * D_MODEL:3 * D_MODEL]
    o_gla = of_ref[...].astype(F32) + ob_ref[...].astype(F32)
    parts = []
    for h in range(B_HEADS):
        sl = slice(h * B_VDIM, (h + 1) * B_VDIM)
        parts.append((_rms(o_gla[:, sl]) * ong_ref[...] * gg_ref[:, sl].astype(F32)).astype(BF16))
    ob = jnp.concatenate(parts, axis=1)

    cu = cu_ref[...].astype(F32)
    z = y_ref[...].astype(F32) + s5d_ref[...] * cu
    yc = 0.5 * z * (1.0 + jnp.tanh(math.sqrt(2.0 / math.pi) * (z + 0.044715 * (z * z * z))))
    glu = _dot(yc.astype(BF16), wglu_ref[...]) + bglu_ref[...]
    oc = (glu[:, :MIX_W] * jax.nn.sigmoid(glu[:, MIX_W:])).astype(BF16)

    merged = None
    for r, br in enumerate((oa_ref[...], ob, oc)):
        term = gate_ref[:, r * D_MODEL:(r + 1) * D_MODEL].astype(F32) * _dot(br, wbr_ref[r])
        merged = term if merged is None else merged + term
    x = x_ref[...] + g1 * _dot(merged.astype(BF16), wout_ref[...])

    sh2 = mod[:, 3 * D_MODEL:4 * D_MODEL]
    sc2 = mod[:, 4 * D_MODEL:5 * D_MODEL]
    g2 = mod[:, 5 * D_MODEL:6 * D_MODEL]
    hb = (_rms(x) * n2_ref[...] * (1.0 + sc2) + sh2).astype(BF16)
    gate = _dot(hb, wg_ref[...])
    act = (gate * jax.nn.sigmoid(gate) * _dot(hb, wu_ref[...])).astype(BF16)
    o_ref[...] = x + g2 * _dot(act, wd_ref[...])


def _mix_ffn(l, x2, mod3, oa, of, ob, gg, y, cu, gates, ong, s5d, wglu, bglu, wbr, wout, n2g, wg, wu, wd):
    n_tok = x2.shape[0]
    seq = n_tok // y.shape[0]
    tm, mod_spec = _token_tiling(n_tok, mod3, math.gcd(DENSE_TILE, seq))
    tiles_per_seq = seq // tm
    tok = lambda w: pl.BlockSpec((tm, w), lambda i: (i, 0))
    y_spec = pl.BlockSpec((None, tm, MIX_W), lambda i: (i // tiles_per_seq, i % tiles_per_seq, 0))
    return pl.pallas_call(
        _mix_ffn_kernel,
        grid=(n_tok // tm,),
        in_specs=[tok(D_MODEL), mod_spec,
                  tok(MIX_W), tok(MIX_W), tok(MIX_W), tok(MIX_W), y_spec, y_spec, tok(3 * D_MODEL),
                  _resident(ong.shape), _resident(s5d.shape), _layer_block(wglu, l), _resident(bglu.shape),
                  _layer_block(wbr, l), _layer_block(wout, l),
                  _resident(n2g.shape), _layer_block(wg, l), _layer_block(wu, l), _layer_block(wd, l)],
        out_specs=tok(D_MODEL),
        out_shape=jax.ShapeDtypeStruct((n_tok, D_MODEL), F32),
        compiler_params=_cparams("parallel"),
        name="mix_ffn",
    )(x2, mod3, oa, of, ob, gg, y, cu, gates, ong, s5d, wglu, bglu, wbr, wout, n2g, wg, wu, wd)


def _rope_tables(n_tok):
    t = np.arange(n_tok)
    row = (t // GRID_W).astype(np.float32)
    col = (t % GRID_W).astype(np.float32)
    half = A_HEAD_DIM // 2
    inv = jnp.asarray(ROPE_THETA, F32) ** (-jnp.arange(0, half, 2, dtype=F32) / half)
    ang_r = jnp.asarray(row)[:, None] * inv
    ang_c = jnp.asarray(col)[:, None] * inv
    cos = jnp.concatenate([jnp.cos(ang_r)] * 2 + [jnp.cos(ang_c)] * 2, axis=-1)
    sin = jnp.concatenate([-jnp.sin(ang_r), jnp.sin(ang_r), -jnp.sin(ang_c), jnp.sin(ang_c)], axis=-1)
    reps = MIX_W // A_HEAD_DIM
    return jnp.tile(cos, (1, reps)), jnp.tile(sin, (1, reps))


def _prepare_shared(w):
    w_in = w['w_in'].astype(BF16)
    w_rank = w_in[:, :, HEAD_COLS:TAIL_START]
    depth = w_in.shape[0]
    return dict(
        w_in=w_in, w_tail=w_in[:, :, TAIL_START:],
        w_rank=jnp.concatenate([w_rank, w_rank, w_rank,
                                jnp.zeros((depth, D_MODEL, BR_PAD - 6 * GLA_RANK), BF16)], axis=2),
        wglu=w['s5_w_glu'].astype(BF16), wbr=w['w_branch'].astype(BF16), wout=w['w_out'].astype(BF16),
        wg=w['w_ffn_gate'].astype(BF16), wu=w['w_ffn_up'].astype(BF16), wd=w['w_ffn_down'].astype(BF16))


def _prepare_layer(l, w):
    kw = B_HEADS * B_KDIM
    zk = jnp.zeros((GLA_RANK, kw), F32)
    wa1 = jnp.concatenate([jnp.concatenate([w['gla_wa2'][l, 0], zk], axis=1),
                           jnp.concatenate([zk, w['gla_wa2'][l, 1]], axis=1)], axis=0)
    wa = jnp.concatenate([wa1, wa1, wa1, jnp.zeros((BR_PAD - 6 * GLA_RANK, 2 * kw), F32)], axis=0)
    ba = w['gla_ba'][l].reshape(1, 2 * kw)
    gidx = np.arange(MIX_W) // A_HEAD_DIM
    member = (gidx[:, None] == np.arange(LANES)[None, :]).astype(np.float32)
    gsum = jnp.asarray(member / A_HEAD_DIM, BF16)
    gspread = jnp.asarray(member.T, BF16)
    vec = lambda a: a.transpose(1, 0, 2).reshape(S5_G, 1, 2 * S5_P)
    mat = lambda a: a.transpose(1, 2, 0, 3).reshape(S5_G, S5_GROUP, 2 * S5_P)
    ldt = jnp.broadcast_to(w['s5_log_dt'][l][:, :, None], (2, S5_G, S5_P))
    toep, wst, cst, ap = _s5_prep(
        vec(w['s5_lam_re'][l]), vec(w['s5_lam_im'][l]), vec(ldt),
        mat(jnp.swapaxes(w['s5_b_re'][l], -1, -2)), mat(jnp.swapaxes(w['s5_b_im'][l], -1, -2)),
        mat(w['s5_c_re'][l]), mat(w['s5_c_im'][l]))
    return dict(
        wa=wa, ba=ba, gsum=gsum, gspread=gspread,
        n1g=w['norm1_g'][l].reshape(1, -1), n2g=w['norm2_g'][l].reshape(1, -1),
        qg=jnp.tile(w['diff_qn_g'][l], MIX_W // A_HEAD_DIM).reshape(1, -1),
        kg=jnp.tile(w['diff_kn_g'][l], MIX_W // A_HEAD_DIM).reshape(1, -1),
        lamp=w['diff_lam'][l], subg=jnp.tile(w['diff_subln_g'][l], A_HEADS).reshape(1, -1),
        ong=w['gla_on_g'][l].reshape(1, -1), s5d=w['s5_d'][l].reshape(1, -1),
        toep=toep, wst=wst, cst=cst, apow=ap, bglu=w['s5_b_glu'][l].reshape(1, -1),
    )


def _layer(l, x, mod3, p, sw, lam_init, ctx, rope_tabs):
    bsz, n_tok, _ = x.shape
    x2 = x.reshape(bsz * n_tok, D_MODEL)
    latent = ctx is not None
    q, k, v, gqk, gv, gg, la, cu, gates = _inproj(
        l, x2, mod3, p['n1g'], sw['w_in'], sw['w_tail'], sw['w_rank'], p['gsum'], p['gspread'], p['qg'], p['kg'],
        p['wa'], p['ba'], rope_tabs if latent else None, BF16 if latent else F32, n_tok // _s5_splits(bsz))
    sh = lambda a: a.reshape(bsz, n_tok, a.shape[-1])

    if latent:
        keys = [ctx['k'].reshape(bsz, -1, MIX_W), sh(k)]
        vals = [ctx['v'].reshape(bsz, -1, MIX_W), sh(v)]
    else:
        keys, vals = [sh(k)], [sh(v)]
    oa = _diff_attention(sh(q), keys, vals, p['lamp'], p['subg'], lam_init)

    of, ob, new_gla = _gla(sh(gqk), sh(gv), sh(la), ctx['gla'] if latent else None)

    h0 = ctx['s5'].transpose(3, 0, 2, 1, 4).reshape(S5_G, bsz, 4 * S5_P) if latent else None
    y, hfin = _s5(cu, p['toep'], p['wst'], p['cst'], p['apow'], h0, bsz)

    x_out = _mix_ffn(l, x2, mod3, oa.reshape(bsz * n_tok, MIX_W), of.reshape(bsz * n_tok, MIX_W),
                     ob.reshape(bsz * n_tok, MIX_W), gg, y, cu, gates, p['ong'], p['s5d'], sw['wglu'], p['bglu'],
                     sw['wbr'], sw['wout'], p['n2g'], sw['wg'], sw['wu'], sw['wd']).reshape(bsz, n_tok, D_MODEL)
    if latent:
        return x_out, None
    new_k = sh(k).reshape(bsz, n_tok, A_HEADS, 2, A_HEAD_DIM)
    new_v = sh(v).reshape(bsz, n_tok, A_HEADS, 2 * A_HEAD_DIM)
    hf = hfin.reshape(S5_G, bsz, 2, 2, S5_P)
    new_s5 = hf.transpose(1, 3, 2, 0, 4)
    return x_out, (new_k, new_v, new_gla, new_s5)


def kernel(x_prompt, x_sample, cache_diff_k, cache_diff_v, state_gla, state_s5, c, c_ctx, w_mod, b_mod, norm1_g, norm2_g, w_in, diff_qn_g, diff_kn_g, diff_lam, diff_subln_g, gla_wa2, gla_ba, gla_on_g, s5_lam_re, s5_lam_im, s5_log_dt, s5_b_re, s5_b_im, s5_c_re, s5_c_im, s5_d, s5_w_glu, s5_b_glu, w_branch, w_out, w_ffn_gate, w_ffn_up, w_ffn_down):
    weights = dict(norm1_g=norm1_g, norm2_g=norm2_g, w_in=w_in, diff_qn_g=diff_qn_g, diff_kn_g=diff_kn_g,
                   diff_lam=diff_lam, diff_subln_g=diff_subln_g, gla_wa2=gla_wa2, gla_ba=gla_ba,
                   gla_on_g=gla_on_g, s5_lam_re=s5_lam_re, s5_lam_im=s5_lam_im, s5_log_dt=s5_log_dt,
                   s5_b_re=s5_b_re, s5_b_im=s5_b_im, s5_c_re=s5_c_re, s5_c_im=s5_c_im, s5_d=s5_d,
                   s5_w_glu=s5_w_glu, s5_b_glu=s5_b_glu, w_branch=w_branch, w_out=w_out,
                   w_ffn_gate=w_ffn_gate, w_ffn_up=w_ffn_up, w_ffn_down=w_ffn_down)
    depth = w_mod.shape[0]
    dec_b = c.shape[0]
    cond8 = jnp.concatenate([c_ctx[None, :], c, jnp.zeros((SUBLANES - 1 - dec_b, D_MODEL), F32)], axis=0)
    mod = _modulation(cond8, w_mod, b_mod)
    rope_tabs = _rope_tables(x_sample.shape[1])
    sw = _prepare_shared(weights)
    y_prompt, y_sample = x_prompt, x_sample
    k_list, v_list, gla_list, s5_list = [], [], [], []
    for l in range(depth):
        p = _prepare_layer(l, weights)
        lam_init = 0.8 - 0.6 * math.exp(-0.3 * l)
        y_prompt, (k_l, v_l, g_l, s_l) = _layer(l, y_prompt, mod[l, 0:1][:, None, :], p, sw, lam_init, None, None)
        k_list.append(k_l)
        v_list.append(v_l)
        gla_list.append(g_l)
        s5_list.append(s_l)
        ctx = dict(k=cache_diff_k[:, l], v=cache_diff_v[:, l], gla=state_gla[:, l], s5=state_s5[:, l])
        y_sample, _ = _layer(l, y_sample, mod[l, 1:1 + dec_b][:, None, :], p, sw, lam_init, ctx, rope_tabs)
    return (y_prompt, y_sample, jnp.stack(k_list, axis=1), jnp.stack(v_list, axis=1),
            jnp.stack(gla_list, axis=1), jnp.stack(s5_list, axis=1))
```

```python
import functools
import math

import numpy as np
import jax
import jax.numpy as jnp
from jax import lax
from jax.experimental import pallas as pl
from jax.experimental.pallas import tpu as pltpu

F32 = jnp.float32
BF16 = jnp.bfloat16

D_MODEL = 1024
MIX_W = 512
A_HEADS = 4
A_HEAD_DIM = 64
GRID_W = 64
ROPE_THETA = 10000.0
B_HEADS = 4
B_KDIM = 64
B_VDIM = 128
GLA_RANK = 16
GLA_TAU = 16.0
S5_G = 32
S5_GROUP = 16
S5_P = 64
FFN_DIM = 2816
EPS = 1e-6

VMEM_LIMIT_BYTES = 56 * 1024 * 1024
INPROJ_TILE = 512
INPROJ_SUB_TILE = 256
DENSE_TILE = 256
ATTN_Q_TILE = 1024
ATTN_SUB_TILE = 256
GLA_CHUNK = 128
GLA_STEP_CHUNKS = 4
GLA_FIRST_DIFF_LEVEL = 3
S5_CHUNK = 16
SUBLANES = 8
LANES = 128
S5_TILE_GROUPS = LANES // 16
S5_ROW_BLOCK = 64
S5_ROW_PAD = 8

OFF_AQ, OFF_AK, OFF_AV, OFF_BQK, OFF_BV, OFF_BG, HEAD_COLS = 0, 512, 1024, 1536, 2048, 2560, 3072
TAIL_START = HEAD_COLS + 2 * GLA_RANK
TAIL_CU, TAIL_GZ, TAIL_COLS = 0, MIX_W, MIX_W + 3 * D_MODEL
BR_PAD = 128


def _cparams(*sem):
    return pltpu.CompilerParams(dimension_semantics=sem, vmem_limit_bytes=VMEM_LIMIT_BYTES)


def _resident(shape):
    nd = len(shape)
    return pl.BlockSpec(shape, lambda *_: (0,) * nd, pipeline_mode=pl.Buffered(1))


def _layer_block(arr, l, last=None):
    tail = arr.shape[1:] if last is None else arr.shape[1:-1] + (last,)
    return pl.BlockSpec((None,) + tail, lambda *_: (l,) + (0,) * len(tail), pipeline_mode=pl.Buffered(1))


def _split_bf16(x):
    hi = x.astype(BF16)
    lo = (x - hi.astype(F32)).astype(BF16)
    return hi, lo


def _dot(a, b):
    return jnp.dot(a, b, preferred_element_type=F32)


def _dot_nt(a, b):
    return lax.dot_general(a, b, (((1,), (1,)), ((), ())), preferred_element_type=F32)


def _dot_tn(a, b):
    return lax.dot_general(a, b, (((0,), (0,)), ((), ())), preferred_element_type=F32)


def _dot_f32(a, b, nt=False):
    d = _dot_nt if nt else _dot
    ah, al = _split_bf16(a)
    bh, bl = _split_bf16(b)
    return d(ah, bh) + d(ah, bl) + d(al, bh)


def _rms(x):
    return x * lax.rsqrt(jnp.mean(x * x, axis=-1, keepdims=True) + EPS)


def _mod_kernel(cond_ref, w_ref, b_ref, o_ref):
    cnd = cond_ref[...]
    s = (cnd * jax.nn.sigmoid(cnd))
    o_ref[...] = _dot_f32(s, w_ref[...]) + b_ref[...]


def _modulation(cond8, w_mod, b_mod):
    depth, d, n = w_mod.shape
    tn = 1536
    return pl.pallas_call(
        _mod_kernel,
        grid=(depth, n // tn),
        in_specs=[pl.BlockSpec((SUBLANES, d), lambda l, j: (0, 0)),
                  pl.BlockSpec((None, d, tn), lambda l, j: (l, 0, j)),
                  pl.BlockSpec((None, 1, tn), lambda l, j: (l, 0, j))],
        out_specs=pl.BlockSpec((None, SUBLANES, tn), lambda l, j: (l, 0, j)),
        out_shape=jax.ShapeDtypeStruct((depth, SUBLANES, n), F32),
        compiler_params=_cparams("parallel", "parallel"),
        name="modulation",
    )(cond8, w_mod, b_mod.reshape(depth, 1, n))


def _group_rms64_pair(za, zb, gsum, gspread, gain_a, gain_b):
    w = za.shape[-1]
    sq = jnp.concatenate([(za * za).astype(BF16), (zb * zb).astype(BF16)], axis=1)
    hi, lo = _split_bf16(_dot(sq, gsum))
    ms = _dot(jnp.concatenate([hi, lo], axis=1), gspread)
    return za * lax.rsqrt(ms[:, :w] + EPS) * gain_a, zb * lax.rsqrt(ms[:, w:] + EPS) * gain_b


def _rope(z, c, s):
    n = z.shape[-1]
    lane = lax.broadcasted_iota(jnp.int32, z.shape, 1)
    first = (lane & 31) < 16
    partner = jnp.where(first, pltpu.roll(z, n - 16, 1), pltpu.roll(z, 16, 1))
    return z * c + partner * s


def _inproj_kernel(rope, x_ref, mod_ref, n1_ref, w_head_ref, w_tail_ref, w_rank_ref, gsum_ref, gspread_ref,
                   qg_ref, kg_ref, wa_ref, ba_ref, *rest):
    if rope:
        cos_ref, sin_ref = rest[:2]
        rest = rest[2:]
    q_ref, k_ref, v_ref, gqk_ref, gv_ref, gg_ref, la_ref, cu_ref, gate_ref = rest[1:]
    mod = mod_ref[...]
    sh1 = mod[:, 0:D_MODEL]
    sc1 = mod[:, D_MODEL:2 * D_MODEL]
    gsum = gsum_ref[...]
    gspread = gspread_ref[...]
    wa_hi, wa_lo = _split_bf16(wa_ref[...])
    wcopy = lax.broadcasted_iota(jnp.int32, wa_hi.shape, 0) // (2 * GLA_RANK)
    wa3 = jnp.where(wcopy == 1, wa_lo, wa_hi)
    sub = math.gcd(INPROJ_SUB_TILE, x_ref.shape[0])

    for rows in [slice(j * sub, (j + 1) * sub) for j in range(x_ref.shape[0] // sub)]:
        h = _rms(x_ref[rows, :]) * n1_ref[...] * (1.0 + sc1) + sh1
        hb = h.astype(BF16)

        def seg(a, b, w_ref=w_head_ref):
            return _dot(hb, w_ref[:, a:b])

        r = _dot(hb, w_rank_ref[...])
        r_hi, r_lo = _split_bf16(r)
        copy = lax.broadcasted_iota(jnp.int32, r.shape, 1) // (2 * GLA_RANK)
        pre = _dot(jnp.where(copy < 2, r_hi, r_lo), wa3) + ba_ref[...]
        la_ref[rows, :] = (jnp.minimum(pre, 0.0) - jnp.log1p(jnp.exp(-jnp.abs(pre)))) * (1.0 / GLA_TAU)

        q, k = _group_rms64_pair(seg(OFF_AQ, OFF_AK), seg(OFF_AK, OFF_AV), gsum, gspread, qg_ref[...], kg_ref[...])
        if rope:
            c = cos_ref[rows, :]
            s = sin_ref[rows, :]
            q = _rope(q, c, s)
            k = _rope(k, c, s)
        q_ref[rows, :] = (q * (A_HEAD_DIM ** -0.5)).astype(q_ref.dtype)
        k_ref[rows, :] = k.astype(k_ref.dtype)

        gate_ref[rows, :] = jax.nn.sigmoid(seg(TAIL_GZ, TAIL_COLS, w_tail_ref)).astype(gate_ref.dtype)
        bg = seg(OFF_BG, HEAD_COLS)
        gg_ref[rows, :] = (bg * jax.nn.sigmoid(bg)).astype(gg_ref.dtype)
        bqk = seg(OFF_BQK, OFF_BV)
        lane = lax.broadcasted_iota(jnp.int32, bqk.shape, 1)
        gqk_ref[rows, :] = jnp.where(lane < B_HEADS * B_KDIM, bqk * (B_KDIM ** -0.5), bqk).astype(gqk_ref.dtype)
        v_ref[rows, :] = seg(OFF_AV, OFF_BQK).astype(v_ref.dtype)
        gv_ref[rows, :] = seg(OFF_BV, OFF_BG).astype(gv_ref.dtype)
        cu = seg(TAIL_CU, TAIL_GZ, w_tail_ref).astype(cu_ref.dtype)
        if len(cu_ref.shape) == 2:
            cu_ref[rows, :] = cu
        else:
            seq = cu_ref.shape[1]
            for s in range(sub // seq):
                cu_ref[(rows.start + s * seq) // seq] = cu[s * seq:(s + 1) * seq]


def _token_tiling(n_tok, mod3, want):
    tm = math.gcd(want, n_tok // mod3.shape[0])
    tiles_per_mod = (n_tok // tm) // mod3.shape[0]
    return tm, pl.BlockSpec((None, 1, mod3.shape[-1]), lambda i: (i // tiles_per_mod, 0, 0))


def _inproj(l, x2, mod3, n1g, w_in, w_tail, w_rank, gsum, gspread, qg, kg, wa, ba, rope_tabs, kv_dtype, s5_seq):
    n_tok = x2.shape[0]
    tm, mod_spec = _token_tiling(n_tok, mod3, INPROJ_TILE)
    rope = rope_tabs is not None
    tok = lambda w: pl.BlockSpec((tm, w), lambda i: (i, 0))
    in_specs = [tok(D_MODEL), mod_spec, _resident(n1g.shape),
                _layer_block(w_in, l, HEAD_COLS), _layer_block(w_tail, l), _layer_block(w_rank, l),
                _resident(gsum.shape), _resident(gspread.shape),
                _resident(qg.shape), _resident(kg.shape), _resident(wa.shape), _resident(ba.shape)]
    args = [x2, mod3, n1g, w_in, w_tail, w_rank, gsum, gspread, qg, kg, wa, ba]
    if rope:
        tiles_per_seq = rope_tabs[0].shape[0] // tm
        for t in rope_tabs:
            in_specs.append(pl.BlockSpec((tm, MIX_W), lambda i: (i % tiles_per_seq, 0)))
            args.append(t)
    widths = [(MIX_W, BF16), (MIX_W, kv_dtype), (MIX_W, kv_dtype), (MIX_W, BF16), (MIX_W, BF16),
              (MIX_W, BF16), (MIX_W, F32), (MIX_W, F32), (3 * D_MODEL, BF16)]
    out_specs = [tok(w) for w, _ in widths]
    out_shape = [jax.ShapeDtypeStruct((n_tok, w), dt) for w, dt in widths]
    cu_out = 7
    cu_shape = (n_tok // s5_seq, s5_seq + S5_ROW_PAD, MIX_W)
    if tm <= s5_seq:
        per_seq = s5_seq // tm
        out_specs[cu_out] = pl.BlockSpec((None, tm, MIX_W), lambda i: (i // per_seq, i % per_seq, 0))
    else:
        out_specs[cu_out] = pl.BlockSpec((tm // s5_seq, s5_seq, MIX_W), lambda i: (i, 0, 0))
    out_shape[cu_out] = jax.ShapeDtypeStruct(cu_shape, F32)
    in_specs.append(pl.BlockSpec(memory_space=pl.ANY))
    args.append(jnp.zeros(cu_shape, F32))
    return pl.pallas_call(
        functools.partial(_inproj_kernel, rope),
        grid=(n_tok // tm,),
        in_specs=in_specs,
        out_specs=out_specs,
        out_shape=out_shape,
        input_output_aliases={len(args) - 1: cu_out},
        compiler_params=_cparams("parallel"),
        name="inproj",
    )(*args)


def _attn_kernel(lam_init, n_seg, q_ref, *refs):
    k_refs = refs[:n_seg]
    v_refs = refs[n_seg:2 * n_seg]
    lamp_ref, subg_ref, o_ref = refs[2 * n_seg:]
    lv = lamp_ref[...]
    lam = (jnp.exp(jnp.sum(lv[0:1] * lv[1:2], axis=-1, keepdims=True))
           - jnp.exp(jnp.sum(lv[2:3] * lv[3:4], axis=-1, keepdims=True)) + lam_init)
    hd = 2 * A_HEAD_DIM
    n_batch, n_rows = q_ref.shape[0], q_ref.shape[1]
    tq = math.gcd(ATTN_SUB_TILE, n_rows)
    for bi, sub, h in [(a, b, c) for a in range(n_batch) for b in range(n_rows // tq) for c in range(A_HEADS)]:
        rows = slice(sub * tq, (sub + 1) * tq)
        sl = slice(h * hd, (h + 1) * hd)
        qh = q_ref[bi, rows, sl]
        first = lax.broadcasted_iota(jnp.int32, qh.shape, 1) < A_HEAD_DIM
        zero = jnp.zeros_like(qh)
        q2 = jnp.concatenate([jnp.where(first, qh, zero), jnp.where(first, zero, qh)], axis=0)
        scores = [_dot_nt(q2, k_ref[bi, :, sl].astype(BF16)) for k_ref in k_refs]
        m = functools.reduce(jnp.maximum, [jnp.max(s, axis=-1, keepdims=True) for s in scores])
        acc = None
        for s, v_ref in zip(scores, v_refs):
            e = jnp.exp((s - m).astype(BF16))
            v_ext = jnp.concatenate([v_ref[bi, :, sl].astype(BF16), jnp.ones((v_ref.shape[1], hd), BF16)], axis=1)
            pv = _dot(e, v_ext)
            acc = pv if acc is None else acc + pv
        o2 = acc[:, :hd] / acc[:, hd:]
        o = o2[:tq] - lam * o2[tq:]
        o_ref[bi, rows, sl] = (_rms(o) * subg_ref[:, sl] * (1.0 - lam_init)).astype(o_ref.dtype)


def _diff_attention(q, ks, vs, lamp, subg, lam_init):
    bsz, lq, w = q.shape
    tq = math.gcd(ATTN_Q_TILE, lq)
    bb = math.gcd(max(1, ATTN_Q_TILE // lq), bsz)
    kv_spec = lambda a: pl.BlockSpec((bb, a.shape[1], w), lambda b, i: (b, 0, 0))
    return pl.pallas_call(
        functools.partial(_attn_kernel, lam_init, len(ks)),
        grid=(bsz // bb, lq // tq),
        in_specs=[pl.BlockSpec((bb, tq, w), lambda b, i: (b, i, 0))]
                 + [kv_spec(a) for a in ks] + [kv_spec(a) for a in vs]
                 + [pl.BlockSpec(lamp.shape, lambda b, i: (0, 0)), pl.BlockSpec(subg.shape, lambda b, i: (0, 0))],
        out_specs=pl.BlockSpec((bb, tq, w), lambda b, i: (b, i, 0)),
        out_shape=jax.ShapeDtypeStruct((bsz, lq, w), BF16),
        compiler_params=_cparams("parallel", "parallel"),
        name="diff_attention",
    )(q, *ks, *vs, lamp, subg)


def _gla_masks(chunk):
    nlev = int(math.log2(chunk))
    assert 1 << nlev == chunk
    t = np.arange(chunk)[:, None]
    r = np.arange(chunk)[None, :]
    cum, pair = [], []
    for j in range(nlev + 1):
        start = (t >> j) << j
        end = start + (1 << j) - 1
        if 0 < j < GLA_FIRST_DIFF_LEVEL:
            cum.append((r >= start) & (r <= t))
            cum.append((r > t) & (r <= end))
        if j < nlev:
            pair.append(((t >> (j + 1)) == (r >> (j + 1))) & (((t >> j) & 1) == 1) & (((r >> j) & 1) == 0))
    cum.append(r <= t)
    pair.append(t == r)
    dup = lambda m: np.concatenate([m, m], axis=1)
    cum_f = dup(np.concatenate(cum, 0).astype(np.float32))
    pair_f = np.stack(pair, 0).astype(np.float32)
    cum_b = dup(np.concatenate([m[::-1, ::-1] for m in cum], 0).astype(np.float32))
    pair_b = pair_f[:, ::-1, ::-1]
    return (jnp.asarray(np.stack([cum_f, cum_b]), BF16), jnp.asarray(np.stack([pair_f, pair_b]), F32), nlev)


def _gla_chunk(qk, v, la, cum, pair_ref, d, st_ref, run_ref, bd, nlev, last_row):
    c = qk.shape[0]
    kw = B_HEADS * B_KDIM
    q = qk[:, :kw].astype(F32)
    k = qk[:, kw:].astype(F32)
    hi, lo = _split_bf16(la)
    sums = _dot(cum, jnp.concatenate([hi, lo], axis=0))
    n_mxu = GLA_FIRST_DIFF_LEVEL - 1
    run = sums[2 * n_mxu * c:]
    run_ref[...] = run

    def run_at(size, row_of_block):
        pieces = []
        for kb in range(c // size):
            r = row_of_block(kb)
            pieces.append(jnp.broadcast_to(run_ref[r:r + 1, :], (size, kw)) if 0 <= r < c
                          else jnp.zeros((size, kw), F32))
        return pieces[0] if len(pieces) == 1 else jnp.concatenate(pieces, axis=0)

    def factors(j):
        if j < GLA_FIRST_DIFF_LEVEL:
            return jnp.exp(sums[(2 * j - 2) * c:(2 * j - 1) * c]), jnp.exp(sums[(2 * j - 1) * c:(2 * j) * c])
        size = 1 << j
        if d == 0:
            before = run_at(size, lambda kb: kb * size - 1)
            last = run_at(size, lambda kb: (kb + 1) * size - 1)
            return jnp.exp(run - before), jnp.exp(last - run)
        after = run_at(size, lambda kb: (kb + 1) * size)
        first = run_at(size, lambda kb: kb * size)
        return jnp.exp(run - after), jnp.exp(first - run)

    lane = lax.broadcasted_iota(jnp.int32, (c, kw), 1)
    head_masks = [(lane >= h * B_KDIM) & (lane < (h + 1) * B_KDIM) for h in range(B_HEADS)]
    zero = jnp.zeros((c, kw), BF16)

    def scores(qf, kf, pm):
        qb = qf.astype(BF16)
        kb = kf.astype(BF16)
        stacked = _dot_nt(jnp.concatenate([jnp.where(m, qb, zero) for m in head_masks], axis=0), kb)
        return [pm * stacked[h * c:(h + 1) * c] for h in range(B_HEADS)]

    att = scores(q, k, pair_ref[d, nlev])
    for j in range(nlev):
        if j == 0:
            lev = scores(q * jnp.exp(la), k, pair_ref[d, 0])
        else:
            eq, ek = factors(j)
            lev = scores(q * eq, k * ek, pair_ref[d, j])
        att = [a + b for a, b in zip(att, lev)]
    eq, ek = factors(nlev)
    st = st_ref[...]
    o = _dot_nt((q * eq).astype(BF16), st.astype(BF16))
    outs = []
    for h in range(B_HEADS):
        sl = slice(h * B_VDIM, (h + 1) * B_VDIM)
        outs.append(o[:, sl] + _dot(att[h].astype(BF16), v[:, sl]))
    dec = eq[last_row:last_row + 1, :]
    st_ref[...] = st * dec + bd * _dot_tn(v, (k * ek).astype(BF16))
    return outs


def _gla_kernel(has_s0, nlev, *refs):
    if has_s0:
        s0_ref, refs = refs[0], refs[1:]
    (qkf_ref, vf_ref, laf_ref, qkb_ref, vb_ref, lab_ref, cum_ref, pair_ref, bd_ref,
     of_ref, ob_ref, sfin_ref, st_f, st_b, run_f, run_b) = refs
    i = pl.program_id(1)
    c = GLA_CHUNK
    n_sub = qkf_ref.shape[0] // c

    @pl.when(i == 0)
    def _():
        for d, st in enumerate((st_f, st_b)):
            if has_s0:
                zero = jnp.zeros((B_KDIM, B_VDIM), F32)
                full = jnp.concatenate(
                    [jnp.concatenate([s0_ref[d, h] if hh == h else zero for hh in range(B_HEADS)], axis=1)
                     for h in range(B_HEADS)], axis=0)
                st[...] = full.T
            else:
                st[...] = jnp.zeros_like(st)

    bd = bd_ref[...]
    for sub in range(n_sub):
        rf = slice(sub * c, (sub + 1) * c)
        outs = _gla_chunk(qkf_ref[rf, :], vf_ref[rf, :], laf_ref[rf, :], cum_ref[0], pair_ref, 0, st_f, run_f, bd,
                          nlev, c - 1)
        for h, o in enumerate(outs):
            of_ref[rf, h * B_VDIM:(h + 1) * B_VDIM] = o.astype(of_ref.dtype)
        rb = slice((n_sub - 1 - sub) * c, (n_sub - sub) * c)
        outs = _gla_chunk(qkb_ref[rb, :], vb_ref[rb, :], lab_ref[rb, :], cum_ref[1], pair_ref, 1, st_b, run_b, bd,
                          nlev, 0)
        for h, o in enumerate(outs):
            ob_ref[rb, h * B_VDIM:(h + 1) * B_VDIM] = o.astype(ob_ref.dtype)

    @pl.when(i == pl.num_programs(1) - 1)
    def _():
        for d, st in enumerate((st_f, st_b)):
            full = st[...].T
            for h in range(B_HEADS):
                sfin_ref[d, h] = full[h * B_KDIM:(h + 1) * B_KDIM, h * B_VDIM:(h + 1) * B_VDIM]


def _gla(gqk, gv, la, s0):
    bsz, n_tok, _ = gqk.shape
    c = math.gcd(GLA_STEP_CHUNKS * GLA_CHUNK, n_tok)
    n = n_tok // c
    cum, pair, nlev = _gla_masks(GLA_CHUNK)
    kw = B_HEADS * B_KDIM
    vw = B_HEADS * B_VDIM
    rows = np.arange(vw)[:, None] // B_VDIM
    cols = np.arange(kw)[None, :] // B_KDIM
    bd = jnp.asarray((rows == cols).astype(np.float32))
    fwd = lambda w, off=0: pl.BlockSpec((None, c, w), lambda b, i: (b, i, off))
    bwd = lambda w, off=0: pl.BlockSpec((None, c, w), lambda b, i: (b, n - 1 - i, off))
    in_specs = [fwd(2 * kw), fwd(vw), fwd(kw, 0), bwd(2 * kw), bwd(vw), bwd(kw, 1),
                _resident(cum.shape), _resident(pair.shape), _resident(bd.shape)]
    args = [gqk, gv, la, gqk, gv, la, cum, pair, bd]
    state_spec = pl.BlockSpec((None, 2, B_HEADS, B_KDIM, B_VDIM), lambda b, i: (b, 0, 0, 0, 0))
    if s0 is not None:
        in_specs.insert(0, state_spec)
        args.insert(0, s0)
    return pl.pallas_call(
        functools.partial(_gla_kernel, s0 is not None, nlev),
        grid=(bsz, n),
        in_specs=in_specs,
        out_specs=[fwd(vw), bwd(vw), state_spec],
        out_shape=[jax.ShapeDtypeStruct((bsz, n_tok, vw), BF16), jax.ShapeDtypeStruct((bsz, n_tok, vw), BF16),
                   jax.ShapeDtypeStruct((bsz, 2, B_HEADS, B_KDIM, B_VDIM), F32)],
        scratch_shapes=[pltpu.VMEM((vw, kw), F32), pltpu.VMEM((vw, kw), F32),
                        pltpu.VMEM((GLA_CHUNK, kw), F32), pltpu.VMEM((GLA_CHUNK, kw), F32)],
        compiler_params=_cparams("parallel", "arbitrary"),
        name="gla",
    )(*args)


def _s5_prep_kernel(lr_ref, li_ref, dt_ref, bre_ref, bim_ref, cre_ref, cim_ref,
                    toep_ref, wst_ref, cst_ref, apow_ref):
    t_len = S5_CHUNK
    grp = S5_GROUP
    rows = t_len * grp
    lr = lr_ref[...]
    li = li_ref[...]
    dt = jnp.exp(dt_ref[...])
    a = lr * dt
    th = li * dt
    mag = jnp.exp(a)
    ar = mag * jnp.cos(th)
    ai = mag * jnp.sin(th)
    den = lr * lr + li * li
    fr = ((ar - 1.0) * lr + ai * li) / den
    fi = (ai * lr - (ar - 1.0) * li) / den
    b_re = bre_ref[...]
    b_im = bim_ref[...]
    bbr = jnp.concatenate([fr * b_re - fi * b_im] * t_len, axis=0)
    bbi = jnp.concatenate([fr * b_im + fi * b_re] * t_len, axis=0)
    c_re = cre_ref[...]
    c_im = cim_ref[...]
    c_re_t = jnp.concatenate([c_re] * t_len, axis=0)
    c_im_t = jnp.concatenate([c_im] * t_len, axis=0)

    n_pow = -(-(t_len + 1) // SUBLANES) * SUBLANES
    ex = lax.broadcasted_iota(jnp.int32, (n_pow, LANES), 0).astype(F32)
    pmag = jnp.exp(ex * a)
    pow_re = pmag * jnp.cos(ex * th)
    pow_im = pmag * jnp.sin(ex * th)
    is_fwd = lax.broadcasted_iota(jnp.int32, (grp, LANES), 1) < S5_P

    def expand(tab, exp_fwd, exp_bwd):
        blocks = []
        for step in range(t_len):
            f = jnp.broadcast_to(tab[exp_fwd(step):exp_fwd(step) + 1, :], (grp, LANES))
            b = jnp.broadcast_to(tab[exp_bwd(step):exp_bwd(step) + 1, :], (grp, LANES))
            blocks.append(jnp.where(is_fwd, f, b))
        return jnp.concatenate(blocks, axis=0)

    pr = expand(pow_re, lambda s: t_len - 1 - s, lambda s: s)
    pi = expand(pow_im, lambda s: t_len - 1 - s, lambda s: s)
    xr = bbr * pr - bbi * pi
    xi = bbr * pi + bbi * pr
    wst_ref[...] = jnp.concatenate([xr, xi], axis=1).astype(wst_ref.dtype)

    fwd_rows = lax.broadcasted_iota(jnp.int32, (rows, LANES), 1) < S5_P
    c_re_rep = jnp.concatenate([c_re] * (LANES // grp), axis=0)
    c_im_rep = jnp.concatenate([c_im] * (LANES // grp), axis=0)
    zero = jnp.zeros_like(xr)

    def lag_kernel(keep):
        return (_dot_f32(jnp.where(keep, xr, zero), c_re_rep, nt=True)
                - _dot_f32(jnp.where(keep, xi, zero), c_im_rep, nt=True))

    ker_f = lag_kernel(fwd_rows)
    ker_b = lag_kernel(jnp.logical_not(fwd_rows))
    blk = lax.broadcasted_iota(jnp.int32, (rows, LANES), 1) // grp

    def column_block(t):
        up = (t_len - 1 - t) * grp
        down = t * grp
        f = ker_f[up:] if up == 0 else jnp.concatenate([ker_f[up:], jnp.zeros((up, LANES), F32)], axis=0)
        b = ker_b if down == 0 else jnp.concatenate([jnp.zeros((down, LANES), F32), ker_b[:rows - down]], axis=0)
        return f + b

    per_tile = LANES // grp
    for h in range(t_len // per_tile):
        tile = column_block(h * per_tile)
        for j in range(1, per_tile):
            tile = jnp.where(blk == j, column_block(h * per_tile + j), tile)
        toep_ref[:, h * LANES:(h + 1) * LANES] = tile.astype(toep_ref.dtype)

    pr = expand(pow_re, lambda t: t + 1, lambda t: t_len - t)
    pi = expand(pow_im, lambda t: t + 1, lambda t: t_len - t)
    cr = c_re_t * pr - c_im_t * pi
    ci = -(c_re_t * pi + c_im_t * pr)
    cst_ref[...] = jnp.concatenate([jnp.where(fwd_rows, cr, zero), jnp.where(fwd_rows, ci, zero),
                                    jnp.where(fwd_rows, zero, cr), jnp.where(fwd_rows, zero, ci)],
                                   axis=1).astype(cst_ref.dtype)
    apow_ref[...] = jnp.concatenate([pow_re[t_len:t_len + 1], pow_im[t_len:t_len + 1],
                                     jnp.zeros((SUBLANES - 2, LANES), F32)], axis=0)


def _s5_prep(lr, li, ldt, bre_t, bim_t, cre, cim):
    g = lr.shape[0]
    rows = S5_CHUNK * S5_GROUP
    grp = lambda *s: pl.BlockSpec((None,) + s, lambda i: (i,) + (0,) * len(s))
    return pl.pallas_call(
        _s5_prep_kernel,
        grid=(g,),
        in_specs=[grp(1, LANES)] * 3 + [grp(S5_GROUP, LANES)] * 4,
        out_specs=[grp(rows, rows), grp(rows, 4 * S5_P), grp(rows, 8 * S5_P), grp(SUBLANES, LANES)],
        out_shape=[jax.ShapeDtypeStruct((g, rows, rows), BF16),
                   jax.ShapeDtypeStruct((g, rows, 4 * S5_P), BF16),
                   jax.ShapeDtypeStruct((g, rows, 8 * S5_P), BF16),
                   jax.ShapeDtypeStruct((g, SUBLANES, LANES), F32)],
        compiler_params=_cparams("parallel"),
        name="s5_prep",
    )(lr, li, ldt, bre_t, bim_t, cre, cim)


def _block_transpose(xs, blk):
    n = len(xs)
    d = n // 2
    while d >= 1:
        low = (blk & d) == 0
        new = list(xs)
        for i in range(n):
            if i & d == 0:
                a, b = xs[i], xs[i | d]
                new[i] = jnp.where(low, a, pltpu.roll(b, d * S5_GROUP, 1))
                new[i | d] = jnp.where(low, pltpu.roll(a, LANES - d * S5_GROUP, 1), b)
        xs = new
        d //= 2
    return xs


def _s5_kernel(n_chunks, bp, splits, cu_ref, toep_ref, wst_ref, cst_ref, apow_ref, h0_ref, y_ref, hfin_ref,
               u_scr, s_scr, h_scr, yg_scr):
    t_len = S5_CHUNK
    ng = S5_TILE_GROUPS
    seq = n_chunks * t_len
    pitch = seq + S5_ROW_PAD
    rows = bp * n_chunks
    rb = max(bp, math.gcd(S5_ROW_BLOCK, rows))
    cpb = rb // bp
    blk = lax.broadcasted_iota(jnp.int32, (rb, LANES), 1) // S5_GROUP

    def relayout_in(r, carry):
        r0 = pl.multiple_of(r * rb, rb)
        for h in range(t_len // ng):
            xs = []
            for b in range(ng):
                pieces = [cu_ref[pl.ds((r * cpb + j) * t_len + ng * h + b, bp, stride=pitch), :] for j in range(cpb)]
                xs.append(pieces[0] if cpb == 1 else jnp.concatenate(pieces, axis=0))
            for g, tile in enumerate(_block_transpose(xs, blk)):
                u_scr[g, pl.ds(r0, rb), h * LANES:(h + 1) * LANES] = tile.astype(BF16)
        return carry

    lax.fori_loop(0, rows // rb, relayout_in, 0)

    half = 2 * S5_P
    for g in range(ng):
        s_scr[g] = _dot(u_scr[g], wst_ref[g])

    is_fwd = lax.broadcasted_iota(jnp.int32, (bp, half), 1) < S5_P

    def scan(init, store):
        def step(kk, hs):
            f0 = pl.multiple_of(kk * bp, bp)
            b0 = pl.multiple_of((n_chunks - 1 - kk) * bp, bp)
            new = []
            for g in range(ng):
                re, im = hs[g]
                if store:
                    h_scr[g, pl.ds(f0, bp), 0:half] = re
                    h_scr[g, pl.ds(f0, bp), half:2 * half] = im
                    h_scr[g, pl.ds(b0, bp), 2 * half:3 * half] = re
                    h_scr[g, pl.ds(b0, bp), 3 * half:4 * half] = im
                s_f = s_scr[g, pl.ds(f0, bp), :]
                s_b = s_scr[g, pl.ds(b0, bp), :]
                a_re = apow_ref[g, 0:1, :]
                a_im = apow_ref[g, 1:2, :]
                new.append((a_re * re - a_im * im + jnp.where(is_fwd, s_f[:, :half], s_b[:, :half]),
                            a_re * im + a_im * re + jnp.where(is_fwd, s_f[:, half:], s_b[:, half:])))
            return tuple(new)

        return lax.fori_loop(0, n_chunks, step, init)

    h0 = tuple((h0_ref[g, :, :half], h0_ref[g, :, half:]) for g in range(ng))
    init = h0
    if splits > 1:
        seg = lax.broadcasted_iota(jnp.int32, (bp, half), 0) & (splits - 1)
        take_prev = is_fwd & (seg >= 1)
        take_next = jnp.logical_not(is_fwd) & (seg <= splits - 2)
        for _ in range(splits - 1):
            fin = scan(init, store=False)
            init = tuple(tuple(part0 + jnp.where(take_prev, pltpu.roll(part, 1, 0),
                                                 jnp.where(take_next, pltpu.roll(part, bp - 1, 0), 0.0))
                               for part0, part in zip(h0[g], fin[g])) for g in range(ng))
    hs = scan(init, store=True)
    for g in range(ng):
        hfin_ref[g, :, :half] = hs[g][0]
        hfin_ref[g, :, half:] = hs[g][1]
        yg_scr[g] = _dot(u_scr[g], toep_ref[g]) + _dot_nt(h_scr[g].astype(BF16), cst_ref[g])

    def relayout_out(r, carry):
        r0 = pl.multiple_of(r * rb, rb)
        for h in range(t_len // ng):
            ys = [yg_scr[g, pl.ds(r0, rb), h * LANES:(h + 1) * LANES] for g in range(ng)]
            for b, tile in enumerate(_block_transpose(ys, blk)):
                for j in range(cpb):
                    y_ref[pl.ds((r * cpb + j) * t_len + ng * h + b, bp, stride=pitch), :] = tile[j * bp:(j + 1) * bp]
        return carry

    lax.fori_loop(0, rows // rb, relayout_out, 0)
    for pb in range(bp):
        y_ref[pb * pitch + seq:(pb + 1) * pitch, :] = jnp.zeros((S5_ROW_PAD, LANES), F32)


def _s5_splits(bsz):
    return SUBLANES // bsz if SUBLANES % bsz == 0 else 1


def _s5(cu, toep, wst, cst, apow, h0, bsz):
    bp, pitch, width = cu.shape
    splits = _s5_splits(bsz)
    assert bp == bsz * splits
    n_chunks = (pitch - S5_ROW_PAD) // S5_CHUNK
    n_rows = bp * pitch
    cu = cu.reshape(n_rows, width)
    rows = bp * n_chunks
    ng = S5_TILE_GROUPS
    w = S5_CHUNK * S5_GROUP
    p2 = 2 * S5_P
    if h0 is None:
        h0p = jnp.zeros((S5_G, bp, 2 * p2), F32)
    elif splits == 1:
        h0p = h0
    else:
        h04 = h0.reshape(S5_G, bsz, 1, 2, 2, S5_P)
        zero = jnp.zeros((S5_G, bsz, splits - 1, 2, 1, S5_P), F32)
        h0p = jnp.concatenate([jnp.concatenate([h04[..., 0:1, :], zero], axis=2),
                               jnp.concatenate([zero, h04[..., 1:2, :]], axis=2)], axis=4)
        h0p = h0p.reshape(S5_G, bp, 2 * p2)
    tile = lambda *s: pl.BlockSpec((ng,) + s, lambda j: (j,) + (0,) * len(s))
    lanes = lambda r: pl.BlockSpec((r, LANES), lambda j: (0, j))
    out_rows = bp * (n_chunks * S5_CHUNK + S5_ROW_PAD)
    y, hfin = pl.pallas_call(
        functools.partial(_s5_kernel, n_chunks, bp, splits),
        grid=(width // LANES,),
        in_specs=[lanes(n_rows), tile(w, w), tile(w, 2 * p2), tile(w, 4 * p2), tile(SUBLANES, p2),
                  tile(bp, 2 * p2)],
        out_specs=[lanes(out_rows), tile(bp, 2 * p2)],
        out_shape=[jax.ShapeDtypeStruct((out_rows, width), F32), jax.ShapeDtypeStruct((S5_G, bp, 2 * p2), F32)],
        scratch_shapes=[pltpu.VMEM((ng, rows, w), BF16), pltpu.VMEM((ng, rows, 2 * p2), F32),
                        pltpu.VMEM((ng, rows, 4 * p2), F32), pltpu.VMEM((ng, rows, w), F32)],
        compiler_params=_cparams("parallel"),
        name="s5_scan",
    )(cu, toep, wst, cst, apow, h0p)
    hf = hfin.reshape(S5_G, bsz, splits, 2, 2, S5_P)
    hfin = jnp.stack([hf[:, :, splits - 1, :, 0], hf[:, :, 0, :, 1]], axis=3)
    return y.reshape(bp, out_rows // bp, width), hfin.reshape(S5_G, bsz, 2 * p2)


def _mix_ffn_kernel(x_ref, mod_ref, oa_ref, of_ref, ob_ref, gg_ref, y_ref, cu_ref, gate_ref,
                    ong_ref, s5d_ref, wglu_ref, bglu_ref, wbr_ref, wout_ref, n2_ref, wg_ref, wu_ref, wd_ref, o_ref):
    mod = mod_ref[...]
    g1 = mod[:, 2 * D_MODEL:3 * D_MODEL]
    o_gla = of_ref[...].astype(F32) + ob_ref[...].astype(F32)
    parts = []
    for h in range(B_HEADS):
        sl = slice(h * B_VDIM, (h + 1) * B_VDIM)
        parts.append((_rms(o_gla[:, sl]) * ong_ref[...] * gg_ref[:, sl].astype(F32)).astype(BF16))
    ob = jnp.concatenate(parts, axis=1)

    cu = cu_ref[...].astype(F32)
    z = y_ref[...].astype(F32) + s5d_ref[...] * cu
    yc = 0.5 * z * (1.0 + jnp.tanh(math.sqrt(2.0 / math.pi) * (z + 0.044715 * (z * z * z))))
    glu = _dot(yc.astype(BF16), wglu_ref[...]) + bglu_ref[...]
    oc = (glu[:, :MIX_W] * jax.nn.sigmoid(glu[:, MIX_W:])).astype(BF16)

    merged = None
    for r, br in enumerate((oa_ref[...], ob, oc)):
        term = gate_ref[:, r * D_MODEL:(r + 1) * D_MODEL].astype(F32) * _dot(br, wbr_ref[r])
        merged = term if merged is None else merged + term
    x = x_ref[...] + g1 * _dot(merged.astype(BF16), wout_ref[...])

    sh2 = mod[:, 3 * D_MODEL:4 * D_MODEL]
    sc2 = mod[:, 4 * D_MODEL:5 * D_MODEL]
    g2 = mod[:, 5 * D_MODEL:6 * D_MODEL]
    hb = (_rms(x) * n2_ref[...] * (1.0 + sc2) + sh2).astype(BF16)
    gate = _dot(hb, wg_ref[...])
    act = (gate * jax.nn.sigmoid(gate) * _dot(hb, wu_ref[...])).astype(BF16)
    o_ref[...] = x + g2 * _dot(act, wd_ref[...])


def _mix_ffn(l, x2, mod3, oa, of, ob, gg, y, cu, gates, ong, s5d, wglu, bglu, wbr, wout, n2g, wg, wu, wd):
    n_tok = x2.shape[0]
    seq = n_tok // y.shape[0]
    tm, mod_spec = _token_tiling(n_tok, mod3, math.gcd(DENSE_TILE, seq))
    tiles_per_seq = seq // tm
    tok = lambda w: pl.BlockSpec((tm, w), lambda i: (i, 0))
    y_spec = pl.BlockSpec((None, tm, MIX_W), lambda i: (i // tiles_per_seq, i % tiles_per_seq, 0))
    return pl.pallas_call(
        _mix_ffn_kernel,
        grid=(n_tok // tm,),
        in_specs=[tok(D_MODEL), mod_spec,
                  tok(MIX_W), tok(MIX_W), tok(MIX_W), tok(MIX_W), y_spec, y_spec, tok(3 * D_MODEL),
                  _resident(ong.shape), _resident(s5d.shape), _layer_block(wglu, l), _resident(bglu.shape),
                  _layer_block(wbr, l), _layer_block(wout, l),
                  _resident(n2g.shape), _layer_block(wg, l), _layer_block(wu, l), _layer_block(wd, l)],
        out_specs=tok(D_MODEL),
        out_shape=jax.ShapeDtypeStruct((n_tok, D_MODEL), F32),
        compiler_params=_cparams("parallel"),
        name="mix_ffn",
    )(x2, mod3, oa, of, ob, gg, y, cu, gates, ong, s5d, wglu, bglu, wbr, wout, n2g, wg, wu, wd)


def _rope_tables(n_tok):
    t = np.arange(n_tok)
    row = (t // GRID_W).astype(np.float32)
    col = (t % GRID_W).astype(np.float32)
    half = A_HEAD_DIM // 2
    inv = jnp.asarray(ROPE_THETA, F32) ** (-jnp.arange(0, half, 2, dtype=F32) / half)
    ang_r = jnp.asarray(row)[:, None] * inv
    ang_c = jnp.asarray(col)[:, None] * inv
    cos = jnp.concatenate([jnp.cos(ang_r)] * 2 + [jnp.cos(ang_c)] * 2, axis=-1)
    sin = jnp.concatenate([-jnp.sin(ang_r), jnp.sin(ang_r), -jnp.sin(ang_c), jnp.sin(ang_c)], axis=-1)
    reps = MIX_W // A_HEAD_DIM
    return jnp.tile(cos, (1, reps)), jnp.tile(sin, (1, reps))


def _prepare_shared(w):
    w_in = w['w_in'].astype(BF16)
    w_rank = w_in[:, :, HEAD_COLS:TAIL_START]
    depth = w_in.shape[0]
    return dict(
        w_in=w_in, w_tail=w_in[:, :, TAIL_START:],
        w_rank=jnp.concatenate([w_rank, w_rank, w_rank,
                                jnp.zeros((depth, D_MODEL, BR_PAD - 6 * GLA_RANK), BF16)], axis=2),
        wglu=w['s5_w_glu'].astype(BF16), wbr=w['w_branch'].astype(BF16), wout=w['w_out'].astype(BF16),
        wg=w['w_ffn_gate'].astype(BF16), wu=w['w_ffn_up'].astype(BF16), wd=w['w_ffn_down'].astype(BF16))


def _prepare_layer(l, w):
    kw = B_HEADS * B_KDIM
    zk = jnp.zeros((GLA_RANK, kw), F32)
    wa1 = jnp.concatenate([jnp.concatenate([w['gla_wa2'][l, 0], zk], axis=1),
                           jnp.concatenate([zk, w['gla_wa2'][l, 1]], axis=1)], axis=0)
    wa = jnp.concatenate([wa1, wa1, wa1, jnp.zeros((BR_PAD - 6 * GLA_RANK, 2 * kw), F32)], axis=0)
    ba = w['gla_ba'][l].reshape(1, 2 * kw)
    gidx = np.arange(2 * MIX_W) // A_HEAD_DIM
    member = (gidx[:, None] == np.arange(LANES)[None, :]).astype(np.float32)
    gsum = jnp.asarray(member / A_HEAD_DIM, BF16)
    gspread = jnp.asarray(np.concatenate([member.T, member.T], axis=0), BF16)
    vec = lambda a: a.transpose(1, 0, 2).reshape(S5_G, 1, 2 * S5_P)
    mat = lambda a: a.transpose(1, 2, 0, 3).reshape(S5_G, S5_GROUP, 2 * S5_P)
    ldt = jnp.broadcast_to(w['s5_log_dt'][l][:, :, None], (2, S5_G, S5_P))
    toep, wst, cst, ap = _s5_prep(
        vec(w['s5_lam_re'][l]), vec(w['s5_lam_im'][l]), vec(ldt),
        mat(jnp.swapaxes(w['s5_b_re'][l], -1, -2)), mat(jnp.swapaxes(w['s5_b_im'][l], -1, -2)),
        mat(w['s5_c_re'][l]), mat(w['s5_c_im'][l]))
    return dict(
        wa=wa, ba=ba, gsum=gsum, gspread=gspread,
        n1g=w['norm1_g'][l].reshape(1, -1), n2g=w['norm2_g'][l].reshape(1, -1),
        qg=jnp.tile(w['diff_qn_g'][l], MIX_W // A_HEAD_DIM).reshape(1, -1),
        kg=jnp.tile(w['diff_kn_g'][l], MIX_W // A_HEAD_DIM).reshape(1, -1),
        lamp=w['diff_lam'][l], subg=jnp.tile(w['diff_subln_g'][l], A_HEADS).reshape(1, -1),
        ong=w['gla_on_g'][l].reshape(1, -1), s5d=w['s5_d'][l].reshape(1, -1),
        toep=toep, wst=wst, cst=cst, apow=ap, bglu=w['s5_b_glu'][l].reshape(1, -1),
    )


def _layer(l, x, mod3, p, sw, lam_init, ctx, rope_tabs):
    bsz, n_tok, _ = x.shape
    x2 = x.reshape(bsz * n_tok, D_MODEL)
    latent = ctx is not None
    q, k, v, gqk, gv, gg, la, cu, gates = _inproj(
        l, x2, mod3, p['n1g'], sw['w_in'], sw['w_tail'], sw['w_rank'], p['gsum'], p['gspread'], p['qg'], p['kg'],
        p['wa'], p['ba'], rope_tabs if latent else None, BF16 if latent else F32, n_tok // _s5_splits(bsz))
    sh = lambda a: a.reshape(bsz, n_tok, a.shape[-1])

    if latent:
        keys = [ctx['k'].reshape(bsz, -1, MIX_W), sh(k)]
        vals = [ctx['v'].reshape(bsz, -1, MIX_W), sh(v)]
    else:
        keys, vals = [sh(k)], [sh(v)]
    oa = _diff_attention(sh(q), keys, vals, p['lamp'], p['subg'], lam_init)

    of, ob, new_gla = _gla(sh(gqk), sh(gv), sh(la), ctx['gla'] if latent else None)

    h0 = ctx['s5'].transpose(3, 0, 2, 1, 4).reshape(S5_G, bsz, 4 * S5_P) if latent else None
    y, hfin = _s5(cu, p['toep'], p['wst'], p['cst'], p['apow'], h0, bsz)

    x_out = _mix_ffn(l, x2, mod3, oa.reshape(bsz * n_tok, MIX_W), of.reshape(bsz * n_tok, MIX_W),
                     ob.reshape(bsz * n_tok, MIX_W), gg, y, cu, gates, p['ong'], p['s5d'], sw['wglu'], p['bglu'],
                     sw['wbr'], sw['wout'], p['n2g'], sw['wg'], sw['wu'], sw['wd']).reshape(bsz, n_tok, D_MODEL)
    if latent:
        return x_out, None
    new_k = sh(k).reshape(bsz, n_tok, A_HEADS, 2, A_HEAD_DIM)
    new_v = sh(v).reshape(bsz, n_tok, A_HEADS, 2 * A_HEAD_DIM)
    hf = hfin.reshape(S5_G, bsz, 2, 2, S5_P)
    new_s5 = hf.transpose(1, 3, 2, 0, 4)
    return x_out, (new_k, new_v, new_gla, new_s5)


def kernel(x_prompt, x_sample, cache_diff_k, cache_diff_v, state_gla, state_s5, c, c_ctx, w_mod, b_mod, norm1_g, norm2_g, w_in, diff_qn_g, diff_kn_g, diff_lam, diff_subln_g, gla_wa2, gla_ba, gla_on_g, s5_lam_re, s5_lam_im, s5_log_dt, s5_b_re, s5_b_im, s5_c_re, s5_c_im, s5_d, s5_w_glu, s5_b_glu, w_branch, w_out, w_ffn_gate, w_ffn_up, w_ffn_down):
    weights = dict(norm1_g=norm1_g, norm2_g=norm2_g, w_in=w_in, diff_qn_g=diff_qn_g, diff_kn_g=diff_kn_g,
                   diff_lam=diff_lam, diff_subln_g=diff_subln_g, gla_wa2=gla_wa2, gla_ba=gla_ba,
                   gla_on_g=gla_on_g, s5_lam_re=s5_lam_re, s5_lam_im=s5_lam_im, s5_log_dt=s5_log_dt,
                   s5_b_re=s5_b_re, s5_b_im=s5_b_im, s5_c_re=s5_c_re, s5_c_im=s5_c_im, s5_d=s5_d,
                   s5_w_glu=s5_w_glu, s5_b_glu=s5_b_glu, w_branch=w_branch, w_out=w_out,
                   w_ffn_gate=w_ffn_gate, w_ffn_up=w_ffn_up, w_ffn_down=w_ffn_down)
    depth = w_mod.shape[0]
    dec_b = c.shape[0]
    cond8 = jnp.concatenate([c_ctx[None, :], c, jnp.zeros((SUBLANES - 1 - dec_b, D_MODEL), F32)], axis=0)
    mod = _modulation(cond8, w_mod, b_mod)
    rope_tabs = _rope_tables(x_sample.shape[1])
    sw = _prepare_shared(weights)
    y_prompt, y_sample = x_prompt, x_sample
    k_list, v_list, gla_list, s5_list = [], [], [], []
    for l in range(depth):
        p = _prepare_layer(l, weights)
        lam_init = 0.8 - 0.6 * math.exp(-0.3 * l)
        y_prompt, (k_l, v_l, g_l, s_l) = _layer(l, y_prompt, mod[l, 0:1][:, None, :], p, sw, lam_init, None, None)
        k_list.append(k_l)
        v_list.append(v_l)
        gla_list.append(g_l)
        s5_list.append(s_l)
        ctx = dict(k=cache_diff_k[:, l], v=cache_diff_v[:, l], gla=state_gla[:, l], s5=state_s5[:, l])
        y_sample, _ = _layer(l, y_sample, mod[l, 1:1 + dec_b][:, None, :], p, sw, lam_init, ctx, rope_tabs)
    return (y_prompt, y_sample, jnp.stack(k_list, axis=1), jnp.stack(v_list, axis=1),
            jnp.stack(gla_list, axis=1), jnp.stack(s5_list, axis=1))
```

```python
import functools
import math

import numpy as np
import jax
import jax.numpy as jnp
from jax import lax
from jax.experimental import pallas as pl
from jax.experimental.pallas import tpu as pltpu

F32 = jnp.float32
BF16 = jnp.bfloat16

D_MODEL = 1024
MIX_W = 512
A_HEADS = 4
A_HEAD_DIM = 64
GRID_W = 64
ROPE_THETA = 10000.0
B_HEADS = 4
B_KDIM = 64
B_VDIM = 128
GLA_RANK = 16
GLA_TAU = 16.0
S5_G = 32
S5_GROUP = 16
S5_P = 64
FFN_DIM = 2816
EPS = 1e-6

VMEM_LIMIT_BYTES = 56 * 1024 * 1024
INPROJ_TILE = 512
INPROJ_SUB_TILE = 256
DENSE_TILE = 256
ATTN_Q_TILE = 1024
ATTN_SUB_TILE = 256
GLA_CHUNK = 128
GLA_STEP_CHUNKS = 4
GLA_FIRST_DIFF_LEVEL = 3
S5_CHUNK = 16
SUBLANES = 8
LANES = 128
S5_TILE_GROUPS = LANES // S5_GROUP
S5_ROW_BLOCK = 64
S5_ROW_PAD = SUBLANES

OFF_AQ, OFF_AK, OFF_AV, OFF_BQK, OFF_BV, OFF_BG, HEAD_COLS = 0, 512, 1024, 1536, 2048, 2560, 3072
TAIL_START = HEAD_COLS + 2 * GLA_RANK
TAIL_CU, TAIL_GZ, TAIL_COLS = 0, MIX_W, MIX_W + 3 * D_MODEL
BR_PAD = 128


def _cparams(*sem):
    return pltpu.CompilerParams(dimension_semantics=sem, vmem_limit_bytes=VMEM_LIMIT_BYTES)


def _resident(shape):
    nd = len(shape)
    return pl.BlockSpec(shape, lambda *_: (0,) * nd, pipeline_mode=pl.Buffered(1))


def _layer_block(arr, l, last=None):
    tail = arr.shape[1:] if last is None else arr.shape[1:-1] + (last,)
    return pl.BlockSpec((None,) + tail, lambda *_: (l,) + (0,) * len(tail), pipeline_mode=pl.Buffered(1))


def _split_bf16(x):
    hi = x.astype(BF16)
    lo = (x - hi.astype(F32)).astype(BF16)
    return hi, lo


def _dot(a, b):
    return jnp.dot(a, b, preferred_element_type=F32)


def _dot_nt(a, b):
    return lax.dot_general(a, b, (((1,), (1,)), ((), ())), preferred_element_type=F32)


def _dot_tn(a, b):
    return lax.dot_general(a, b, (((0,), (0,)), ((), ())), preferred_element_type=F32)


def _dot_f32(a, b, nt=False):
    d = _dot_nt if nt else _dot
    ah, al = _split_bf16(a)
    bh, bl = _split_bf16(b)
    return d(ah, bh) + d(ah, bl) + d(al, bh)


def _rms(x):
    return x * lax.rsqrt(jnp.mean(x * x, axis=-1, keepdims=True) + EPS)


def _mod_kernel(cond_ref, w_ref, b_ref, o_ref):
    cnd = cond_ref[...]
    s = (cnd * jax.nn.sigmoid(cnd))
    o_ref[...] = _dot_f32(s, w_ref[...]) + b_ref[...]


def _modulation(cond8, w_mod, b_mod):
    depth, d, n = w_mod.shape
    tn = 1536
    return pl.pallas_call(
        _mod_kernel,
        grid=(depth, n // tn),
        in_specs=[pl.BlockSpec((SUBLANES, d), lambda l, j: (0, 0)),
                  pl.BlockSpec((None, d, tn), lambda l, j: (l, 0, j)),
                  pl.BlockSpec((None, 1, tn), lambda l, j: (l, 0, j))],
        out_specs=pl.BlockSpec((None, SUBLANES, tn), lambda l, j: (l, 0, j)),
        out_shape=jax.ShapeDtypeStruct((depth, SUBLANES, n), F32),
        compiler_params=_cparams("parallel", "parallel"),
        name="modulation",
    )(cond8, w_mod, b_mod.reshape(depth, 1, n))


def _group_rms64_pair(za, zb, gsum, gspread, gain_a, gain_b):
    w = za.shape[-1]
    sq = jnp.concatenate([(za * za).astype(BF16), (zb * zb).astype(BF16)], axis=1)
    hi, lo = _split_bf16(_dot(sq, gsum))
    ms = _dot(jnp.concatenate([hi, lo], axis=1), gspread)
    return za * lax.rsqrt(ms[:, :w] + EPS) * gain_a, zb * lax.rsqrt(ms[:, w:] + EPS) * gain_b


def _rope(z, c, s):
    n = z.shape[-1]
    lane = lax.broadcasted_iota(jnp.int32, z.shape, 1)
    first = (lane & 31) < 16
    partner = jnp.where(first, pltpu.roll(z, n - 16, 1), pltpu.roll(z, 16, 1))
    return z * c + partner * s


def _inproj_kernel(rope, x_ref, mod_ref, n1_ref, w_head_ref, w_tail_ref, w_rank_ref, gsum_ref, gspread_ref,
                   qg_ref, kg_ref, wa_ref, ba_ref, *rest):
    if rope:
        cos_ref, sin_ref = rest[:2]
        rest = rest[2:]
    q_ref, k_ref, v_ref, gqk_ref, gv_ref, gg_ref, la_ref, cu_ref, gate_ref = rest[1:]
    mod = mod_ref[...]
    sh1 = mod[:, 0:D_MODEL]
    sc1 = mod[:, D_MODEL:2 * D_MODEL]
    gsum = gsum_ref[...]
    gspread = gspread_ref[...]
    wa_hi, wa_lo = _split_bf16(wa_ref[...])
    wcopy = lax.broadcasted_iota(jnp.int32, wa_hi.shape, 0) // (2 * GLA_RANK)
    wa3 = jnp.where(wcopy == 1, wa_lo, wa_hi)
    sub = math.gcd(INPROJ_SUB_TILE, x_ref.shape[0])

    for rows in [slice(j * sub, (j + 1) * sub) for j in range(x_ref.shape[0] // sub)]:
        h = _rms(x_ref[rows, :]) * n1_ref[...] * (1.0 + sc1) + sh1
        hb = h.astype(BF16)

        def seg(a, b, w_ref=w_head_ref):
            return _dot(hb, w_ref[:, a:b])

        r = _dot(hb, w_rank_ref[...])
        r_hi, r_lo = _split_bf16(r)
        copy = lax.broadcasted_iota(jnp.int32, r.shape, 1) // (2 * GLA_RANK)
        pre = _dot(jnp.where(copy < 2, r_hi, r_lo), wa3) + ba_ref[...]
        la_ref[rows, :] = (jnp.minimum(pre, 0.0) - jnp.log1p(jnp.exp(-jnp.abs(pre)))) * (1.0 / GLA_TAU)

        q, k = _group_rms64_pair(seg(OFF_AQ, OFF_AK), seg(OFF_AK, OFF_AV), gsum, gspread, qg_ref[...], kg_ref[...])
        if rope:
            c = cos_ref[rows, :]
            s = sin_ref[rows, :]
            q = _rope(q, c, s)
            k = _rope(k, c, s)
        q_ref[rows, :] = (q * (A_HEAD_DIM ** -0.5)).astype(q_ref.dtype)
        k_ref[rows, :] = k.astype(k_ref.dtype)

        gate_ref[rows, :] = jax.nn.sigmoid(seg(TAIL_GZ, TAIL_COLS, w_tail_ref)).astype(gate_ref.dtype)
        bg = seg(OFF_BG, HEAD_COLS)
        gg_ref[rows, :] = (bg * jax.nn.sigmoid(bg)).astype(gg_ref.dtype)
        bqk = seg(OFF_BQK, OFF_BV)
        lane = lax.broadcasted_iota(jnp.int32, bqk.shape, 1)
        gqk_ref[rows, :] = jnp.where(lane < B_HEADS * B_KDIM, bqk * (B_KDIM ** -0.5), bqk).astype(gqk_ref.dtype)
        v_ref[rows, :] = seg(OFF_AV, OFF_BQK).astype(v_ref.dtype)
        gv_ref[rows, :] = seg(OFF_BV, OFF_BG).astype(gv_ref.dtype)
        cu = seg(TAIL_CU, TAIL_GZ, w_tail_ref).astype(cu_ref.dtype)
        if len(cu_ref.shape) == 2:
            cu_ref[rows, :] = cu
        else:
            seq = cu_ref.shape[1]
            for s in range(sub // seq):
                cu_ref[(rows.start + s * seq) // seq] = cu[s * seq:(s + 1) * seq]


def _token_tiling(n_tok, mod3, want):
    tm = math.gcd(want, n_tok // mod3.shape[0])
    tiles_per_mod = (n_tok // tm) // mod3.shape[0]
    return tm, pl.BlockSpec((None, 1, mod3.shape[-1]), lambda i: (i // tiles_per_mod, 0, 0))


def _inproj(l, x2, mod3, n1g, w_in, w_tail, w_rank, gsum, gspread, qg, kg, wa, ba, rope_tabs, kv_dtype, s5_seq):
    n_tok = x2.shape[0]
    tm, mod_spec = _token_tiling(n_tok, mod3, INPROJ_TILE)
    rope = rope_tabs is not None
    tok = lambda w: pl.BlockSpec((tm, w), lambda i: (i, 0))
    in_specs = [tok(D_MODEL), mod_spec, _resident(n1g.shape),
                _layer_block(w_in, l, HEAD_COLS), _layer_block(w_tail, l), _layer_block(w_rank, l),
                _resident(gsum.shape), _resident(gspread.shape),
                _resident(qg.shape), _resident(kg.shape), _resident(wa.shape), _resident(ba.shape)]
    args = [x2, mod3, n1g, w_in, w_tail, w_rank, gsum, gspread, qg, kg, wa, ba]
    if rope:
        tiles_per_seq = rope_tabs[0].shape[0] // tm
        for t in rope_tabs:
            in_specs.append(pl.BlockSpec((tm, MIX_W), lambda i: (i % tiles_per_seq, 0)))
            args.append(t)
    widths = [(MIX_W, BF16), (MIX_W, kv_dtype), (MIX_W, kv_dtype), (MIX_W, BF16), (MIX_W, BF16),
              (MIX_W, BF16), (MIX_W, F32), (MIX_W, F32), (3 * D_MODEL, BF16)]
    out_specs = [tok(w) for w, _ in widths]
    out_shape = [jax.ShapeDtypeStruct((n_tok, w), dt) for w, dt in widths]
    cu_out = 7
    cu_shape = (n_tok // s5_seq, s5_seq + S5_ROW_PAD, MIX_W)
    if tm <= s5_seq:
        per_seq = s5_seq // tm
        out_specs[cu_out] = pl.BlockSpec((None, tm, MIX_W), lambda i: (i // per_seq, i % per_seq, 0))
    else:
        out_specs[cu_out] = pl.BlockSpec((tm // s5_seq, s5_seq, MIX_W), lambda i: (i, 0, 0))
    out_shape[cu_out] = jax.ShapeDtypeStruct(cu_shape, F32)
    in_specs.append(pl.BlockSpec(memory_space=pl.ANY))
    args.append(jnp.zeros(cu_shape, F32))
    return pl.pallas_call(
        functools.partial(_inproj_kernel, rope),
        grid=(n_tok // tm,),
        in_specs=in_specs,
        out_specs=out_specs,
        out_shape=out_shape,
        input_output_aliases={len(args) - 1: cu_out},
        compiler_params=_cparams("parallel"),
        name="inproj",
    )(*args)


def _attn_kernel(lam_init, n_seg, q_ref, *refs):
    k_refs = refs[:n_seg]
    v_refs = refs[n_seg:2 * n_seg]
    lamp_ref, subg_ref, o_ref = refs[2 * n_seg:]
    lv = lamp_ref[...]
    lam = (jnp.exp(jnp.sum(lv[0:1] * lv[1:2], axis=-1, keepdims=True))
           - jnp.exp(jnp.sum(lv[2:3] * lv[3:4], axis=-1, keepdims=True)) + lam_init)
    hd = 2 * A_HEAD_DIM
    n_batch, n_rows = q_ref.shape[0], q_ref.shape[1]
    tq = math.gcd(ATTN_SUB_TILE, n_rows)
    for bi, sub, h in [(a, b, c) for a in range(n_batch) for b in range(n_rows // tq) for c in range(A_HEADS)]:
        rows = slice(sub * tq, (sub + 1) * tq)
        sl = slice(h * hd, (h + 1) * hd)
        qh = q_ref[bi, rows, sl]
        first = lax.broadcasted_iota(jnp.int32, qh.shape, 1) < A_HEAD_DIM
        zero = jnp.zeros_like(qh)
        q2 = jnp.concatenate([jnp.where(first, qh, zero), jnp.where(first, zero, qh)], axis=0)
        scores = [_dot_nt(q2, k_ref[bi, :, sl].astype(BF16)) for k_ref in k_refs]
        m = functools.reduce(jnp.maximum, [jnp.max(s, axis=-1, keepdims=True) for s in scores])
        acc = None
        for s, v_ref in zip(scores, v_refs):
            e = jnp.exp((s - m).astype(BF16))
            v_ext = jnp.concatenate([v_ref[bi, :, sl].astype(BF16), jnp.ones((v_ref.shape[1], hd), BF16)], axis=1)
            pv = _dot(e, v_ext)
            acc = pv if acc is None else acc + pv
        o2 = acc[:, :hd] / acc[:, hd:]
        o = o2[:tq] - lam * o2[tq:]
        o_ref[bi, rows, sl] = (_rms(o) * subg_ref[:, sl] * (1.0 - lam_init)).astype(o_ref.dtype)


def _diff_attention(q, ks, vs, lamp, subg, lam_init):
    bsz, lq, w = q.shape
    tq = math.gcd(ATTN_Q_TILE, lq)
    bb = math.gcd(max(1, ATTN_Q_TILE // lq), bsz)
    kv_spec = lambda a: pl.BlockSpec((bb, a.shape[1], w), lambda b, i: (b, 0, 0))
    return pl.pallas_call(
        functools.partial(_attn_kernel, lam_init, len(ks)),
        grid=(bsz // bb, lq // tq),
        in_specs=[pl.BlockSpec((bb, tq, w), lambda b, i: (b, i, 0))]
                 + [kv_spec(a) for a in ks] + [kv_spec(a) for a in vs]
                 + [pl.BlockSpec(lamp.shape, lambda b, i: (0, 0)), pl.BlockSpec(subg.shape, lambda b, i: (0, 0))],
        out_specs=pl.BlockSpec((bb, tq, w), lambda b, i: (b, i, 0)),
        out_shape=jax.ShapeDtypeStruct((bsz, lq, w), BF16),
        compiler_params=_cparams("parallel", "parallel"),
        name="diff_attention",
    )(q, *ks, *vs, lamp, subg)


def _gla_masks(chunk):
    nlev = int(math.log2(chunk))
    assert 1 << nlev == chunk
    t = np.arange(chunk)[:, None]
    r = np.arange(chunk)[None, :]
    cum, pair = [], []
    for j in range(nlev + 1):
        start = (t >> j) << j
        end = start + (1 << j) - 1
        if 0 < j < GLA_FIRST_DIFF_LEVEL:
            cum.append((r >= start) & (r <= t))
            cum.append((r > t) & (r <= end))
        if j < nlev:
            pair.append(((t >> (j + 1)) == (r >> (j + 1))) & (((t >> j) & 1) == 1) & (((r >> j) & 1) == 0))
    cum.append(r <= t)
    pair.append(t == r)
    dup = lambda m: np.concatenate([m, m], axis=1)
    cum_f = dup(np.concatenate(cum, 0).astype(np.float32))
    pair_f = np.stack(pair, 0).astype(np.float32)
    cum_b = dup(np.concatenate([m[::-1, ::-1] for m in cum], 0).astype(np.float32))
    pair_b = pair_f[:, ::-1, ::-1]
    return (jnp.asarray(np.stack([cum_f, cum_b]), BF16), jnp.asarray(np.stack([pair_f, pair_b]), F32), nlev)


def _gla_chunk(qk, v, la, cum, pair_ref, d, st_ref, run_ref, bd, nlev, last_row):
    c = qk.shape[0]
    kw = B_HEADS * B_KDIM
    q = qk[:, :kw].astype(F32)
    k = qk[:, kw:].astype(F32)
    hi, lo = _split_bf16(la)
    sums = _dot(cum, jnp.concatenate([hi, lo], axis=0))
    n_mxu = GLA_FIRST_DIFF_LEVEL - 1
    run = sums[2 * n_mxu * c:]
    run_ref[...] = run

    def run_at(size, row_of_block):
        pieces = []
        for kb in range(c // size):
            r = row_of_block(kb)
            pieces.append(jnp.broadcast_to(run_ref[r:r + 1, :], (size, kw)) if 0 <= r < c
                          else jnp.zeros((size, kw), F32))
        return pieces[0] if len(pieces) == 1 else jnp.concatenate(pieces, axis=0)

    def factors(j):
        if j < GLA_FIRST_DIFF_LEVEL:
            return jnp.exp(sums[(2 * j - 2) * c:(2 * j - 1) * c]), jnp.exp(sums[(2 * j - 1) * c:(2 * j) * c])
        size = 1 << j
        if d == 0:
            before = run_at(size, lambda kb: kb * size - 1)
            last = run_at(size, lambda kb: (kb + 1) * size - 1)
            return jnp.exp(run - before), jnp.exp(last - run)
        after = run_at(size, lambda kb: (kb + 1) * size)
        first = run_at(size, lambda kb: kb * size)
        return jnp.exp(run - after), jnp.exp(first - run)

    lane = lax.broadcasted_iota(jnp.int32, (c, kw), 1)
    head_masks = [(lane >= h * B_KDIM) & (lane < (h + 1) * B_KDIM) for h in range(B_HEADS)]
    zero = jnp.zeros((c, kw), BF16)

    def scores(qf, kf, pm):
        qb = qf.astype(BF16)
        kb = kf.astype(BF16)
        stacked = _dot_nt(jnp.concatenate([jnp.where(m, qb, zero) for m in head_masks], axis=0), kb)
        return [pm * stacked[h * c:(h + 1) * c] for h in range(B_HEADS)]

    att = scores(q, k, pair_ref[d, nlev])
    for j in range(nlev):
        if j == 0:
            lev = scores(q * jnp.exp(la), k, pair_ref[d, 0])
        else:
            eq, ek = factors(j)
            lev = scores(q * eq, k * ek, pair_ref[d, j])
        att = [a + b for a, b in zip(att, lev)]
    eq, ek = factors(nlev)
    st = st_ref[...]
    o = _dot_nt((q * eq).astype(BF16), st.astype(BF16))
    outs = []
    for h in range(B_HEADS):
        sl = slice(h * B_VDIM, (h + 1) * B_VDIM)
        outs.append(o[:, sl] + _dot(att[h].astype(BF16), v[:, sl]))
    dec = eq[last_row:last_row + 1, :]
    st_ref[...] = st * dec + bd * _dot_tn(v, (k * ek).astype(BF16))
    return outs


def _gla_kernel(has_s0, nlev, *refs):
    if has_s0:
        s0_ref, refs = refs[0], refs[1:]
    (qkf_ref, vf_ref, laf_ref, qkb_ref, vb_ref, lab_ref, cum_ref, pair_ref, bd_ref,
     of_ref, ob_ref, sfin_ref, st_f, st_b, run_f, run_b) = refs
    i = pl.program_id(1)
    c = GLA_CHUNK
    n_sub = qkf_ref.shape[0] // c

    @pl.when(i == 0)
    def _():
        for d, st in enumerate((st_f, st_b)):
            if has_s0:
                zero = jnp.zeros((B_KDIM, B_VDIM), F32)
                full = jnp.concatenate(
                    [jnp.concatenate([s0_ref[d, h] if hh == h else zero for hh in range(B_HEADS)], axis=1)
                     for h in range(B_HEADS)], axis=0)
                st[...] = full.T
            else:
                st[...] = jnp.zeros_like(st)

    bd = bd_ref[...]
    for sub in range(n_sub):
        rf = slice(sub * c, (sub + 1) * c)
        outs = _gla_chunk(qkf_ref[rf, :], vf_ref[rf, :], laf_ref[rf, :], cum_ref[0], pair_ref, 0, st_f, run_f, bd,
                          nlev, c - 1)
        for h, o in enumerate(outs):
            of_ref[rf, h * B_VDIM:(h + 1) * B_VDIM] = o.astype(of_ref.dtype)
        rb = slice((n_sub - 1 - sub) * c, (n_sub - sub) * c)
        outs = _gla_chunk(qkb_ref[rb, :], vb_ref[rb, :], lab_ref[rb, :], cum_ref[1], pair_ref, 1, st_b, run_b, bd,
                          nlev, 0)
        for h, o in enumerate(outs):
            ob_ref[rb, h * B_VDIM:(h + 1) * B_VDIM] = o.astype(ob_ref.dtype)

    @pl.when(i == pl.num_programs(1) - 1)
    def _():
        for d, st in enumerate((st_f, st_b)):
            full = st[...].T
            for h in range(B_HEADS):
                sfin_ref[d, h] = full[h * B_KDIM:(h + 1) * B_KDIM, h * B_VDIM:(h + 1) * B_VDIM]


def _gla(gqk, gv, la, s0):
    bsz, n_tok, _ = gqk.shape
    c = math.gcd(GLA_STEP_CHUNKS * GLA_CHUNK, n_tok)
    n = n_tok // c
    cum, pair, nlev = _gla_masks(GLA_CHUNK)
    kw = B_HEADS * B_KDIM
    vw = B_HEADS * B_VDIM
    rows = np.arange(vw)[:, None] // B_VDIM
    cols = np.arange(kw)[None, :] // B_KDIM
    bd = jnp.asarray((rows == cols).astype(np.float32))
    fwd = lambda w, off=0: pl.BlockSpec((None, c, w), lambda b, i: (b, i, off))
    bwd = lambda w, off=0: pl.BlockSpec((None, c, w), lambda b, i: (b, n - 1 - i, off))
    in_specs = [fwd(2 * kw), fwd(vw), fwd(kw, 0), bwd(2 * kw), bwd(vw), bwd(kw, 1),
                _resident(cum.shape), _resident(pair.shape), _resident(bd.shape)]
    args = [gqk, gv, la, gqk, gv, la, cum, pair, bd]
    state_spec = pl.BlockSpec((None, 2, B_HEADS, B_KDIM, B_VDIM), lambda b, i: (b, 0, 0, 0, 0))
    if s0 is not None:
        in_specs.insert(0, state_spec)
        args.insert(0, s0)
    return pl.pallas_call(
        functools.partial(_gla_kernel, s0 is not None, nlev),
        grid=(bsz, n),
        in_specs=in_specs,
        out_specs=[fwd(vw), bwd(vw), state_spec],
        out_shape=[jax.ShapeDtypeStruct((bsz, n_tok, vw), BF16), jax.ShapeDtypeStruct((bsz, n_tok, vw), BF16),
                   jax.ShapeDtypeStruct((bsz, 2, B_HEADS, B_KDIM, B_VDIM), F32)],
        scratch_shapes=[pltpu.VMEM((vw, kw), F32), pltpu.VMEM((vw, kw), F32),
                        pltpu.VMEM((GLA_CHUNK, kw), F32), pltpu.VMEM((GLA_CHUNK, kw), F32)],
        compiler_params=_cparams("parallel", "arbitrary"),
        name="gla",
    )(*args)


def _s5_prep_kernel(lr_ref, li_ref, dt_ref, bre_ref, bim_ref, cre_ref, cim_ref,
                    toep_ref, wst_ref, cst_ref, apow_ref):
    t_len = S5_CHUNK
    grp = S5_GROUP
    rows = t_len * grp
    lr = lr_ref[...]
    li = li_ref[...]
    dt = jnp.exp(dt_ref[...])
    a = lr * dt
    th = li * dt
    mag = jnp.exp(a)
    ar = mag * jnp.cos(th)
    ai = mag * jnp.sin(th)
    den = lr * lr + li * li
    fr = ((ar - 1.0) * lr + ai * li) / den
    fi = (ai * lr - (ar - 1.0) * li) / den
    b_re = bre_ref[...]
    b_im = bim_ref[...]
    bbr = jnp.concatenate([fr * b_re - fi * b_im] * t_len, axis=0)
    bbi = jnp.concatenate([fr * b_im + fi * b_re] * t_len, axis=0)
    c_re = cre_ref[...]
    c_im = cim_ref[...]
    c_re_t = jnp.concatenate([c_re] * t_len, axis=0)
    c_im_t = jnp.concatenate([c_im] * t_len, axis=0)

    n_pow = -(-(t_len + 1) // SUBLANES) * SUBLANES
    ex = lax.broadcasted_iota(jnp.int32, (n_pow, LANES), 0).astype(F32)
    pmag = jnp.exp(ex * a)
    pow_re = pmag * jnp.cos(ex * th)
    pow_im = pmag * jnp.sin(ex * th)
    is_fwd = lax.broadcasted_iota(jnp.int32, (grp, LANES), 1) < S5_P

    def expand(tab, exp_fwd, exp_bwd):
        blocks = []
        for step in range(t_len):
            f = jnp.broadcast_to(tab[exp_fwd(step):exp_fwd(step) + 1, :], (grp, LANES))
            b = jnp.broadcast_to(tab[exp_bwd(step):exp_bwd(step) + 1, :], (grp, LANES))
            blocks.append(jnp.where(is_fwd, f, b))
        return jnp.concatenate(blocks, axis=0)

    pr = expand(pow_re, lambda s: t_len - 1 - s, lambda s: s)
    pi = expand(pow_im, lambda s: t_len - 1 - s, lambda s: s)
    xr = bbr * pr - bbi * pi
    xi = bbr * pi + bbi * pr
    wst_ref[...] = jnp.concatenate([xr, xi], axis=1).astype(wst_ref.dtype)

    fwd_rows = lax.broadcasted_iota(jnp.int32, (rows, LANES), 1) < S5_P
    c_re_rep = jnp.concatenate([c_re] * (LANES // grp), axis=0)
    c_im_rep = jnp.concatenate([c_im] * (LANES // grp), axis=0)
    zero = jnp.zeros_like(xr)

    def lag_kernel(keep):
        return (_dot_f32(jnp.where(keep, xr, zero), c_re_rep, nt=True)
                - _dot_f32(jnp.where(keep, xi, zero), c_im_rep, nt=True))

    ker_f = lag_kernel(fwd_rows)
    ker_b = lag_kernel(jnp.logical_not(fwd_rows))
    blk = lax.broadcasted_iota(jnp.int32, (rows, LANES), 1) // grp

    def column_block(t):
        up = (t_len - 1 - t) * grp
        down = t * grp
        f = ker_f[up:] if up == 0 else jnp.concatenate([ker_f[up:], jnp.zeros((up, LANES), F32)], axis=0)
        b = ker_b if down == 0 else jnp.concatenate([jnp.zeros((down, LANES), F32), ker_b[:rows - down]], axis=0)
        return f + b

    per_tile = LANES // grp
    for h in range(t_len // per_tile):
        tile = column_block(h * per_tile)
        for j in range(1, per_tile):
            tile = jnp.where(blk == j, column_block(h * per_tile + j), tile)
        toep_ref[:, h * LANES:(h + 1) * LANES] = tile.astype(toep_ref.dtype)

    pr = expand(pow_re, lambda t: t + 1, lambda t: t_len - t)
    pi = expand(pow_im, lambda t: t + 1, lambda t: t_len - t)
    cr = c_re_t * pr - c_im_t * pi
    ci = -(c_re_t * pi + c_im_t * pr)
    cst_ref[...] = jnp.concatenate([jnp.where(fwd_rows, cr, zero), jnp.where(fwd_rows, ci, zero),
                                    jnp.where(fwd_rows, zero, cr), jnp.where(fwd_rows, zero, ci)],
                                   axis=1).astype(cst_ref.dtype)
    apow_ref[...] = jnp.concatenate([pow_re[t_len:t_len + 1], pow_im[t_len:t_len + 1],
                                     jnp.zeros((SUBLANES - 2, LANES), F32)], axis=0)


def _s5_prep(lr, li, ldt, bre_t, bim_t, cre, cim):
    g = lr.shape[0]
    rows = S5_CHUNK * S5_GROUP
    grp = lambda *s: pl.BlockSpec((None,) + s, lambda i: (i,) + (0,) * len(s))
    return pl.pallas_call(
        _s5_prep_kernel,
        grid=(g,),
        in_specs=[grp(1, LANES)] * 3 + [grp(S5_GROUP, LANES)] * 4,
        out_specs=[grp(rows, rows), grp(rows, 4 * S5_P), grp(rows, 8 * S5_P), grp(SUBLANES, LANES)],
        out_shape=[jax.ShapeDtypeStruct((g, rows, rows), BF16),
                   jax.ShapeDtypeStruct((g, rows, 4 * S5_P), BF16),
                   jax.ShapeDtypeStruct((g, rows, 8 * S5_P), BF16),
                   jax.ShapeDtypeStruct((g, SUBLANES, LANES), F32)],
        compiler_params=_cparams("parallel"),
        name="s5_prep",
    )(lr, li, ldt, bre_t, bim_t, cre, cim)


def _block_transpose(xs, blk):
    n = len(xs)
    d = n // 2
    while d >= 1:
        low = (blk & d) == 0
        new = list(xs)
        for i in range(n):
            if i & d == 0:
                a, b = xs[i], xs[i | d]
                new[i] = jnp.where(low, a, pltpu.roll(b, d * S5_GROUP, 1))
                new[i | d] = jnp.where(low, pltpu.roll(a, LANES - d * S5_GROUP, 1), b)
        xs = new
        d //= 2
    return xs


def _s5_kernel(n_chunks, bp, splits, cu_ref, toep_ref, wst_ref, cst_ref, apow_ref, h0_ref, y_ref, hfin_ref,
               u_scr, s_scr, h_scr, yg_scr):
    t_len = S5_CHUNK
    ng = S5_TILE_GROUPS
    seq = n_chunks * t_len
    pitch = seq + S5_ROW_PAD
    rows = bp * n_chunks
    rb = max(bp, math.gcd(S5_ROW_BLOCK, rows))
    cpb = rb // bp
    blk = lax.broadcasted_iota(jnp.int32, (rb, LANES), 1) // S5_GROUP

    def relayout_in(r, carry):
        r0 = pl.multiple_of(r * rb, rb)
        for h in range(t_len // ng):
            xs = []
            for b in range(ng):
                pieces = [cu_ref[pl.ds((r * cpb + j) * t_len + ng * h + b, bp, stride=pitch), :] for j in range(cpb)]
                xs.append(pieces[0] if cpb == 1 else jnp.concatenate(pieces, axis=0))
            for g, tile in enumerate(_block_transpose(xs, blk)):
                u_scr[g, pl.ds(r0, rb), h * LANES:(h + 1) * LANES] = tile.astype(BF16)
        return carry

    lax.fori_loop(0, rows // rb, relayout_in, 0)

    half = 2 * S5_P
    for g in range(ng):
        s_scr[g] = _dot(u_scr[g], wst_ref[g])

    is_fwd = lax.broadcasted_iota(jnp.int32, (bp, half), 1) < S5_P

    def scan(init, store):
        def step(kk, hs):
            f0 = pl.multiple_of(kk * bp, bp)
            b0 = pl.multiple_of((n_chunks - 1 - kk) * bp, bp)
            new = []
            for g in range(ng):
                re, im = hs[g]
                if store:
                    h_scr[g, pl.ds(f0, bp), 0:half] = re
                    h_scr[g, pl.ds(f0, bp), half:2 * half] = im
                    h_scr[g, pl.ds(b0, bp), 2 * half:3 * half] = re
                    h_scr[g, pl.ds(b0, bp), 3 * half:4 * half] = im
                s_f = s_scr[g, pl.ds(f0, bp), :]
                s_b = s_scr[g, pl.ds(b0, bp), :]
                a_re = apow_ref[g, 0:1, :]
                a_im = apow_ref[g, 1:2, :]
                new.append((a_re * re - a_im * im + jnp.where(is_fwd, s_f[:, :half], s_b[:, :half]),
                            a_re * im + a_im * re + jnp.where(is_fwd, s_f[:, half:], s_b[:, half:])))
            return tuple(new)

        return lax.fori_loop(0, n_chunks, step, init)

    h0 = tuple((h0_ref[g, :, :half], h0_ref[g, :, half:]) for g in range(ng))
    init = h0
    if splits > 1:
        seg = lax.broadcasted_iota(jnp.int32, (bp, half), 0) & (splits - 1)
        take_prev = is_fwd & (seg >= 1)
        take_next = jnp.logical_not(is_fwd) & (seg <= splits - 2)
        for _ in range(splits - 1):
            fin = scan(init, store=False)
            init = tuple(tuple(part0 + jnp.where(take_prev, pltpu.roll(part, 1, 0),
                                                 jnp.where(take_next, pltpu.roll(part, bp - 1, 0), 0.0))
                               for part0, part in zip(h0[g], fin[g])) for g in range(ng))
    hs = scan(init, store=True)
    for g in range(ng):
        hfin_ref[g, :, :half] = hs[g][0]
        hfin_ref[g, :, half:] = hs[g][1]
        yg_scr[g] = _dot(u_scr[g], toep_ref[g]) + _dot_nt(h_scr[g].astype(BF16), cst_ref[g])

    def relayout_out(r, carry):
        r0 = pl.multiple_of(r * rb, rb)
        for h in range(t_len // ng):
            ys = [yg_scr[g, pl.ds(r0, rb), h * LANES:(h + 1) * LANES] for g in range(ng)]
            for b, tile in enumerate(_block_transpose(ys, blk)):
                for j in range(cpb):
                    y_ref[pl.ds((r * cpb + j) * t_len + ng * h + b, bp, stride=pitch), :] = tile[j * bp:(j + 1) * bp]
        return carry

    lax.fori_loop(0, rows // rb, relayout_out, 0)
    for pb in range(bp):
        y_ref[pb * pitch + seq:(pb + 1) * pitch, :] = jnp.zeros((S5_ROW_PAD, LANES), F32)


def _s5_splits(bsz):
    return SUBLANES // bsz if SUBLANES % bsz == 0 else 1


def _s5(cu, toep, wst, cst, apow, h0, bsz):
    bp, pitch, width = cu.shape
    splits = _s5_splits(bsz)
    assert bp == bsz * splits
    n_chunks = (pitch - S5_ROW_PAD) // S5_CHUNK
    n_rows = bp * pitch
    cu = cu.reshape(n_rows, width)
    rows = bp * n_chunks
    ng = S5_TILE_GROUPS
    w = S5_CHUNK * S5_GROUP
    p2 = 2 * S5_P
    if h0 is None:
        h0p = jnp.zeros((S5_G, bp, 2 * p2), F32)
    elif splits == 1:
        h0p = h0
    else:
        h04 = h0.reshape(S5_G, bsz, 1, 2, 2, S5_P)
        zero = jnp.zeros((S5_G, bsz, splits - 1, 2, 1, S5_P), F32)
        h0p = jnp.concatenate([jnp.concatenate([h04[..., 0:1, :], zero], axis=2),
                               jnp.concatenate([zero, h04[..., 1:2, :]], axis=2)], axis=4)
        h0p = h0p.reshape(S5_G, bp, 2 * p2)
    tile = lambda *s: pl.BlockSpec((ng,) + s, lambda j: (j,) + (0,) * len(s))
    lanes = lambda r: pl.BlockSpec((r, LANES), lambda j: (0, j))
    out_rows = bp * (n_chunks * S5_CHUNK + S5_ROW_PAD)
    y, hfin = pl.pallas_call(
        functools.partial(_s5_kernel, n_chunks, bp, splits),
        grid=(width // LANES,),
        in_specs=[lanes(n_rows), tile(w, w), tile(w, 2 * p2), tile(w, 4 * p2), tile(SUBLANES, p2),
                  tile(bp, 2 * p2)],
        out_specs=[lanes(out_rows), tile(bp, 2 * p2)],
        out_shape=[jax.ShapeDtypeStruct((out_rows, width), F32), jax.ShapeDtypeStruct((S5_G, bp, 2 * p2), F32)],
        scratch_shapes=[pltpu.VMEM((ng, rows, w), BF16), pltpu.VMEM((ng, rows, 2 * p2), F32),
                        pltpu.VMEM((ng, rows, 4 * p2), F32), pltpu.VMEM((ng, rows, w), F32)],
        compiler_params=_cparams("parallel"),
        name="s5_scan",
    )(cu, toep, wst, cst, apow, h0p)
    hf = hfin.reshape(S5_G, bsz, splits, 2, 2, S5_P)
    hfin = jnp.stack([hf[:, :, splits - 1, :, 0], hf[:, :, 0, :, 1]], axis=3)
    return y.reshape(bp, out_rows // bp, width), hfin.reshape(S5_G, bsz, 2 * p2)


def _mix_ffn_kernel(x_ref, mod_ref, oa_ref, of_ref, ob_ref, gg_ref, y_ref, cu_ref, gate_ref,
                    ong_ref, s5d_ref, wglu_ref, bglu_ref, wbr_ref, wout_ref, n2_ref, wg_ref, wu_ref, wd_ref, o_ref):
    mod = mod_ref[...]
    g1 = mod[:, 2 * D_MODEL:3 * D_MODEL]
    o_gla = of_ref[...].astype(F32) + ob_ref[...].astype(F32)
    parts = []
    for h in range(B_HEADS):
        sl = slice(h * B_VDIM, (h + 1) * B_VDIM)
        parts.append((_rms(o_gla[:, sl]) * ong_ref[...] * gg_ref[:, sl].astype(F32)).astype(BF16))
    ob = jnp.concatenate(parts, axis=1)

    cu = cu_ref[...].astype(F32)
    z = y_ref[...].astype(F32) + s5d_ref[...] * cu
    yc = 0.5 * z * (1.0 + jnp.tanh(math.sqrt(2.0 / math.pi) * (z + 0.044715 * (z * z * z))))
    glu = _dot(yc.astype(BF16), wglu_ref[...]) + bglu_ref[...]
    oc = (glu[:, :MIX_W] * jax.nn.sigmoid(glu[:, MIX_W:])).astype(BF16)

    merged = None
    for r, br in enumerate((oa_ref[...], ob, oc)):
        term = gate_ref[:, r * D_MODEL:(r + 1) * D_MODEL].astype(F32) * _dot(br, wbr_ref[r])
        merged = term if merged is None else merged + term
    x = x_ref[...] + g1 * _dot(merged.astype(BF16), wout_ref[...])

    sh2 = mod[:, 3 * D_MODEL:4 * D_MODEL]
    sc2 = mod[:, 4 * D_MODEL:5 * D_MODEL]
    g2 = mod[:, 5 * D_MODEL:6 * D_MODEL]
    hb = (_rms(x) * n2_ref[...] * (1.0 + sc2) + sh2).astype(BF16)
    gate = _dot(hb, wg_ref[...])
    act = (gate * jax.nn.sigmoid(gate) * _dot(hb, wu_ref[...])).astype(BF16)
    o_ref[...] = x + g2 * _dot(act, wd_ref[...])


def _mix_ffn(l, x2, mod3, oa, of, ob, gg, y, cu, gates, ong, s5d, wglu, bglu, wbr, wout, n2g, wg, wu, wd):
    n_tok = x2.shape[0]
    seq = n_tok // y.shape[0]
    tm, mod_spec = _token_tiling(n_tok, mod3, math.gcd(DENSE_TILE, seq))
    tiles_per_seq = seq // tm
    tok = lambda w: pl.BlockSpec((tm, w), lambda i: (i, 0))
    y_spec = pl.BlockSpec((None, tm, MIX_W), lambda i: (i // tiles_per_seq, i % tiles_per_seq, 0))
    return pl.pallas_call(
        _mix_ffn_kernel,
        grid=(n_tok // tm,),
        in_specs=[tok(D_MODEL), mod_spec,
                  tok(MIX_W), tok(MIX_W), tok(MIX_W), tok(MIX_W), y_spec, y_spec, tok(3 * D_MODEL),
                  _resident(ong.shape), _resident(s5d.shape), _layer_block(wglu, l), _resident(bglu.shape),
                  _layer_block(wbr, l), _layer_block(wout, l),
                  _resident(n2g.shape), _layer_block(wg, l), _layer_block(wu, l), _layer_block(wd, l)],
        out_specs=tok(D_MODEL),
        out_shape=jax.ShapeDtypeStruct((n_tok, D_MODEL), F32),
        compiler_params=_cparams("parallel"),
        name="mix_ffn",
    )(x2, mod3, oa, of, ob, gg, y, cu, gates, ong, s5d, wglu, bglu, wbr, wout, n2g, wg, wu, wd)


def _rope_tables(n_tok):
    t = np.arange(n_tok)
    row = (t // GRID_W).astype(np.float32)
    col = (t % GRID_W).astype(np.float32)
    half = A_HEAD_DIM // 2
    inv = jnp.asarray(ROPE_THETA, F32) ** (-jnp.arange(0, half, 2, dtype=F32) / half)
    ang_r = jnp.asarray(row)[:, None] * inv
    ang_c = jnp.asarray(col)[:, None] * inv
    cos = jnp.concatenate([jnp.cos(ang_r)] * 2 + [jnp.cos(ang_c)] * 2, axis=-1)
    sin = jnp.concatenate([-jnp.sin(ang_r), jnp.sin(ang_r), -jnp.sin(ang_c), jnp.sin(ang_c)], axis=-1)
    reps = MIX_W // A_HEAD_DIM
    return jnp.tile(cos, (1, reps)), jnp.tile(sin, (1, reps))


def _prepare_shared(w):
    w_in = w['w_in'].astype(BF16)
    w_rank = w_in[:, :, HEAD_COLS:TAIL_START]
    depth = w_in.shape[0]
    return dict(
        w_in=w_in, w_tail=w_in[:, :, TAIL_START:],
        w_rank=jnp.concatenate([w_rank, w_rank, w_rank,
                                jnp.zeros((depth, D_MODEL, BR_PAD - 6 * GLA_RANK), BF16)], axis=2),
        wglu=w['s5_w_glu'].astype(BF16), wbr=w['w_branch'].astype(BF16), wout=w['w_out'].astype(BF16),
        wg=w['w_ffn_gate'].astype(BF16), wu=w['w_ffn_up'].astype(BF16), wd=w['w_ffn_down'].astype(BF16))


def _prepare_layer(l, w):
    kw = B_HEADS * B_KDIM
    zk = jnp.zeros((GLA_RANK, kw), F32)
    wa1 = jnp.concatenate([jnp.concatenate([w['gla_wa2'][l, 0], zk], axis=1),
                           jnp.concatenate([zk, w['gla_wa2'][l, 1]], axis=1)], axis=0)
    wa = jnp.concatenate([wa1, wa1, wa1, jnp.zeros((BR_PAD - 6 * GLA_RANK, 2 * kw), F32)], axis=0)
    ba = w['gla_ba'][l].reshape(1, 2 * kw)
    gidx = np.arange(2 * MIX_W) // A_HEAD_DIM
    member = (gidx[:, None] == np.arange(LANES)[None, :]).astype(np.float32)
    gsum = jnp.asarray(member / A_HEAD_DIM, BF16)
    gspread = jnp.asarray(np.concatenate([member.T, member.T], axis=0), BF16)
    vec = lambda a: a.transpose(1, 0, 2).reshape(S5_G, 1, 2 * S5_P)
    mat = lambda a: a.transpose(1, 2, 0, 3).reshape(S5_G, S5_GROUP, 2 * S5_P)
    ldt = jnp.broadcast_to(w['s5_log_dt'][l][:, :, None], (2, S5_G, S5_P))
    toep, wst, cst, ap = _s5_prep(
        vec(w['s5_lam_re'][l]), vec(w['s5_lam_im'][l]), vec(ldt),
        mat(jnp.swapaxes(w['s5_b_re'][l], -1, -2)), mat(jnp.swapaxes(w['s5_b_im'][l], -1, -2)),
        mat(w['s5_c_re'][l]), mat(w['s5_c_im'][l]))
    return dict(
        wa=wa, ba=ba, gsum=gsum, gspread=gspread,
        n1g=w['norm1_g'][l].reshape(1, -1), n2g=w['norm2_g'][l].reshape(1, -1),
        qg=jnp.tile(w['diff_qn_g'][l], MIX_W // A_HEAD_DIM).reshape(1, -1),
        kg=jnp.tile(w['diff_kn_g'][l], MIX_W // A_HEAD_DIM).reshape(1, -1),
        lamp=w['diff_lam'][l], subg=jnp.tile(w['diff_subln_g'][l], A_HEADS).reshape(1, -1),
        ong=w['gla_on_g'][l].reshape(1, -1), s5d=w['s5_d'][l].reshape(1, -1),
        toep=toep, wst=wst, cst=cst, apow=ap, bglu=w['s5_b_glu'][l].reshape(1, -1),
    )


def _layer(l, x, mod3, p, sw, lam_init, ctx, rope_tabs):
    bsz, n_tok, _ = x.shape
    x2 = x.reshape(bsz * n_tok, D_MODEL)
    latent = ctx is not None
    q, k, v, gqk, gv, gg, la, cu, gates = _inproj(
        l, x2, mod3, p['n1g'], sw['w_in'], sw['w_tail'], sw['w_rank'], p['gsum'], p['gspread'], p['qg'], p['kg'],
        p['wa'], p['ba'], rope_tabs if latent else None, BF16 if latent else F32, n_tok // _s5_splits(bsz))
    sh = lambda a: a.reshape(bsz, n_tok, a.shape[-1])

    if latent:
        keys = [ctx['k'].reshape(bsz, -1, MIX_W), sh(k)]
        vals = [ctx['v'].reshape(bsz, -1, MIX_W), sh(v)]
    else:
        keys, vals = [sh(k)], [sh(v)]
    oa = _diff_attention(sh(q), keys, vals, p['lamp'], p['subg'], lam_init)

    of, ob, new_gla = _gla(sh(gqk), sh(gv), sh(la), ctx['gla'] if latent else None)

    h0 = ctx['s5'].transpose(3, 0, 2, 1, 4).reshape(S5_G, bsz, 4 * S5_P) if latent else None
    y, hfin = _s5(cu, p['toep'], p['wst'], p['cst'], p['apow'], h0, bsz)

    x_out = _mix_ffn(l, x2, mod3, oa.reshape(bsz * n_tok, MIX_W), of.reshape(bsz * n_tok, MIX_W),
                     ob.reshape(bsz * n_tok, MIX_W), gg, y, cu, gates, p['ong'], p['s5d'], sw['wglu'], p['bglu'],
                     sw['wbr'], sw['wout'], p['n2g'], sw['wg'], sw['wu'], sw['wd']).reshape(bsz, n_tok, D_MODEL)
    if latent:
        return x_out, None
    new_k = sh(k).reshape(bsz, n_tok, A_HEADS, 2, A_HEAD_DIM)
    new_v = sh(v).reshape(bsz, n_tok, A_HEADS, 2 * A_HEAD_DIM)
    hf = hfin.reshape(S5_G, bsz, 2, 2, S5_P)
    new_s5 = hf.transpose(1, 3, 2, 0, 4)
    return x_out, (new_k, new_v, new_gla, new_s5)


def kernel(x_prompt, x_sample, cache_diff_k, cache_diff_v, state_gla, state_s5, c, c_ctx, w_mod, b_mod, norm1_g, norm2_g, w_in, diff_qn_g, diff_kn_g, diff_lam, diff_subln_g, gla_wa2, gla_ba, gla_on_g, s5_lam_re, s5_lam_im, s5_log_dt, s5_b_re, s5_b_im, s5_c_re, s5_c_im, s5_d, s5_w_glu, s5_b_glu, w_branch, w_out, w_ffn_gate, w_ffn_up, w_ffn_down):
    weights = dict(norm1_g=norm1_g, norm2_g=norm2_g, w_in=w_in, diff_qn_g=diff_qn_g, diff_kn_g=diff_kn_g,
                   diff_lam=diff_lam, diff_subln_g=diff_subln_g, gla_wa2=gla_wa2, gla_ba=gla_ba,
                   gla_on_g=gla_on_g, s5_lam_re=s5_lam_re, s5_lam_im=s5_lam_im, s5_log_dt=s5_log_dt,
                   s5_b_re=s5_b_re, s5_b_im=s5_b_im, s5_c_re=s5_c_re, s5_c_im=s5_c_im, s5_d=s5_d,
                   s5_w_glu=s5_w_glu, s5_b_glu=s5_b_glu, w_branch=w_branch, w_out=w_out,
                   w_ffn_gate=w_ffn_gate, w_ffn_up=w_ffn_up, w_ffn_down=w_ffn_down)
    depth = w_mod.shape[0]
    dec_b = c.shape[0]
    cond8 = jnp.concatenate([c_ctx[None, :], c, jnp.zeros((SUBLANES - 1 - dec_b, D_MODEL), F32)], axis=0)
    mod = _modulation(cond8, w_mod, b_mod)
    rope_tabs = _rope_tables(x_sample.shape[1])
    sw = _prepare_shared(weights)
    y_prompt, y_sample = x_prompt, x_sample
    k_list, v_list, gla_list, s5_list = [], [], [], []
    for l in range(depth):
        p = _prepare_layer(l, weights)
        lam_init = 0.8 - 0.6 * math.exp(-0.3 * l)
        y_prompt, (k_l, v_l, g_l, s_l) = _layer(l, y_prompt, mod[l, 0:1][:, None, :], p, sw, lam_init, None, None)
        k_list.append(k_l)
        v_list.append(v_l)
        gla_list.append(g_l)
        s5_list.append(s_l)
        ctx = dict(k=cache_diff_k[:, l], v=cache_diff_v[:, l], gla=state_gla[:, l], s5=state_s5[:, l])
        y_sample, _ = _layer(l, y_sample, mod[l, 1:1 + dec_b][:, None, :], p, sw, lam_init, ctx, rope_tabs)
    return (y_prompt, y_sample, jnp.stack(k_list, axis=1), jnp.stack(v_list, axis=1),
            jnp.stack(gla_list, axis=1), jnp.stack(s5_list, axis=1))
```
